```python
import math, functools
import jax, jax.numpy as jnp
from jax import lax
import numpy as np

D_MODEL = 1024
BATCH = 2
SEQ = 8192
DEPTH = 2
DEC_BATCH = 128
DEC_SEQ = 4
PAST_LEN = 16384
PAGE_SIZE = 128

N_META = 16
N_A_LAYERS = DEPTH // 2
N_B_LAYERS = DEPTH - N_A_LAYERS
N_DENSE_LAYERS = (DEPTH + 1) // 2
N_MOE_LAYERS = DEPTH // 2
RET_HEADS = 4
RET_DK = D_MODEL // RET_HEADS
RET_DV = 2 * RET_DK
RET_QK = RET_HEADS * RET_DK
RET_VW = RET_HEADS * RET_DV
RET_CHUNK = 128
MLA_HEADS = 16
MLA_NOPE = 64
MLA_ROPE = 32
MLA_V = 64
Q_LORA = D_MODEL // 4
KV_LORA = D_MODEL // 4
MLA_SCALE = (MLA_NOPE + MLA_ROPE) ** -0.5
Q_BLOCK = 128
D_FF = 2816
N_EXPERTS = 8
TOP_K = 2
EXPERT_FF = 7 * D_MODEL // 2
ROPE_BASE = 10000.0
LN_EPS = 1e-5
RMS_EPS = 1e-6
ALPHA = (2 * DEPTH) ** 0.25
BETA = (8 * DEPTH) ** -0.25

kernel_name = 'retnet_mla_yoco_moe_step'


def layer_norm(x, g, b):
    xf = x.astype(jnp.float32)
    mu = jnp.mean(xf, -1, keepdims=True)
    var = jnp.mean(jnp.square(xf - mu), -1, keepdims=True)
    return ((xf - mu) * lax.rsqrt(var + LN_EPS) * g + b).astype(x.dtype)


def rms_norm(x, g):
    xf = x.astype(jnp.float32)
    return (xf * lax.rsqrt(jnp.mean(jnp.square(xf), -1, keepdims=True) + RMS_EPS) * g).astype(x.dtype)


def rope(x, pos):
    half = x.shape[-1] // 2
    inv = ROPE_BASE ** (-jnp.arange(half, dtype=jnp.float32) / half)
    ang = pos.astype(jnp.float32)[:, None] * inv[None, :]
    ang = ang.reshape(ang.shape[:1] + (1,) * (x.ndim - 3) + ang.shape[1:])
    cos, sin = jnp.cos(ang), jnp.sin(ang)
    xf = x.astype(jnp.float32)
    x1, x2 = xf[..., :half], xf[..., half:]
    return jnp.concatenate([x1 * cos - x2 * sin, x1 * sin + x2 * cos], axis=-1).astype(x.dtype)


def swiglu(x, w13, w2):
    gate, up = jnp.split(x @ w13, 2, axis=-1)
    return (jax.nn.silu(gate) * up) @ w2


def moe_ffn(x, w_r, b_r, w13, w2):
    logits = (x @ w_r).astype(jnp.float32) + b_r
    top_vals, top_idx = lax.top_k(logits, TOP_K)
    top_w = jax.nn.softmax(top_vals, axis=-1)
    gates = jnp.sum(jax.nn.one_hot(top_idx, N_EXPERTS, dtype=jnp.float32) * top_w[..., None], axis=-2).astype(x.dtype)
    y = jnp.zeros_like(x)
    for e in range(N_EXPERTS):
        y = y + gates[..., e:e + 1] * swiglu(x, w13[e], w2[e])
    return y


def retention_log_decay():
    return jnp.log1p(-jnp.exp2(-5.0 - jnp.arange(RET_HEADS, dtype=jnp.float32)))


def retention_project(h, w_in, pos):
    bsz, s = h.shape[:2]
    q, k, v, g = jnp.split(h @ w_in, [RET_QK, 2 * RET_QK, 2 * RET_QK + RET_VW], axis=-1)
    q = rope(q.reshape(bsz, s, RET_HEADS, RET_DK), pos)
    k = rope(k.reshape(bsz, s, RET_HEADS, RET_DK), pos) * (RET_DK ** -0.5)
    v = v.reshape(bsz, s, RET_HEADS, RET_DV)
    return q, k, v, g


def retention_chunk(s0, q, k, v, log_gamma):
    c = q.shape[1]
    s0 = s0.astype(jnp.float32)
    qf, kf, vf = q.astype(jnp.float32), k.astype(jnp.float32), v.astype(jnp.float32)
    t = jnp.arange(c, dtype=jnp.float32)
    diff = t[:, None] - t[None, :]
    decay = jnp.where(diff >= 0, jnp.exp(jnp.maximum(diff, 0.0)[None] * log_gamma[:, None, None]), 0.0)
    scores = jnp.einsum('bihd,bjhd->bhij', qf, kf) * decay
    o = jnp.einsum('bhij,bjhe->bihe', scores, vf)
    q_in = qf * jnp.exp((t[:, None] + 1.0) * log_gamma[None, :])[None, :, :, None]
    o = o + jnp.einsum('bihd,bhde->bihe', q_in, s0)
    k_out = kf * jnp.exp((c - 1.0 - t)[:, None] * log_gamma[None, :])[None, :, :, None]
    s_new = jnp.exp(c * log_gamma)[None, :, None, None] * s0 + jnp.einsum('bjhd,bjhe->bhde', k_out, vf)
    return s_new, o


def retention_output(o, g, gn_g, w_o):
    mu = jnp.mean(o, -1, keepdims=True)
    var = jnp.mean(jnp.square(o - mu), -1, keepdims=True)
    o = ((o - mu) * lax.rsqrt(var + LN_EPS)).reshape(o.shape[:2] + (RET_VW,)) * gn_g
    return (jax.nn.silu(g) * o.astype(g.dtype)) @ w_o


def retention_prompt(h, w_in, gn_g, w_o, pos):
    bsz = h.shape[0]
    q, k, v, g = retention_project(h, w_in, pos)
    lg = retention_log_decay()
    s0 = jnp.zeros((bsz, RET_HEADS, RET_DK, RET_DV), jnp.float32)
    s_meta, o_meta = retention_chunk(s0, q[:, :N_META], k[:, :N_META], v[:, :N_META], lg)
    n_chunks = (h.shape[1] - N_META) // RET_CHUNK

    def to_chunks(a):
        r = a[:, N_META:]
        return jnp.moveaxis(r.reshape((bsz, n_chunks, RET_CHUNK) + r.shape[2:]), 1, 0)

    s_fin, o_real = lax.scan(lambda s, xs: retention_chunk(s, xs[0], xs[1], xs[2], lg), s_meta,
                             (to_chunks(q), to_chunks(k), to_chunks(v)))
    o_real = jnp.moveaxis(o_real, 0, 1).reshape(bsz, n_chunks * RET_CHUNK, RET_HEADS, RET_DV)
    o = jnp.concatenate([o_meta, o_real], axis=1)
    return retention_output(o, g, gn_g, w_o), s_fin


def retention_sample(h, s0, w_in, gn_g, w_o, pos):
    q, k, v, g = retention_project(h, w_in, pos)
    s_new, o = retention_chunk(s0, q, k, v, retention_log_decay())
    return retention_output(o, g, gn_g, w_o), s_new


def mla_shared_kv(h, w_a, norm_g, pos):
    c, kpe = jnp.split(h @ w_a, [KV_LORA], axis=-1)
    return rms_norm(c, norm_g), rope(kpe, pos)


def mla_queries(h, w_dq, q_norm_g, w_uq, kv_w_b, pos):
    bsz, s = h.shape[:2]
    q = (rms_norm(h @ w_dq, q_norm_g) @ w_uq).reshape(bsz, s, MLA_HEADS, MLA_NOPE + MLA_ROPE)
    q_nope, q_pe = jnp.split(q, [MLA_NOPE], axis=-1)
    q_abs = jnp.einsum('bshn,lhn->bshl', q_nope, kv_w_b[..., :MLA_NOPE])
    return q_abs, rope(q_pe, pos)


def mla_core(q_abs, q_pe, q_pos, c, kpe, k_pos):
    s = (jnp.einsum('bqhl,bkl->bhqk', q_abs, c) + jnp.einsum('bqhr,bkr->bhqk', q_pe, kpe)).astype(jnp.float32) * MLA_SCALE
    mask = k_pos[None, :] <= q_pos[:, None]
    p = jax.nn.softmax(jnp.where(mask[None, None], s, -jnp.inf), axis=-1)
    return jnp.einsum('bhqk,bkl->bqhl', p.astype(c.dtype), c)


def mla_prompt(q_abs, q_pe, q_pos, c, kpe, k_pos):
    bsz, s = q_abs.shape[:2]
    nb = s // Q_BLOCK

    def blk(a):
        return jnp.moveaxis(a.reshape((bsz, nb, Q_BLOCK) + a.shape[2:]), 1, 0)

    o = lax.map(lambda xs: mla_core(xs[0], xs[1], xs[2], c, kpe, k_pos),
                (blk(q_abs), blk(q_pe), q_pos.reshape(nb, Q_BLOCK)))
    return jnp.moveaxis(o, 0, 1).reshape(q_abs.shape)


def mla_output(o_lat, kv_w_b, w_o):
    o = jnp.einsum('bshl,lhv->bshv', o_lat, kv_w_b[..., MLA_NOPE:])
    return o.reshape(o.shape[:2] + (MLA_HEADS * MLA_V,)) @ w_o


def gather_pages(pool, page_table):
    g = pool[page_table]
    return g.reshape(g.shape[0], g.shape[1] * g.shape[2], g.shape[3])


def setup_inputs(seed: int = 0) -> dict:
    key = jax.random.key(seed)
    ks = iter(jax.random.split(key, 32))

    def nrm(shape, scale=1.0):
        return jax.random.normal(next(ks), shape, jnp.float32) * scale

    n_pages = PAST_LEN // PAGE_SIZE
    n_used = DEC_BATCH * n_pages
    n_pool = n_used + n_used // 4
    x_prompt = nrm((BATCH, SEQ, D_MODEL))
    x_sample = nrm((DEC_BATCH, DEC_SEQ, D_MODEL))
    state_ret = nrm((N_A_LAYERS, DEC_BATCH, RET_HEADS, RET_DK, RET_DV), 0.5)
    cache_ckv = nrm((n_pool, PAGE_SIZE, KV_LORA))
    cache_kpe = nrm((n_pool, PAGE_SIZE, MLA_ROPE))
    page_table = jax.random.permutation(next(ks), n_pool)[:n_used].reshape(DEC_BATCH, n_pages).astype(jnp.int32)
    return {
        'x_prompt': x_prompt,
        'x_sample': x_sample,
        'state_ret': state_ret,
        'cache_ckv': cache_ckv,
        'cache_kpe': cache_kpe,
        'page_table': page_table,
        'meta_tokens': nrm((N_META, D_MODEL)),
        'ln_g': 1.0 + nrm((DEPTH, 2, D_MODEL), 0.02),
        'ln_b': nrm((DEPTH, 2, D_MODEL), 0.02),
        'ret_w_in': nrm((N_A_LAYERS, D_MODEL, 2 * RET_QK + 2 * RET_VW), D_MODEL ** -0.5),
        'ret_gn_g': 1.0 + nrm((N_A_LAYERS, RET_VW), 0.02),
        'ret_w_o': nrm((N_A_LAYERS, RET_VW, D_MODEL), BETA * RET_VW ** -0.5),
        'mla_w_dq': nrm((N_B_LAYERS, D_MODEL, Q_LORA), D_MODEL ** -0.5),
        'mla_q_norm': 1.0 + nrm((N_B_LAYERS, Q_LORA), 0.02),
        'mla_w_uq': nrm((N_B_LAYERS, Q_LORA, MLA_HEADS * (MLA_NOPE + MLA_ROPE)), Q_LORA ** -0.5),
        'mla_w_o': nrm((N_B_LAYERS, MLA_HEADS * MLA_V, D_MODEL), BETA * (MLA_HEADS * MLA_V) ** -0.5),
        'kv_w_a': nrm((D_MODEL, KV_LORA + MLA_ROPE), D_MODEL ** -0.5),
        'kv_norm': 1.0 + nrm((KV_LORA,), 0.02),
        'kv_w_b': nrm((KV_LORA, MLA_HEADS, MLA_NOPE + MLA_V), KV_LORA ** -0.5),
        'ffn_w13': nrm((N_DENSE_LAYERS, D_MODEL, 2 * D_FF), D_MODEL ** -0.5),
        'ffn_w2': nrm((N_DENSE_LAYERS, D_FF, D_MODEL), BETA * D_FF ** -0.5),
        'moe_w_r': nrm((N_MOE_LAYERS, D_MODEL, N_EXPERTS), D_MODEL ** -0.5),
        'moe_b_r': nrm((N_MOE_LAYERS, N_EXPERTS), 0.01),
        'moe_w13': nrm((N_MOE_LAYERS, N_EXPERTS, D_MODEL, 2 * EXPERT_FF), D_MODEL ** -0.5),
        'moe_w2': nrm((N_MOE_LAYERS, N_EXPERTS, EXPERT_FF, D_MODEL), BETA * EXPERT_FF ** -0.5),
    }


def reference(x_prompt, x_sample, state_ret, cache_ckv, cache_kpe, page_table, meta_tokens, ln_g, ln_b,
              ret_w_in, ret_gn_g, ret_w_o, mla_w_dq, mla_q_norm, mla_w_uq, mla_w_o, kv_w_a, kv_norm, kv_w_b,
              ffn_w13, ffn_w2, moe_w_r, moe_b_r, moe_w13, moe_w2):
    bsz, s_prompt, d = x_prompt.shape
    n_new = x_sample.shape[1]
    pos_p = jnp.arange(N_META + s_prompt)
    pos_s = PAST_LEN + jnp.arange(n_new)
    kpos_s = jnp.arange(PAST_LEN + n_new)

    h_p = jnp.concatenate([jnp.broadcast_to(meta_tokens[None], (bsz, N_META, d)).astype(x_prompt.dtype), x_prompt], axis=1)
    h_s = x_sample
    ret_states_p, ret_states_s = [], []
    for l in range(DEPTH):
        if l < N_A_LAYERS:
            mix_p, st_p = retention_prompt(h_p, ret_w_in[l], ret_gn_g[l], ret_w_o[l], pos_p)
            mix_s, st_s = retention_sample(h_s, state_ret[l], ret_w_in[l], ret_gn_g[l], ret_w_o[l], pos_s)
            ret_states_p.append(st_p)
            ret_states_s.append(st_s)
        else:
            if l == N_A_LAYERS:
                ckv_p, kpe_p = mla_shared_kv(h_p, kv_w_a, kv_norm, pos_p)
                ckv_s, kpe_s = mla_shared_kv(h_s, kv_w_a, kv_norm, pos_s)
                keys_ckv_s = jnp.concatenate([gather_pages(cache_ckv, page_table), ckv_s], axis=1)
                keys_kpe_s = jnp.concatenate([gather_pages(cache_kpe, page_table), kpe_s], axis=1)
                h_p = h_p[:, N_META:]
            b = l - N_A_LAYERS
            qa_p, qp_p = mla_queries(h_p, mla_w_dq[b], mla_q_norm[b], mla_w_uq[b], kv_w_b, pos_p[N_META:])
            mix_p = mla_output(mla_prompt(qa_p, qp_p, pos_p[N_META:], ckv_p, kpe_p, pos_p), kv_w_b, mla_w_o[b])
            qa_s, qp_s = mla_queries(h_s, mla_w_dq[b], mla_q_norm[b], mla_w_uq[b], kv_w_b, pos_s)
            mix_s = mla_output(mla_core(qa_s, qp_s, pos_s, keys_ckv_s, keys_kpe_s, kpos_s), kv_w_b, mla_w_o[b])
        h_p = layer_norm(ALPHA * h_p + mix_p, ln_g[l, 0], ln_b[l, 0])
        h_s = layer_norm(ALPHA * h_s + mix_s, ln_g[l, 0], ln_b[l, 0])
        f = l // 2
        if l % 2 == 0:
            ffn_p = swiglu(h_p, ffn_w13[f], ffn_w2[f])
            ffn_s = swiglu(h_s, ffn_w13[f], ffn_w2[f])
        else:
            ffn_p = moe_ffn(h_p, moe_w_r[f], moe_b_r[f], moe_w13[f], moe_w2[f])
            ffn_s = moe_ffn(h_s, moe_w_r[f], moe_b_r[f], moe_w13[f], moe_w2[f])
        h_p = layer_norm(ALPHA * h_p + ffn_p, ln_g[l, 1], ln_b[l, 1])
        h_s = layer_norm(ALPHA * h_s + ffn_s, ln_g[l, 1], ln_b[l, 1])

    state_ret_prompt = jnp.stack(ret_states_p)
    state_ret_sample = jnp.stack(ret_states_s)
    return (h_p, h_s, state_ret_prompt, ckv_p, kpe_p, state_ret_sample, ckv_s, kpe_s)
```

```python
import functools
import math

import jax
import jax.numpy as jnp
from jax import lax
from jax.experimental import pallas as pl
from jax.experimental.pallas import tpu as pltpu

F32 = jnp.float32
BF16 = jnp.bfloat16

N_META = 16
TOP_K = 2
ROPE_BASE = 10000.0
LN_EPS = 1e-5
RMS_EPS = 1e-6
PAGE_SIZE = 128
RET_CHUNK = 128

LANE = 128
ROW_TILE = 512
ATTN_TQ = 512
ATTN_TK = 512
FFN_CHUNK = 256
MOE_ROW_TILE = 512
MOE_FF_CHUNK = 512
GATHER_TILE = 256
PAGES_PER_CHUNK = 16
SAMPLE_BATCH_TILE = 4
VMEM_LIMIT = 56 * 1024 * 1024


def _cp(sem, vmem=None):
    return pltpu.CompilerParams(dimension_semantics=sem, vmem_limit_bytes=vmem)


def _layer_norm(z, g, b):
    mu = jnp.mean(z, axis=-1, keepdims=True)
    zc = z - mu
    var = jnp.mean(zc * zc, axis=-1, keepdims=True)
    return zc * lax.rsqrt(var + LN_EPS) * g + b


def _rms_norm(c, g):
    return c * lax.rsqrt(jnp.mean(c * c, axis=-1, keepdims=True) + RMS_EPS) * g


def _silu(x):
    return x * (1.0 / (1.0 + jnp.exp(-x)))


def _dot(a, b):
    return jnp.dot(a, b, preferred_element_type=F32)


def _dot_nt(a, b):
    return lax.dot_general(a, b, (((1,), (1,)), ((), ())), preferred_element_type=F32)


def _dot_tn(a, b):
    return lax.dot_general(a, b, (((0,), (0,)), ((), ())), preferred_element_type=F32)


def _ret_proj_kernel(x_ref, w_ref, cos_ref, sin_ref, o_ref, wbf_ref, *, n_heads, dk, k_scale):
    j = pl.program_id(0)
    i = pl.program_id(1)

    @pl.when(i == 0)
    def _():
        wbf_ref[...] = w_ref[...].astype(BF16)

    y = _dot(x_ref[...].astype(BF16), wbf_ref[...])
    half = dk // 2

    @pl.when(j < 2)
    def _():
        c = cos_ref[...]
        s = sin_ref[...]
        scale = jnp.where(j == 1, k_scale, 1.0).astype(F32)
        for h in range(n_heads):
            x1 = y[:, h * dk:h * dk + half]
            x2 = y[:, h * dk + half:(h + 1) * dk]
            o_ref[:, h * dk:h * dk + half] = ((x1 * c - x2 * s) * scale).astype(BF16)
            o_ref[:, h * dk + half:(h + 1) * dk] = ((x1 * s + x2 * c) * scale).astype(BF16)

    @pl.when(j >= 2)
    def _():
        o_ref[...] = y.astype(BF16)


def _ret_proj(x, w_in, cos_r, sin_r, n_heads, dk):
    r, d = x.shape
    n = w_in.shape[1]
    qk = n_heads * dk
    assert n % qk == 0 and dk // 2 == LANE
    kern = functools.partial(_ret_proj_kernel, n_heads=n_heads, dk=dk, k_scale=dk ** -0.5)
    return pl.pallas_call(
        kern,
        grid=(n // qk, r // ROW_TILE),
        in_specs=[
            pl.BlockSpec((ROW_TILE, d), lambda j, i: (i, 0)),
            pl.BlockSpec((d, qk), lambda j, i: (0, j)),
            pl.BlockSpec((ROW_TILE, LANE), lambda j, i: (i, 0)),
            pl.BlockSpec((ROW_TILE, LANE), lambda j, i: (i, 0)),
        ],
        out_specs=pl.BlockSpec((ROW_TILE, qk), lambda j, i: (i, j)),
        out_shape=jax.ShapeDtypeStruct((r, n), BF16),
        scratch_shapes=[pltpu.VMEM((d, qk), BF16)],
        compiler_params=_cp(("arbitrary", "arbitrary"), VMEM_LIMIT),
    )(x, w_in, cos_r, sin_r)


def _group_norm_gate(o, g, gn):
    mu = jnp.mean(o, axis=-1, keepdims=True)
    oc = o - mu
    var = jnp.mean(oc * oc, axis=-1, keepdims=True)
    return (_silu(g.astype(F32)) * (oc * lax.rsqrt(var + LN_EPS) * gn)).astype(BF16)


def _ret_prompt_kernel(q_ref, k_ref, v_ref, g_ref, qm_ref, km_ref, vm_ref, gm_ref, gn_ref,
                       dec_ref, rdec_ref, cdec_ref, cpow_ref,
                       o_ref, om_ref, st_ref, s_ref, *, n_heads, dk, dv, chunk, n_meta, tile):
    t = pl.program_id(1)
    nt = pl.num_programs(1)

    @pl.when(t == 0)
    def _():
        for h in range(n_heads):
            q = qm_ref[:, h * dk:(h + 1) * dk]
            k = km_ref[:, h * dk:(h + 1) * dk]
            v = vm_ref[:, h * dv:(h + 1) * dv]
            sc = _dot_nt(q, k) * dec_ref[h, :n_meta, :n_meta]
            o = _dot(sc.astype(BF16), v)
            om_ref[:, h * dv:(h + 1) * dv] = _group_norm_gate(
                o, gm_ref[:, h * dv:(h + 1) * dv], gn_ref[:, h * dv:(h + 1) * dv])
            k_out = (k.astype(F32) * cdec_ref[h, chunk - n_meta:, :]).astype(BF16)
            s_ref[h] = _dot_tn(k_out, v)

    for h in range(n_heads):
        for c in range(tile // chunk):
            rows = slice(c * chunk, (c + 1) * chunk)
            q = q_ref[rows, h * dk:(h + 1) * dk]
            k = k_ref[rows, h * dk:(h + 1) * dk]
            v = v_ref[rows, h * dv:(h + 1) * dv]
            s_old = s_ref[h]
            sc = _dot_nt(q, k) * dec_ref[h]
            q_in = (q.astype(F32) * rdec_ref[h]).astype(BF16)
            o = _dot(sc.astype(BF16), v) + _dot(q_in, s_old.astype(BF16))
            k_out = (k.astype(F32) * cdec_ref[h]).astype(BF16)
            s_ref[h] = cpow_ref[h] * s_old + _dot_tn(k_out, v)
            o_ref[rows, h * dv:(h + 1) * dv] = _group_norm_gate(
                o, g_ref[rows, h * dv:(h + 1) * dv], gn_ref[:, h * dv:(h + 1) * dv])

    @pl.when(t == nt - 1)
    def _():
        st_ref[0] = s_ref[...]


def _ret_prompt(qkvg, gn_g, tabs, bsz, seq, meta_row0, n_heads, dk, dv):
    qk, vw = n_heads * dk, n_heads * dv
    tile = ROW_TILE
    assert seq % tile == 0 and tile % RET_CHUNK == 0 and meta_row0 % N_META == 0 and vw == 2 * qk
    nt = seq // tile
    mb = meta_row0 // N_META
    dec, rdec, cdec, cpow = tabs
    kern = functools.partial(_ret_prompt_kernel, n_heads=n_heads, dk=dk, dv=dv, chunk=RET_CHUNK,
                             n_meta=N_META, tile=tile)
    full = lambda *shape: pl.BlockSpec(shape, lambda b, t: (0,) * len(shape))
    return pl.pallas_call(
        kern,
        grid=(bsz, nt),
        in_specs=[
            pl.BlockSpec((tile, qk), lambda b, t: (b * nt + t, 0)),
            pl.BlockSpec((tile, qk), lambda b, t: (b * nt + t, 1)),
            pl.BlockSpec((tile, vw), lambda b, t: (b * nt + t, 1)),
            pl.BlockSpec((tile, vw), lambda b, t: (b * nt + t, 2)),
            pl.BlockSpec((N_META, qk), lambda b, t: (mb, 0)),
            pl.BlockSpec((N_META, qk), lambda b, t: (mb, 1)),
            pl.BlockSpec((N_META, vw), lambda b, t: (mb, 1)),
            pl.BlockSpec((N_META, vw), lambda b, t: (mb, 2)),
            full(1, vw),
            full(n_heads, RET_CHUNK, RET_CHUNK),
            full(n_heads, RET_CHUNK, dk),
            full(n_heads, RET_CHUNK, dk),
            full(n_heads, 1, 1),
        ],
        out_specs=[
            pl.BlockSpec((tile, vw), lambda b, t: (b * nt + t, 0)),
            pl.BlockSpec((N_META, vw), lambda b, t: (0, 0)),
            pl.BlockSpec((1, n_heads, dk, dv), lambda b, t: (b, 0, 0, 0)),
        ],
        out_shape=[
            jax.ShapeDtypeStruct((bsz * seq, vw), BF16),
            jax.ShapeDtypeStruct((N_META, vw), BF16),
            jax.ShapeDtypeStruct((bsz, n_heads, dk, dv), F32),
        ],
        scratch_shapes=[pltpu.VMEM((n_heads, dk, dv), F32)],
        compiler_params=_cp(("arbitrary", "arbitrary"), VMEM_LIMIT),
    )(qkvg, qkvg, qkvg, qkvg, qkvg, qkvg, qkvg, qkvg, gn_g, dec, rdec, cdec, cpow)


def _ret_sample_kernel(q_ref, k_ref, v_ref, g_ref, s0_ref, gn_ref, dec_ref, rdec_ref, cdec_ref, cpow_ref,
                       o_ref, st_ref, *, n_heads, dk, dv, nb, dec_seq):
    rows = nb * dec_seq
    row_b = lax.broadcasted_iota(jnp.int32, (rows, 1), 0) // dec_seq
    for h in range(n_heads):
        q = q_ref[:, h * dk:(h + 1) * dk]
        k = k_ref[:, h * dk:(h + 1) * dk]
        v = v_ref[:, h * dv:(h + 1) * dv]
        sc = _dot_nt(q, k) * dec_ref[h]
        o = _dot(sc.astype(BF16), v)
        q_in = q.astype(F32) * rdec_ref[h]
        k_out = k.astype(F32) * cdec_ref[h]
        for b in range(nb):
            s_old = s0_ref[0, b, h]
            sel = row_b == b
            o = o + _dot(jnp.where(sel, q_in, 0.0).astype(BF16), s_old.astype(BF16))
            st_ref[0, b, h] = cpow_ref[h] * s_old + _dot_tn(jnp.where(sel, k_out, 0.0).astype(BF16), v)
        o_ref[:, h * dv:(h + 1) * dv] = _group_norm_gate(
            o, g_ref[:, h * dv:(h + 1) * dv], gn_ref[:, h * dv:(h + 1) * dv])


def _ret_sample(qkvg, state, gn_g, tabs, row0, dec_b, dec_seq, n_heads, dk, dv):
    qk, vw = n_heads * dk, n_heads * dv
    nb = SAMPLE_BATCH_TILE
    rows = nb * dec_seq
    assert dec_b % nb == 0 and row0 % rows == 0 and rows % 16 == 0
    rb = row0 // rows
    dec, rdec, cdec, cpow = tabs
    state5 = state.reshape(dec_b // nb, nb, n_heads, dk, dv)
    kern = functools.partial(_ret_sample_kernel, n_heads=n_heads, dk=dk, dv=dv, nb=nb, dec_seq=dec_seq)
    full = lambda *shape: pl.BlockSpec(shape, lambda i: (0,) * len(shape))
    out, st = pl.pallas_call(
        kern,
        grid=(dec_b // nb,),
        in_specs=[
            pl.BlockSpec((rows, qk), lambda i: (rb + i, 0)),
            pl.BlockSpec((rows, qk), lambda i: (rb + i, 1)),
            pl.BlockSpec((rows, vw), lambda i: (rb + i, 1)),
            pl.BlockSpec((rows, vw), lambda i: (rb + i, 2)),
            pl.BlockSpec((1, nb, n_heads, dk, dv), lambda i: (i, 0, 0, 0, 0)),
            full(1, vw),
            full(n_heads, rows, rows),
            full(n_heads, rows, dk),
            full(n_heads, rows, dk),
            full(n_heads, 1, 1),
        ],
        out_specs=[
            pl.BlockSpec((rows, vw), lambda i: (i, 0)),
            pl.BlockSpec((1, nb, n_heads, dk, dv), lambda i: (i, 0, 0, 0, 0)),
        ],
        out_shape=[
            jax.ShapeDtypeStruct((dec_b * dec_seq, vw), BF16),
            jax.ShapeDtypeStruct(state5.shape, F32),
        ],
        compiler_params=_cp(("arbitrary",), VMEM_LIMIT),
    )(qkvg, qkvg, qkvg, qkvg, state5, gn_g, dec, rdec, cdec, cpow)
    return out, st.reshape(dec_b, n_heads, dk, dv)


def _decay_tables(n_heads, dk, chunk, group):
    lg = jnp.log1p(-jnp.exp2(-5.0 - jnp.arange(n_heads, dtype=F32)))
    r = jnp.arange(chunk)
    t = (r % group).astype(F32)
    diff = t[:, None] - t[None, :]
    same = (r[:, None] // group) == (r[None, :] // group)
    dec = jnp.where(same & (diff >= 0), jnp.exp(jnp.maximum(diff, 0.0)[None] * lg[:, None, None]), 0.0)
    rdec = jnp.exp((t[None, :] + 1.0) * lg[:, None])
    cdec = jnp.exp((group - 1.0 - t)[None, :] * lg[:, None])
    cpow = jnp.exp(group * lg)
    bc = lambda a: jnp.broadcast_to(a[:, :, None], (n_heads, chunk, dk))
    return dec, bc(rdec), bc(cdec), cpow.reshape(n_heads, 1, 1)


def _proj_ln_kernel(a_ref, w_ref, h_ref, g_ref, b_ref, o_ref, obf_ref, wbf_ref, *, alpha):
    @pl.when(pl.program_id(0) == 0)
    def _():
        wbf_ref[...] = w_ref[...].astype(BF16)

    z = alpha * h_ref[...] + _dot(a_ref[...], wbf_ref[...])
    y = _layer_norm(z, g_ref[...], b_ref[...])
    o_ref[...] = y
    obf_ref[...] = y.astype(BF16)


def _proj_ln(a, w, h, g, b, alpha):
    r, k = a.shape
    d = w.shape[1]
    row = lambda n: pl.BlockSpec((ROW_TILE, n), lambda i: (i, 0))
    full = lambda *shape: pl.BlockSpec(shape, lambda i: (0,) * len(shape))
    return pl.pallas_call(
        functools.partial(_proj_ln_kernel, alpha=alpha),
        grid=(r // ROW_TILE,),
        in_specs=[row(k), full(k, d), row(d), full(1, d), full(1, d)],
        out_specs=[row(d), row(d)],
        out_shape=[jax.ShapeDtypeStruct((r, d), F32), jax.ShapeDtypeStruct((r, d), BF16)],
        scratch_shapes=[pltpu.VMEM((k, d), BF16)],
        compiler_params=_cp(("arbitrary",), VMEM_LIMIT),
    )(a, w, h, g, b)


def _ffn_ln_kernel(x_ref, w13_ref, w2_ref, h_ref, g_ref, b_ref, o_ref, obf_ref, *, alpha, d_ff, chunk):
    x = x_ref[...]
    acc = alpha * h_ref[...]
    for c in range(d_ff // chunk):
        gate = _dot(x, w13_ref[:, c * chunk:(c + 1) * chunk])
        up = _dot(x, w13_ref[:, d_ff + c * chunk:d_ff + (c + 1) * chunk])
        acc = acc + _dot((_silu(gate) * up).astype(BF16), w2_ref[c * chunk:(c + 1) * chunk, :])
    y = _layer_norm(acc, g_ref[...], b_ref[...])
    o_ref[...] = y
    obf_ref[...] = y.astype(BF16)


def _ffn_ln(x, w13, w2, h, g, b, alpha):
    r, d = x.shape
    d_ff = w2.shape[0]
    assert d_ff % FFN_CHUNK == 0
    row = lambda n: pl.BlockSpec((ROW_TILE, n), lambda i: (i, 0))
    full = lambda *shape: pl.BlockSpec(shape, lambda i: (0,) * len(shape))
    return pl.pallas_call(
        functools.partial(_ffn_ln_kernel, alpha=alpha, d_ff=d_ff, chunk=FFN_CHUNK),
        grid=(r // ROW_TILE,),
        in_specs=[row(d), full(d, 2 * d_ff), full(d_ff, d), row(d), full(1, d), full(1, d)],
        out_specs=[row(d), row(d)],
        out_shape=[jax.ShapeDtypeStruct((r, d), F32), jax.ShapeDtypeStruct((r, d), BF16)],
        compiler_params=_cp(("arbitrary",), VMEM_LIMIT),
    )(x, w13, w2, h, g, b)


def _kv_kernel(x_ref, wkv_ref, g_ref, ca_ref, sa_ref, cb_ref, sb_ref, wuk_ref, wuv_ref,
               ckv_ref, kpe_ref, k_ref, v_ref, *, lora, rope, n_heads):
    y = _dot(x_ref[...], wkv_ref[...])
    cn = _rms_norm(y[:, :lora], g_ref[...])
    ckv_ref[...] = cn
    kpe_a = y[:, lora:lora + LANE] * ca_ref[...] + y[:, lora + LANE:lora + 2 * LANE] * sa_ref[...]
    kpe_ref[...] = kpe_a[:, :rope]
    kpe_b = y[:, lora + 2 * LANE:lora + 3 * LANE] * cb_ref[...] + y[:, lora + 3 * LANE:lora + 4 * LANE] * sb_ref[...]
    cb = cn.astype(BF16)
    kn = _dot(cb, wuk_ref[...])
    for h in range(n_heads):
        k_ref[:, h * LANE:(h + 1) * LANE] = (kn[:, h * LANE:(h + 1) * LANE] + kpe_b).astype(BF16)
    v_ref[...] = _dot(cb, wuv_ref[...]).astype(BF16)


def _kv_proj(x, wkv, g, tabs, wuk, wuv, lora, rope, n_heads):
    r, d = x.shape
    nv = wuv.shape[1]
    row = lambda n: pl.BlockSpec((ROW_TILE, n), lambda i: (i, 0))
    full = lambda *shape: pl.BlockSpec(shape, lambda i: (0,) * len(shape))
    return pl.pallas_call(
        functools.partial(_kv_kernel, lora=lora, rope=rope, n_heads=n_heads),
        grid=(r // ROW_TILE,),
        in_specs=[row(d), full(*wkv.shape), full(1, lora), row(LANE), row(LANE), row(LANE), row(LANE),
                  full(*wuk.shape), full(*wuv.shape)],
        out_specs=[row(lora), row(rope), row(n_heads * LANE), row(nv)],
        out_shape=[jax.ShapeDtypeStruct((r, lora), F32), jax.ShapeDtypeStruct((r, rope), F32),
                   jax.ShapeDtypeStruct((r, n_heads * LANE), BF16), jax.ShapeDtypeStruct((r, nv), BF16)],
        compiler_params=_cp(("arbitrary",), VMEM_LIMIT),
    )(x, wkv, g, *tabs, wuk, wuv)


def _q_kernel(x_ref, wdq_ref, g_ref, w1_ref, w2_ref, c_ref, s_ref, q_ref, *, n_heads):
    qc = _rms_norm(_dot(x_ref[...], wdq_ref[...]), g_ref[...]).astype(BF16)
    a = _dot(qc, w1_ref[...])
    b = _dot(qc, w2_ref[...])
    c = c_ref[...]
    s = s_ref[...]
    for h in range(n_heads):
        cols = slice(h * LANE, (h + 1) * LANE)
        q_ref[:, cols] = (a[:, cols] * c + b[:, cols] * s).astype(BF16)


def _q_proj(x, wdq, g, w1, w2, c_tab, s_tab, n_heads):
    r, d = x.shape
    ql = wdq.shape[1]
    row = lambda n: pl.BlockSpec((ROW_TILE, n), lambda i: (i, 0))
    full = lambda *shape: pl.BlockSpec(shape, lambda i: (0,) * len(shape))
    return pl.pallas_call(
        functools.partial(_q_kernel, n_heads=n_heads),
        grid=(r // ROW_TILE,),
        in_specs=[row(d), full(d, ql), full(1, ql), full(*w1.shape), full(*w2.shape), row(LANE), row(LANE)],
        out_specs=row(n_heads * LANE),
        out_shape=jax.ShapeDtypeStruct((r, n_heads * LANE), BF16),
        compiler_params=_cp(("arbitrary",), VMEM_LIMIT),
    )(x, wdq, g, w1, w2, c_tab, s_tab)


def _q_sample_kernel(x_ref, wdq_ref, g_ref, w1_ref, wukt_ref, wpe_ref, wrot_ref, c_ref, s_ref,
                     qabs_ref, qpe_ref, *, n_heads, lora, scale):
    qc = _rms_norm(_dot(x_ref[...], wdq_ref[...]), g_ref[...]).astype(BF16)
    qn = (_dot(qc, w1_ref[...]) * scale).astype(BF16)
    for h in range(n_heads):
        qabs_ref[:, h * lora:(h + 1) * lora] = _dot(qn[:, h * LANE:(h + 1) * LANE], wukt_ref[h]).astype(BF16)
    qpe = _dot(qc, wpe_ref[...]) * c_ref[...] + _dot(qc, wrot_ref[...]) * s_ref[...]
    qpe_ref[...] = (qpe * scale).astype(BF16)


def _q_sample(x, wdq, g, w1, wukt, wpe, wrot, c_tab, s_tab, n_heads, lora, scale):
    ns = x.shape[0]
    args = (x, wdq, g, w1, wukt, wpe, wrot, c_tab, s_tab)
    full = lambda a: pl.BlockSpec(a.shape, lambda i, nd=a.ndim: (0,) * nd)
    return pl.pallas_call(
        functools.partial(_q_sample_kernel, n_heads=n_heads, lora=lora, scale=scale),
        grid=(1,),
        in_specs=[full(a) for a in args],
        out_specs=[pl.BlockSpec((ns, n_heads * lora), lambda i: (0, 0)),
                   pl.BlockSpec((ns, wpe.shape[1]), lambda i: (0, 0))],
        out_shape=[jax.ShapeDtypeStruct((ns, n_heads * lora), BF16),
                   jax.ShapeDtypeStruct((ns, wpe.shape[1]), BF16)],
        compiler_params=_cp(("arbitrary",), VMEM_LIMIT),
    )(*args)


def _attn_update(carry, s, v):
    m, l, acc = carry
    m_new = jnp.maximum(m, jnp.max(s, axis=-1, keepdims=True))
    alpha = jnp.exp(m - m_new)
    p = jnp.exp(s - m_new)
    l = alpha * l + jnp.sum(p, axis=-1, keepdims=True)
    acc = alpha * acc + _dot(p.astype(BF16), v)
    return m_new, l, acc


def _attn_prompt_kernel(q_ref, k_ref, v_ref, km_ref, vm_ref, o_ref, *, tq, tk, n_meta):
    i = pl.program_id(2)
    meta_col = lax.broadcasted_iota(jnp.int32, (tq, LANE), 1)
    row = lax.broadcasted_iota(jnp.int32, (tq, tk), 0)
    col = lax.broadcasted_iota(jnp.int32, (tq, tk), 1)
    outs = []
    for e in range(2):
        q = q_ref[:, e * LANE:(e + 1) * LANE]
        s = jnp.where(meta_col < n_meta, _dot_nt(q, km_ref[:, e * LANE:(e + 1) * LANE]), -jnp.inf)
        m = jnp.max(s, axis=-1, keepdims=True)
        p = jnp.exp(s - m)
        carry = (m, jnp.sum(p, axis=-1, keepdims=True), _dot(p.astype(BF16), vm_ref[...]))

        def body(j, carry, q=q, e=e):
            off = pl.multiple_of(j * tk, tk)
            s = _dot_nt(q, k_ref[pl.ds(off, tk), e * LANE:(e + 1) * LANE])
            return _attn_update(carry, s, v_ref[pl.ds(off, tk), :])

        carry = lax.fori_loop(0, i, body, carry)
        off = pl.multiple_of(i * tk, tk)
        s = _dot_nt(q, k_ref[pl.ds(off, tk), e * LANE:(e + 1) * LANE])
        s = jnp.where(col <= row, s, -jnp.inf)
        m, l, acc = _attn_update(carry, s, v_ref[pl.ds(off, tk), :])
        outs.append(acc * (1.0 / l))
    half_col = lax.broadcasted_iota(jnp.int32, (tq, LANE), 1)
    o_ref[...] = jnp.where(half_col < LANE // 2, outs[0], outs[1]).astype(BF16)


def _attn_prompt(q, k, v, bsz, seq, meta_row0, n_heads):
    tq, tk = ATTN_TQ, ATTN_TK
    assert tq == tk and seq % tq == 0 and meta_row0 % LANE == 0 and n_heads % 2 == 0
    assert v.shape[1] == n_heads * LANE // 2
    nq = seq // tq
    mb = meta_row0 // LANE
    return pl.pallas_call(
        functools.partial(_attn_prompt_kernel, tq=tq, tk=tk, n_meta=N_META),
        grid=(bsz, n_heads // 2, nq),
        in_specs=[
            pl.BlockSpec((tq, 2 * LANE), lambda b, hp, i: (b * nq + i, hp)),
            pl.BlockSpec((seq, 2 * LANE), lambda b, hp, i: (b, hp)),
            pl.BlockSpec((seq, LANE), lambda b, hp, i: (b, hp)),
            pl.BlockSpec((LANE, 2 * LANE), lambda b, hp, i: (mb, hp)),
            pl.BlockSpec((LANE, LANE), lambda b, hp, i: (mb, hp)),
        ],
        out_specs=pl.BlockSpec((tq, LANE), lambda b, hp, i: (b * nq + i, hp)),
        out_shape=jax.ShapeDtypeStruct((bsz * seq, n_heads * LANE // 2), BF16),
        compiler_params=_cp(("arbitrary", "arbitrary", "arbitrary"), VMEM_LIMIT),
    )(q, k, v, k, v)


def _attn_paged_kernel(pt_ref, qabs_ref, qpe_ref, cnew_ref, pnew_ref, ckv_hbm, kpe_hbm, o_ref,
                       cbuf, pbuf, sem, m_ref, l_ref, acc_ref, *, ppc, n_chunks, n_pages, n_heads, dec_seq):
    b = pl.program_id(0)
    c = pl.program_id(1)
    step = b * n_chunks + c
    n_steps = pl.num_programs(0) * n_chunks
    slot = step % 2

    def copies(st, sl):
        base = st * ppc
        out = []
        for p in range(ppc):
            page = pt_ref[base + p]
            out.append(pltpu.make_async_copy(ckv_hbm.at[page], cbuf.at[sl, pl.ds(p * PAGE_SIZE, PAGE_SIZE), :],
                                             sem.at[0, sl]))
            out.append(pltpu.make_async_copy(kpe_hbm.at[page], pbuf.at[sl, pl.ds(p * PAGE_SIZE, PAGE_SIZE), :],
                                             sem.at[1, sl]))
        return out

    @pl.when(step == 0)
    def _():
        for cp in copies(step, slot):
            cp.start()

    @pl.when(step + 1 < n_steps)
    def _():
        for cp in copies(step + 1, 1 - slot):
            cp.start()

    @pl.when(c == 0)
    def _():
        m_ref[...] = jnp.full(m_ref.shape, -jnp.inf, F32)
        l_ref[...] = jnp.zeros(l_ref.shape, F32)
        acc_ref[...] = jnp.zeros(acc_ref.shape, F32)

    for cp in copies(step, slot):
        cp.wait()

    qa = qabs_ref[0]
    qp = qpe_ref[0]
    kc = cbuf[slot].astype(BF16)
    kp = pbuf[slot].astype(BF16)
    s = _dot_nt(qa, kc) + _dot_nt(qp, kp)
    m, l, acc = _attn_update((m_ref[...], l_ref[...], acc_ref[...]), s, kc)
    m_ref[...] = m
    l_ref[...] = l
    acc_ref[...] = acc

    @pl.when(c == n_chunks - 1)
    def _():
        kn = cnew_ref[0]
        s = _dot_nt(qa, kn) + _dot_nt(qp, pnew_ref[0])
        rows = s.shape[0]
        tok = lax.broadcasted_iota(jnp.int32, (rows, LANE), 0) // n_heads
        col = lax.broadcasted_iota(jnp.int32, (rows, LANE), 1)
        s = jnp.where((col <= tok) & (col < dec_seq), s, -jnp.inf)
        m2, l2, acc2 = _attn_update((m_ref[...], l_ref[...], acc_ref[...]), s, kn)
        o_ref[0] = acc2 * (1.0 / l2)


def _attn_paged(page_table, qabs, qpe, cnew, pnew, cache_ckv, cache_kpe, n_heads, dec_seq):
    dec_b, n_pages = page_table.shape
    ppc = min(PAGES_PER_CHUNK, n_pages)
    assert n_pages % ppc == 0 and cache_ckv.shape[1] == PAGE_SIZE
    n_chunks = n_pages // ppc
    rows, lora = qabs.shape[1], qabs.shape[2]
    rope = qpe.shape[2]
    kern = functools.partial(_attn_paged_kernel, ppc=ppc, n_chunks=n_chunks, n_pages=n_pages,
                             n_heads=n_heads, dec_seq=dec_seq)
    grid_spec = pltpu.PrefetchScalarGridSpec(
        num_scalar_prefetch=1,
        grid=(dec_b, n_chunks),
        in_specs=[
            pl.BlockSpec((1, rows, lora), lambda b, c, pt: (b, 0, 0)),
            pl.BlockSpec((1, rows, rope), lambda b, c, pt: (b, 0, 0)),
            pl.BlockSpec((1, LANE, lora), lambda b, c, pt: (b, 0, 0)),
            pl.BlockSpec((1, LANE, rope), lambda b, c, pt: (b, 0, 0)),
            pl.BlockSpec(memory_space=pl.ANY),
            pl.BlockSpec(memory_space=pl.ANY),
        ],
        out_specs=pl.BlockSpec((1, rows, lora), lambda b, c, pt: (b, 0, 0)),
        scratch_shapes=[
            pltpu.VMEM((2, ppc * PAGE_SIZE, lora), F32),
            pltpu.VMEM((2, ppc * PAGE_SIZE, rope), F32),
            pltpu.SemaphoreType.DMA((2, 2)),
            pltpu.VMEM((rows, 1), F32),
            pltpu.VMEM((rows, 1), F32),
            pltpu.VMEM((rows, lora), F32),
        ],
    )
    return pl.pallas_call(
        kern,
        grid_spec=grid_spec,
        out_shape=jax.ShapeDtypeStruct((dec_b, rows, lora), F32),
        compiler_params=_cp(("arbitrary", "arbitrary"), VMEM_LIMIT),
    )(page_table.reshape(-1), qabs, qpe, cnew, pnew, cache_ckv, cache_kpe)


def _uv_sample_kernel(o_ref, w_ref, out_ref, *, n_heads, lora):
    for p in range(n_heads // 2):
        acc = _dot(o_ref[:, (2 * p) * lora:(2 * p + 1) * lora].astype(BF16), w_ref[2 * p])
        acc = acc + _dot(o_ref[:, (2 * p + 1) * lora:(2 * p + 2) * lora].astype(BF16), w_ref[2 * p + 1])
        out_ref[:, p * LANE:(p + 1) * LANE] = acc.astype(BF16)


def _uv_sample(o_lat, wuv_pad, n_heads, lora):
    ns = o_lat.shape[0]
    nv = n_heads * LANE // 2
    return pl.pallas_call(
        functools.partial(_uv_sample_kernel, n_heads=n_heads, lora=lora),
        grid=(1,),
        in_specs=[pl.BlockSpec(o_lat.shape, lambda i: (0, 0)), pl.BlockSpec(wuv_pad.shape, lambda i: (0, 0, 0))],
        out_specs=pl.BlockSpec((ns, nv), lambda i: (0, 0)),
        out_shape=jax.ShapeDtypeStruct((ns, nv), BF16),
        compiler_params=_cp(("arbitrary",), VMEM_LIMIT),
    )(o_lat, wuv_pad)


def _router_kernel(x_ref, w_ref, b_ref, o_ref, *, n_experts):
    logits = _dot(x_ref[...], w_ref[...]) + b_ref[...]
    lane = lax.broadcasted_iota(jnp.int32, logits.shape, 1).astype(F32)
    logits = jnp.where(lane < n_experts, logits, -jnp.inf)
    m1 = jnp.max(logits, axis=-1, keepdims=True)
    i1 = jnp.min(jnp.where(logits == m1, lane, float(LANE)), axis=-1, keepdims=True)
    rest = jnp.where(lane == i1, -jnp.inf, logits)
    m2 = jnp.max(rest, axis=-1, keepdims=True)
    i2 = jnp.min(jnp.where(rest == m2, lane, float(LANE)), axis=-1, keepdims=True)
    e2 = jnp.exp(m2 - m1)
    den = 1.0 + e2
    g1 = 1.0 / den
    g2 = e2 / den
    o_ref[...] = jnp.where(lane == 0, i1, jnp.where(lane == 1, i2, jnp.where(lane == 2, g1, jnp.where(lane == 3, g2, 0.0))))


def _router(x, w_pad, b_pad, n_experts):
    r, d = x.shape
    return pl.pallas_call(
        functools.partial(_router_kernel, n_experts=n_experts),
        grid=(r // ROW_TILE,),
        in_specs=[pl.BlockSpec((ROW_TILE, d), lambda i: (i, 0)), pl.BlockSpec((d, LANE), lambda i: (0, 0)),
                  pl.BlockSpec((1, LANE), lambda i: (0, 0))],
        out_specs=pl.BlockSpec((ROW_TILE, LANE), lambda i: (i, 0)),
        out_shape=jax.ShapeDtypeStruct((r, LANE), F32),
        compiler_params=_cp(("arbitrary",), VMEM_LIMIT),
    )(x, w_pad, b_pad)


def _gather_rows_kernel(idx_ref, src_hbm, o_ref, buf, sem, *, tile):
    i = pl.program_id(0)
    n = pl.num_programs(0)
    slot = i % 2

    def issue(t, sl):
        def body(r, carry):
            row = idx_ref[t * tile + r]
            pltpu.make_async_copy(src_hbm.at[pl.ds(row, 1), :], buf.at[sl, pl.ds(r, 1), :], sem.at[sl]).start()
            return carry
        lax.fori_loop(0, tile, body, 0)

    @pl.when(i == 0)
    def _():
        issue(i, slot)

    @pl.when(i + 1 < n)
    def _():
        issue(i + 1, 1 - slot)

    def wait_body(r, carry):
        pltpu.make_async_copy(src_hbm.at[pl.ds(0, 1), :], buf.at[slot, pl.ds(r, 1), :], sem.at[slot]).wait()
        return carry
    lax.fori_loop(0, tile, wait_body, 0)
    o_ref[...] = buf[slot].astype(o_ref.dtype)


def _gather_rows(idx, src, out_dtype):
    n = idx.shape[0]
    d = src.shape[1]
    tile = GATHER_TILE
    assert n % tile == 0
    grid_spec = pltpu.PrefetchScalarGridSpec(
        num_scalar_prefetch=1,
        grid=(n // tile,),
        in_specs=[pl.BlockSpec(memory_space=pl.ANY)],
        out_specs=pl.BlockSpec((tile, d), lambda i, idx: (i, 0)),
        scratch_shapes=[pltpu.VMEM((2, tile, d), src.dtype), pltpu.SemaphoreType.DMA((2,))],
    )
    return pl.pallas_call(
        functools.partial(_gather_rows_kernel, tile=tile),
        grid_spec=grid_spec,
        out_shape=jax.ShapeDtypeStruct((n, d), out_dtype),
        compiler_params=_cp(("arbitrary",), VMEM_LIMIT),
    )(idx, src)


def _moe_ffn_kernel(te_ref, nu_ref, x_ref, wg_ref, wu_ref, w2_ref, gs_ref, o_ref, acc_ref):
    t = pl.program_id(0)
    f = pl.program_id(1)
    nf = pl.num_programs(1)
    used = t < nu_ref[0]

    @pl.when(used)
    def _():
        x = x_ref[...]
        hcol = (_silu(_dot(x, wg_ref[0])) * _dot(x, wu_ref[0])).astype(BF16)
        y = _dot(hcol, w2_ref[0])

        @pl.when(f == 0)
        def _():
            acc_ref[...] = y

        @pl.when(f > 0)
        def _():
            acc_ref[...] += y

    @pl.when(f == nf - 1)
    def _():
        o_ref[...] = jnp.where(used, acc_ref[...] * gs_ref[...], 0.0)


def _moe_ffn(tile_expert, n_used, x_sorted, w13, w2, gate_sorted):
    n, d = x_sorted.shape
    n_exp, eff = w2.shape[0], w2.shape[1]
    tm, tf = MOE_ROW_TILE, min(MOE_FF_CHUNK, eff)
    assert n % tm == 0 and eff % tf == 0
    nf = eff // tf

    def widx(base):
        def index_map(t, f, te, nu):
            live = t < nu[0]
            return (te[t], 0, jnp.where(live, f, nf - 1) + base)
        return index_map

    def w2idx(t, f, te, nu):
        return (te[t], jnp.where(t < nu[0], f, nf - 1), 0)

    grid_spec = pltpu.PrefetchScalarGridSpec(
        num_scalar_prefetch=2,
        grid=(n // tm, nf),
        in_specs=[
            pl.BlockSpec((tm, d), lambda t, f, te, nu: (t, 0)),
            pl.BlockSpec((1, d, tf), widx(0)),
            pl.BlockSpec((1, d, tf), widx(nf)),
            pl.BlockSpec((1, tf, d), w2idx),
            pl.BlockSpec((tm, 1), lambda t, f, te, nu: (t, 0)),
        ],
        out_specs=pl.BlockSpec((tm, d), lambda t, f, te, nu: (t, 0)),
        scratch_shapes=[pltpu.VMEM((tm, d), F32)],
    )
    return pl.pallas_call(
        _moe_ffn_kernel,
        grid_spec=grid_spec,
        out_shape=jax.ShapeDtypeStruct((n, d), F32),
        compiler_params=_cp(("arbitrary", "arbitrary"), VMEM_LIMIT),
    )(tile_expert, n_used, x_sorted, w13, w13, w2, gate_sorted)


def _combine_ln_kernel(p1_ref, p2_ref, y_hbm, h_ref, g_ref, b_ref, o_ref, buf, sem, *, tile, alpha):
    i = pl.program_id(0)
    n = pl.num_programs(0)
    slot = i % 2

    def issue(t, sl):
        def body(r, carry):
            a = p1_ref[t * tile + r]
            b = p2_ref[t * tile + r]
            pltpu.make_async_copy(y_hbm.at[pl.ds(a, 1), :], buf.at[sl, 0, pl.ds(r, 1), :], sem.at[sl]).start()
            pltpu.make_async_copy(y_hbm.at[pl.ds(b, 1), :], buf.at[sl, 1, pl.ds(r, 1), :], sem.at[sl]).start()
            return carry
        lax.fori_loop(0, tile, body, 0)

    @pl.when(i == 0)
    def _():
        issue(i, slot)

    @pl.when(i + 1 < n)
    def _():
        issue(i + 1, 1 - slot)

    def wait_body(r, carry):
        pltpu.make_async_copy(y_hbm.at[pl.ds(0, 1), :], buf.at[slot, 0, pl.ds(r, 1), :], sem.at[slot]).wait()
        pltpu.make_async_copy(y_hbm.at[pl.ds(0, 1), :], buf.at[slot, 1, pl.ds(r, 1), :], sem.at[slot]).wait()
        return carry
    lax.fori_loop(0, tile, wait_body, 0)
    z = alpha * h_ref[...] + (buf[slot, 0] + buf[slot, 1])
    o_ref[...] = _layer_norm(z, g_ref[...], b_ref[...])


def _combine_ln(pos1, pos2, y_sorted, h, g, b, alpha):
    r, d = h.shape
    tile = GATHER_TILE
    assert r % tile == 0
    grid_spec = pltpu.PrefetchScalarGridSpec(
        num_scalar_prefetch=2,
        grid=(r // tile,),
        in_specs=[
            pl.BlockSpec(memory_space=pl.ANY),
            pl.BlockSpec((tile, d), lambda i, p1, p2: (i, 0)),
            pl.BlockSpec((1, d), lambda i, p1, p2: (0, 0)),
            pl.BlockSpec((1, d), lambda i, p1, p2: (0, 0)),
        ],
        out_specs=pl.BlockSpec((tile, d), lambda i, p1, p2: (i, 0)),
        scratch_shapes=[pltpu.VMEM((2, 2, tile, d), F32), pltpu.SemaphoreType.DMA((2,))],
    )
    return pl.pallas_call(
        functools.partial(_combine_ln_kernel, tile=tile, alpha=alpha),
        grid_spec=grid_spec,
        out_shape=jax.ShapeDtypeStruct((r, d), F32),
        compiler_params=_cp(("arbitrary",), VMEM_LIMIT),
    )(pos1, pos2, y_sorted, h, g, b)


def _route_plan(route, n_experts, tm):
    r = route.shape[0]
    e = jnp.concatenate([route[:, 0], route[:, 1]]).astype(jnp.int32)
    gate = jnp.concatenate([route[:, 2], route[:, 3]])
    onehot = (e[:, None] == jnp.arange(n_experts, dtype=jnp.int32)[None, :]).astype(jnp.int32)
    csum = jnp.cumsum(onehot, axis=0)
    rank = jnp.take_along_axis(csum, e[:, None], axis=1)[:, 0] - 1
    tiles = (csum[-1] + tm - 1) // tm
    tile_end = jnp.cumsum(tiles)
    pos = (tile_end - tiles)[e] * tm + rank
    n_tiles = (TOP_K * r) // tm + n_experts
    tok = jnp.arange(TOP_K * r, dtype=jnp.int32) % r
    src = jnp.zeros((n_tiles * tm,), jnp.int32).at[pos].set(tok)
    gate_sorted = jnp.zeros((n_tiles * tm,), F32).at[pos].set(gate)
    n_used = tile_end[-1]
    tile_expert = jnp.searchsorted(tile_end, jnp.arange(n_tiles, dtype=jnp.int32), side="right").astype(jnp.int32)
    last_expert = tile_expert[jnp.maximum(n_used - 1, 0)]
    tile_expert = jnp.where(jnp.arange(n_tiles) < n_used, tile_expert, last_expert).astype(jnp.int32)
    return src, gate_sorted.reshape(-1, 1), tile_expert, n_used.reshape(1).astype(jnp.int32), pos[:r], pos[r:]


def _pad_cols(w, offset, width):
    return jnp.pad(w, ((0, 0), (offset, width - offset - w.shape[1])))


def kernel(x_prompt, x_sample, state_ret, cache_ckv, cache_kpe, page_table, meta_tokens, ln_g, ln_b, ret_w_in, ret_gn_g, ret_w_o, mla_w_dq, mla_q_norm, mla_w_uq, mla_w_o, kv_w_a, kv_norm, kv_w_b, ffn_w13, ffn_w2, moe_w_r, moe_b_r, moe_w13, moe_w2):
    bsz, seq, d = x_prompt.shape
    dec_b, dec_seq, _ = x_sample.shape
    n_ret, _, ret_h, dk, dv = state_ret.shape
    depth = ln_g.shape[0]
    assert depth == 2 and n_ret == 1 and mla_w_dq.shape[0] == 1, "layer pattern: one retention layer then one MLA layer"
    past_len = page_table.shape[1] * PAGE_SIZE
    lora, mla_h, nope_v = kv_w_b.shape
    rope = kv_w_a.shape[1] - lora
    nope = mla_w_uq.shape[2] // mla_h - rope
    v_dim = nope_v - nope
    n_experts = moe_w_r.shape[2]
    assert nope == LANE // 2 and v_dim == LANE // 2 and rope <= LANE // 4
    alpha = (2 * depth) ** 0.25
    mla_scale = (nope + rope) ** -0.5
    half = rope // 2

    n_p, n_s = bsz * seq, dec_b * dec_seq
    meta_row0 = n_p + n_s
    r_real = meta_row0 + N_META
    r_pad = -(-r_real // ROW_TILE) * ROW_TILE
    assert r_pad - meta_row0 >= LANE

    x0 = jnp.concatenate([x_prompt.reshape(n_p, d), x_sample.reshape(n_s, d), meta_tokens,
                          jnp.zeros((r_pad - r_real, d), F32)], axis=0)
    pos = jnp.concatenate([jnp.tile(N_META + jnp.arange(seq), bsz), jnp.tile(past_len + jnp.arange(dec_seq), dec_b),
                           jnp.arange(N_META), jnp.zeros((r_pad - r_real,), jnp.int32)]).astype(F32)

    inv_r = ROPE_BASE ** (-jnp.arange(dk // 2, dtype=F32) / (dk // 2))
    ang_r = pos[:, None] * inv_r[None, :]
    cos_r, sin_r = jnp.cos(ang_r), jnp.sin(ang_r)
    inv_m = ROPE_BASE ** (-jnp.arange(half, dtype=F32) / half)
    ang_m = pos[:, None] * inv_m[None, :]
    cos_m = jnp.concatenate([jnp.cos(ang_m)] * 2, axis=1)
    sin_m = jnp.concatenate([jnp.sin(ang_m)] * 2, axis=1)
    cos_a, sin_a = _pad_cols(cos_m, 0, LANE), _pad_cols(sin_m, 0, LANE)
    cos_b, sin_b = _pad_cols(cos_m, nope, LANE), _pad_cols(sin_m, nope, LANE)
    q_ctab = (_pad_cols(jnp.ones((r_pad, nope), F32), 0, LANE) + cos_b) * mla_scale
    q_stab = sin_b * mla_scale

    def rot(w):
        return jnp.concatenate([-w[..., half:], w[..., :half]], axis=-1)

    qkvg = _ret_proj(x0, ret_w_in[0], cos_r, sin_r, ret_h, dk)
    gn = ret_gn_g[0].reshape(1, -1)
    mix_p, mix_meta, st_p = _ret_prompt(qkvg, gn, _decay_tables(ret_h, dk, RET_CHUNK, RET_CHUNK),
                                        bsz, seq, meta_row0, ret_h, dk, dv)
    mix_s, st_s = _ret_sample(qkvg, state_ret[0], gn,
                              _decay_tables(ret_h, dk, SAMPLE_BATCH_TILE * dec_seq, dec_seq),
                              n_p, dec_b, dec_seq, ret_h, dk, dv)
    mix = jnp.concatenate([mix_p, mix_s, mix_meta, jnp.zeros((r_pad - r_real, mix_p.shape[1]), BF16)], axis=0)
    row = lambda v: v.reshape(1, -1)
    h1, h1b = _proj_ln(mix, ret_w_o[0], x0, row(ln_g[0, 0]), row(ln_b[0, 0]), alpha)
    h2, h2b = _ffn_ln(h1b, ffn_w13[0].astype(BF16), ffn_w2[0].astype(BF16), h1, row(ln_g[0, 1]), row(ln_b[0, 1]), alpha)

    wc, wpe = kv_w_a[:, :lora], kv_w_a[:, lora:]
    wkv = jnp.concatenate([wc, _pad_cols(wpe, 0, LANE), _pad_cols(rot(wpe), 0, LANE),
                           _pad_cols(wpe, nope, LANE), _pad_cols(rot(wpe), nope, LANE)], axis=1).astype(BF16)
    wuk = jnp.pad(kv_w_b[:, :, :nope], ((0, 0), (0, 0), (0, LANE - nope))).reshape(lora, mla_h * LANE).astype(BF16)
    wuv = kv_w_b[:, :, nope:].reshape(lora, mla_h * v_dim).astype(BF16)
    ckv, kpe, k_heads, v_heads = _kv_proj(h2b, wkv, row(kv_norm), (cos_a, sin_a, cos_b, sin_b), wuk, wuv,
                                          lora, rope, mla_h)

    wuq = mla_w_uq[0].reshape(-1, mla_h, nope + rope)
    wq_n, wq_r = wuq[:, :, :nope], wuq[:, :, nope:]
    pad_h = lambda w, off: jnp.pad(w, ((0, 0), (0, 0), (off, LANE - off - w.shape[2]))).reshape(w.shape[0], mla_h * LANE)
    w_q1 = (pad_h(wq_n, 0) + pad_h(wq_r, nope)).astype(BF16)
    w_q2 = pad_h(rot(wq_r), nope).astype(BF16)
    wdq = mla_w_dq[0].astype(BF16)
    qn_g = row(mla_q_norm[0])
    q_heads = _q_proj(h2b, wdq, qn_g, w_q1, w_q2, q_ctab, q_stab, mla_h)
    attn_p = _attn_prompt(q_heads, k_heads, v_heads, bsz, seq, meta_row0, mla_h)

    wukt = jnp.pad(jnp.transpose(kv_w_b[:, :, :nope], (1, 2, 0)), ((0, 0), (0, LANE - nope), (0, 0))).astype(BF16)
    w_pe = wq_r.reshape(-1, mla_h * rope).astype(BF16)
    w_pe_rot = rot(wq_r).reshape(-1, mla_h * rope).astype(BF16)
    cs = jnp.tile(cos_m[n_p:n_p + n_s], (1, mla_h))
    ss = jnp.tile(sin_m[n_p:n_p + n_s], (1, mla_h))
    qabs, qpe = _q_sample(h2b[n_p:n_p + n_s], wdq, qn_g, pad_h(wq_n, 0).astype(BF16), wukt, w_pe, w_pe_rot, cs, ss,
                          mla_h, lora, mla_scale)
    ckv_s, kpe_s = ckv[n_p:n_p + n_s], kpe[n_p:n_p + n_s]
    new_pad = lambda a: jnp.pad(a.reshape(dec_b, dec_seq, -1), ((0, 0), (0, LANE - dec_seq), (0, 0))).astype(BF16)
    o_lat = _attn_paged(page_table, qabs.reshape(dec_b, dec_seq * mla_h, lora), qpe.reshape(dec_b, dec_seq * mla_h, rope),
                        new_pad(ckv_s), new_pad(kpe_s), cache_ckv, cache_kpe, mla_h, dec_seq)
    wuv_h = jnp.transpose(kv_w_b[:, :, nope:], (1, 0, 2))
    wuv_pad = jnp.where((jnp.arange(mla_h) % 2 == 0)[:, None, None],
                        jnp.pad(wuv_h, ((0, 0), (0, 0), (0, LANE - v_dim))),
                        jnp.pad(wuv_h, ((0, 0), (0, 0), (LANE - v_dim, 0)))).astype(BF16)
    attn_s = _uv_sample(o_lat.reshape(n_s, mla_h * lora), wuv_pad, mla_h, lora)
    attn = jnp.concatenate([attn_p, attn_s, jnp.zeros((r_pad - meta_row0, attn_p.shape[1]), BF16)], axis=0)
    h3, h3b = _proj_ln(attn, mla_w_o[0], h2, row(ln_g[1, 0]), row(ln_b[1, 0]), alpha)

    w_r = _pad_cols(moe_w_r[0], 0, LANE).astype(BF16)
    b_r = _pad_cols(moe_b_r[0].reshape(1, -1), 0, LANE)
    route = _router(h3b, w_r, b_r, n_experts)
    src, gate_sorted, tile_expert, n_used, pos1, pos2 = _route_plan(route, n_experts, MOE_ROW_TILE)
    x_sorted = _gather_rows(src, h3, BF16)
    y_sorted = _moe_ffn(tile_expert, n_used, x_sorted, moe_w13[0].astype(BF16), moe_w2[0].astype(BF16), gate_sorted)
    h4 = _combine_ln(pos1, pos2, y_sorted, h3, row(ln_g[1, 1]), row(ln_b[1, 1]), alpha)

    y_prompt = h4[:n_p].reshape(bsz, seq, d)
    y_sample = h4[n_p:n_p + n_s].reshape(dec_b, dec_seq, d)

    def with_meta(a):
        meta = jnp.broadcast_to(a[meta_row0:r_real][None], (bsz, N_META, a.shape[1]))
        return jnp.concatenate([meta, a[:n_p].reshape(bsz, seq, -1)], axis=1)

    return (y_prompt, y_sample, st_p[None], with_meta(ckv), with_meta(kpe), st_s[None],
            ckv_s.reshape(dec_b, dec_seq, lora), kpe_s.reshape(dec_b, dec_seq, rope))
```

```python
import functools
import math

import jax
import jax.numpy as jnp
from jax import lax
from jax.experimental import pallas as pl
from jax.experimental.pallas import tpu as pltpu

F32 = jnp.float32
BF16 = jnp.bfloat16

N_META = 16
TOP_K = 2
ROPE_BASE = 10000.0
LN_EPS = 1e-5
RMS_EPS = 1e-6
PAGE_SIZE = 128
RET_CHUNK = 128

LANE = 128
ROW_TILE = 512
ATTN_TQ = 512
ATTN_TK = 512
FFN_CHUNK = 256
MOE_ROW_TILE = 512
MOE_FF_CHUNK = 512
GATHER_TILE = 256
GATHER_UNROLL = 8
PAGES_PER_CHUNK = 32
PAGED_KEY_BLOCK = 512
LOG2E = math.log2(math.e)
SAMPLE_BATCH_TILE = 4
VMEM_LIMIT = 56 * 1024 * 1024


def _cp(sem, vmem=None):
    return pltpu.CompilerParams(dimension_semantics=sem, vmem_limit_bytes=vmem)


def _layer_norm(z, g, b):
    mu = jnp.mean(z, axis=-1, keepdims=True)
    zc = z - mu
    var = jnp.mean(zc * zc, axis=-1, keepdims=True)
    return zc * lax.rsqrt(var + LN_EPS) * g + b


def _rms_norm(c, g):
    return c * lax.rsqrt(jnp.mean(c * c, axis=-1, keepdims=True) + RMS_EPS) * g


def _silu(x):
    return x * (1.0 / (1.0 + jnp.exp(-x)))


def _dot(a, b):
    return jnp.dot(a, b, preferred_element_type=F32)


def _dot_nt(a, b):
    return lax.dot_general(a, b, (((1,), (1,)), ((), ())), preferred_element_type=F32)


def _dot_tn(a, b):
    return lax.dot_general(a, b, (((0,), (0,)), ((), ())), preferred_element_type=F32)


def _ret_proj_kernel(x_ref, w_ref, cos_ref, sin_ref, o_ref, wbf_ref, *, n_heads, dk, k_scale):
    j = pl.program_id(0)
    i = pl.program_id(1)

    @pl.when(i == 0)
    def _():
        wbf_ref[...] = w_ref[...].astype(BF16)

    y = _dot(x_ref[...].astype(BF16), wbf_ref[...])
    half = dk // 2

    @pl.when(j < 2)
    def _():
        c = cos_ref[...]
        s = sin_ref[...]
        scale = jnp.where(j == 1, k_scale, 1.0).astype(F32)
        for h in range(n_heads):
            x1 = y[:, h * dk:h * dk + half]
            x2 = y[:, h * dk + half:(h + 1) * dk]
            o_ref[:, h * dk:h * dk + half] = ((x1 * c - x2 * s) * scale).astype(BF16)
            o_ref[:, h * dk + half:(h + 1) * dk] = ((x1 * s + x2 * c) * scale).astype(BF16)

    @pl.when(j >= 2)
    def _():
        o_ref[...] = y.astype(BF16)


def _ret_proj(x, w_in, cos_r, sin_r, n_heads, dk):
    r, d = x.shape
    n = w_in.shape[1]
    qk = n_heads * dk
    assert n % qk == 0 and dk // 2 == LANE
    kern = functools.partial(_ret_proj_kernel, n_heads=n_heads, dk=dk, k_scale=dk ** -0.5)
    return pl.pallas_call(
        kern,
        grid=(n // qk, r // ROW_TILE),
        in_specs=[
            pl.BlockSpec((ROW_TILE, d), lambda j, i: (i, 0)),
            pl.BlockSpec((d, qk), lambda j, i: (0, j)),
            pl.BlockSpec((ROW_TILE, LANE), lambda j, i: (i, 0)),
            pl.BlockSpec((ROW_TILE, LANE), lambda j, i: (i, 0)),
        ],
        out_specs=pl.BlockSpec((ROW_TILE, qk), lambda j, i: (i, j)),
        out_shape=jax.ShapeDtypeStruct((r, n), BF16),
        scratch_shapes=[pltpu.VMEM((d, qk), BF16)],
        compiler_params=_cp(("arbitrary", "arbitrary"), VMEM_LIMIT),
        name="ret_proj",
    )(x, w_in, cos_r, sin_r)


def _group_norm_gate(o, g, gn):
    mu = jnp.mean(o, axis=-1, keepdims=True)
    oc = o - mu
    var = jnp.mean(oc * oc, axis=-1, keepdims=True)
    return (_silu(g.astype(F32)) * (oc * lax.rsqrt(var + LN_EPS) * gn)).astype(BF16)


def _ret_prompt_kernel(q_ref, k_ref, v_ref, g_ref, qm_ref, km_ref, vm_ref, gm_ref, gn_ref,
                       dec_ref, rdec_ref, cdec_ref, cpow_ref,
                       o_ref, om_ref, st_ref, s_ref, *, n_heads, dk, dv, chunk, n_meta, tile):
    t = pl.program_id(1)
    nt = pl.num_programs(1)

    @pl.when(t == 0)
    def _():
        for h in range(n_heads):
            q = qm_ref[:, h * dk:(h + 1) * dk]
            k = km_ref[:, h * dk:(h + 1) * dk]
            v = vm_ref[:, h * dv:(h + 1) * dv]
            sc = _dot_nt(q, k) * dec_ref[h, :n_meta, :n_meta]
            o = _dot(sc.astype(BF16), v)
            om_ref[:, h * dv:(h + 1) * dv] = _group_norm_gate(
                o, gm_ref[:, h * dv:(h + 1) * dv], gn_ref[:, h * dv:(h + 1) * dv])
            k_out = (k.astype(F32) * cdec_ref[h, chunk - n_meta:, :]).astype(BF16)
            s_ref[h] = _dot_tn(k_out, v)

    for h in range(n_heads):
        for c in range(tile // chunk):
            rows = slice(c * chunk, (c + 1) * chunk)
            q = q_ref[rows, h * dk:(h + 1) * dk]
            k = k_ref[rows, h * dk:(h + 1) * dk]
            v = v_ref[rows, h * dv:(h + 1) * dv]
            s_old = s_ref[h]
            sc = _dot_nt(q, k) * dec_ref[h]
            q_in = (q.astype(F32) * rdec_ref[h]).astype(BF16)
            o = _dot(sc.astype(BF16), v) + _dot(q_in, s_old.astype(BF16))
            k_out = (k.astype(F32) * cdec_ref[h]).astype(BF16)
            s_ref[h] = cpow_ref[h] * s_old + _dot_tn(k_out, v)
            o_ref[rows, h * dv:(h + 1) * dv] = _group_norm_gate(
                o, g_ref[rows, h * dv:(h + 1) * dv], gn_ref[:, h * dv:(h + 1) * dv])

    @pl.when(t == nt - 1)
    def _():
        st_ref[0] = s_ref[...]


def _ret_prompt(qkvg, gn_g, tabs, bsz, seq, meta_row0, n_heads, dk, dv):
    qk, vw = n_heads * dk, n_heads * dv
    tile = ROW_TILE
    assert seq % tile == 0 and tile % RET_CHUNK == 0 and meta_row0 % N_META == 0 and vw == 2 * qk
    nt = seq // tile
    mb = meta_row0 // N_META
    dec, rdec, cdec, cpow = tabs
    kern = functools.partial(_ret_prompt_kernel, n_heads=n_heads, dk=dk, dv=dv, chunk=RET_CHUNK,
                             n_meta=N_META, tile=tile)
    full = lambda *shape: pl.BlockSpec(shape, lambda b, t: (0,) * len(shape))
    return pl.pallas_call(
        kern,
        grid=(bsz, nt),
        in_specs=[
            pl.BlockSpec((tile, qk), lambda b, t: (b * nt + t, 0)),
            pl.BlockSpec((tile, qk), lambda b, t: (b * nt + t, 1)),
            pl.BlockSpec((tile, vw), lambda b, t: (b * nt + t, 1)),
            pl.BlockSpec((tile, vw), lambda b, t: (b * nt + t, 2)),
            pl.BlockSpec((N_META, qk), lambda b, t: (mb, 0)),
            pl.BlockSpec((N_META, qk), lambda b, t: (mb, 1)),
            pl.BlockSpec((N_META, vw), lambda b, t: (mb, 1)),
            pl.BlockSpec((N_META, vw), lambda b, t: (mb, 2)),
            full(1, vw),
            full(n_heads, RET_CHUNK, RET_CHUNK),
            full(n_heads, RET_CHUNK, dk),
            full(n_heads, RET_CHUNK, dk),
            full(n_heads, 1, 1),
        ],
        out_specs=[
            pl.BlockSpec((tile, vw), lambda b, t: (b * nt + t, 0)),
            pl.BlockSpec((N_META, vw), lambda b, t: (0, 0)),
            pl.BlockSpec((1, n_heads, dk, dv), lambda b, t: (b, 0, 0, 0)),
        ],
        out_shape=[
            jax.ShapeDtypeStruct((bsz * seq, vw), BF16),
            jax.ShapeDtypeStruct((N_META, vw), BF16),
            jax.ShapeDtypeStruct((bsz, n_heads, dk, dv), F32),
        ],
        scratch_shapes=[pltpu.VMEM((n_heads, dk, dv), F32)],
        compiler_params=_cp(("arbitrary", "arbitrary"), VMEM_LIMIT),
        name="ret_prompt",
    )(qkvg, qkvg, qkvg, qkvg, qkvg, qkvg, qkvg, qkvg, gn_g, dec, rdec, cdec, cpow)


def _ret_sample_kernel(q_ref, k_ref, v_ref, g_ref, s0_ref, gn_ref, dec_ref, rdec_ref, cdec_ref, cpow_ref,
                       o_ref, st_ref, *, n_heads, dk, dv, nb, dec_seq):
    rows = nb * dec_seq
    row_b = lax.broadcasted_iota(jnp.int32, (rows, 1), 0) // dec_seq
    for h in range(n_heads):
        q = q_ref[:, h * dk:(h + 1) * dk]
        k = k_ref[:, h * dk:(h + 1) * dk]
        v = v_ref[:, h * dv:(h + 1) * dv]
        sc = _dot_nt(q, k) * dec_ref[h]
        o = _dot(sc.astype(BF16), v)
        q_in = q.astype(F32) * rdec_ref[h]
        k_out = k.astype(F32) * cdec_ref[h]
        for b in range(nb):
            s_old = s0_ref[0, b, h]
            sel = row_b == b
            o = o + _dot(jnp.where(sel, q_in, 0.0).astype(BF16), s_old.astype(BF16))
            st_ref[0, b, h] = cpow_ref[h] * s_old + _dot_tn(jnp.where(sel, k_out, 0.0).astype(BF16), v)
        o_ref[:, h * dv:(h + 1) * dv] = _group_norm_gate(
            o, g_ref[:, h * dv:(h + 1) * dv], gn_ref[:, h * dv:(h + 1) * dv])


def _ret_sample(qkvg, state, gn_g, tabs, row0, dec_b, dec_seq, n_heads, dk, dv):
    qk, vw = n_heads * dk, n_heads * dv
    nb = SAMPLE_BATCH_TILE
    rows = nb * dec_seq
    assert dec_b % nb == 0 and row0 % rows == 0 and rows % 16 == 0
    rb = row0 // rows
    dec, rdec, cdec, cpow = tabs
    state5 = state.reshape(dec_b // nb, nb, n_heads, dk, dv)
    kern = functools.partial(_ret_sample_kernel, n_heads=n_heads, dk=dk, dv=dv, nb=nb, dec_seq=dec_seq)
    full = lambda *shape: pl.BlockSpec(shape, lambda i: (0,) * len(shape))
    out, st = pl.pallas_call(
        kern,
        grid=(dec_b // nb,),
        in_specs=[
            pl.BlockSpec((rows, qk), lambda i: (rb + i, 0)),
            pl.BlockSpec((rows, qk), lambda i: (rb + i, 1)),
            pl.BlockSpec((rows, vw), lambda i: (rb + i, 1)),
            pl.BlockSpec((rows, vw), lambda i: (rb + i, 2)),
            pl.BlockSpec((1, nb, n_heads, dk, dv), lambda i: (i, 0, 0, 0, 0)),
            full(1, vw),
            full(n_heads, rows, rows),
            full(n_heads, rows, dk),
            full(n_heads, rows, dk),
            full(n_heads, 1, 1),
        ],
        out_specs=[
            pl.BlockSpec((rows, vw), lambda i: (i, 0)),
            pl.BlockSpec((1, nb, n_heads, dk, dv), lambda i: (i, 0, 0, 0, 0)),
        ],
        out_shape=[
            jax.ShapeDtypeStruct((dec_b * dec_seq, vw), BF16),
            jax.ShapeDtypeStruct(state5.shape, F32),
        ],
        compiler_params=_cp(("arbitrary",), VMEM_LIMIT),
        name="ret_sample",
    )(qkvg, qkvg, qkvg, qkvg, state5, gn_g, dec, rdec, cdec, cpow)
    return out, st.reshape(dec_b, n_heads, dk, dv)


def _decay_tables(n_heads, dk, chunk, group):
    lg = jnp.log1p(-jnp.exp2(-5.0 - jnp.arange(n_heads, dtype=F32)))
    r = jnp.arange(chunk)
    t = (r % group).astype(F32)
    diff = t[:, None] - t[None, :]
    same = (r[:, None] // group) == (r[None, :] // group)
    dec = jnp.where(same & (diff >= 0), jnp.exp(jnp.maximum(diff, 0.0)[None] * lg[:, None, None]), 0.0)
    rdec = jnp.exp((t[None, :] + 1.0) * lg[:, None])
    cdec = jnp.exp((group - 1.0 - t)[None, :] * lg[:, None])
    cpow = jnp.exp(group * lg)
    bc = lambda a: jnp.broadcast_to(a[:, :, None], (n_heads, chunk, dk))
    return dec, bc(rdec), bc(cdec), cpow.reshape(n_heads, 1, 1)


def _proj_ln_kernel(a_ref, w_ref, h_ref, g_ref, b_ref, o_ref, obf_ref, wbf_ref, *, alpha):
    @pl.when(pl.program_id(0) == 0)
    def _():
        wbf_ref[...] = w_ref[...].astype(BF16)

    z = alpha * h_ref[...] + _dot(a_ref[...], wbf_ref[...])
    y = _layer_norm(z, g_ref[...], b_ref[...])
    o_ref[...] = y
    obf_ref[...] = y.astype(BF16)


def _proj_ln(a, w, h, g, b, alpha):
    r, k = a.shape
    d = w.shape[1]
    row = lambda n: pl.BlockSpec((ROW_TILE, n), lambda i: (i, 0))
    full = lambda *shape: pl.BlockSpec(shape, lambda i: (0,) * len(shape))
    return pl.pallas_call(
        functools.partial(_proj_ln_kernel, alpha=alpha),
        grid=(r // ROW_TILE,),
        in_specs=[row(k), full(k, d), row(d), full(1, d), full(1, d)],
        out_specs=[row(d), row(d)],
        out_shape=[jax.ShapeDtypeStruct((r, d), F32), jax.ShapeDtypeStruct((r, d), BF16)],
        scratch_shapes=[pltpu.VMEM((k, d), BF16)],
        compiler_params=_cp(("arbitrary",), VMEM_LIMIT),
        name="proj_ln",
    )(a, w, h, g, b)


def _ffn_ln_kernel(x_ref, w13_ref, w2_ref, h_ref, g_ref, b_ref, o_ref, obf_ref, *, alpha, d_ff, chunk):
    x = x_ref[...]
    acc = alpha * h_ref[...]
    for c in range(d_ff // chunk):
        gate = _dot(x, w13_ref[:, c * chunk:(c + 1) * chunk])
        up = _dot(x, w13_ref[:, d_ff + c * chunk:d_ff + (c + 1) * chunk])
        acc = acc + _dot((_silu(gate) * up).astype(BF16), w2_ref[c * chunk:(c + 1) * chunk, :])
    y = _layer_norm(acc, g_ref[...], b_ref[...])
    o_ref[...] = y
    obf_ref[...] = y.astype(BF16)


def _ffn_ln(x, w13, w2, h, g, b, alpha):
    r, d = x.shape
    d_ff = w2.shape[0]
    assert d_ff % FFN_CHUNK == 0
    row = lambda n: pl.BlockSpec((ROW_TILE, n), lambda i: (i, 0))
    full = lambda *shape: pl.BlockSpec(shape, lambda i: (0,) * len(shape))
    return pl.pallas_call(
        functools.partial(_ffn_ln_kernel, alpha=alpha, d_ff=d_ff, chunk=FFN_CHUNK),
        grid=(r // ROW_TILE,),
        in_specs=[row(d), full(d, 2 * d_ff), full(d_ff, d), row(d), full(1, d), full(1, d)],
        out_specs=[row(d), row(d)],
        out_shape=[jax.ShapeDtypeStruct((r, d), F32), jax.ShapeDtypeStruct((r, d), BF16)],
        compiler_params=_cp(("arbitrary",), VMEM_LIMIT),
        name="ffn_ln",
    )(x, w13, w2, h, g, b)


def _kv_kernel(x_ref, wkv_ref, g_ref, ca_ref, sa_ref, cb_ref, sb_ref, wuk_ref, wuv_ref, ones_ref,
               ckv_ref, kpe_ref, k_ref, v_ref, *, lora, rope, n_heads):
    y = _dot(x_ref[...], wkv_ref[...])
    cn = _rms_norm(y[:, :lora], g_ref[...])
    ckv_ref[...] = cn
    kpe_a = y[:, lora:lora + LANE] * ca_ref[...] + y[:, lora + LANE:lora + 2 * LANE] * sa_ref[...]
    kpe_ref[...] = kpe_a[:, :rope]
    kpe_b = y[:, lora + 2 * LANE:lora + 3 * LANE] * cb_ref[...] + y[:, lora + 3 * LANE:lora + 4 * LANE] * sb_ref[...]
    cb = cn.astype(BF16)
    kn = _dot(cb, wuk_ref[...])
    for h in range(n_heads):
        k_ref[:, h * LANE:(h + 1) * LANE] = (kn[:, h * LANE:(h + 1) * LANE] + kpe_b).astype(BF16)
    v_ref[...] = (_dot(cb, wuv_ref[...]) + ones_ref[...]).astype(BF16)


def _kv_proj(x, wkv, g, tabs, wuk, wuv, ones_row, lora, rope, n_heads):
    r, d = x.shape
    nv = wuv.shape[1]
    row = lambda n: pl.BlockSpec((ROW_TILE, n), lambda i: (i, 0))
    full = lambda *shape: pl.BlockSpec(shape, lambda i: (0,) * len(shape))
    return pl.pallas_call(
        functools.partial(_kv_kernel, lora=lora, rope=rope, n_heads=n_heads),
        grid=(r // ROW_TILE,),
        in_specs=[row(d), full(*wkv.shape), full(1, lora), row(LANE), row(LANE), row(LANE), row(LANE),
                  full(*wuk.shape), full(*wuv.shape), full(1, nv)],
        out_specs=[row(lora), row(rope), row(n_heads * LANE), row(nv)],
        out_shape=[jax.ShapeDtypeStruct((r, lora), F32), jax.ShapeDtypeStruct((r, rope), F32),
                   jax.ShapeDtypeStruct((r, n_heads * LANE), BF16), jax.ShapeDtypeStruct((r, nv), BF16)],
        compiler_params=_cp(("arbitrary",), VMEM_LIMIT),
        name="kv_proj",
    )(x, wkv, g, *tabs, wuk, wuv, ones_row)


def _q_kernel(x_ref, wdq_ref, g_ref, w1_ref, w2_ref, c_ref, s_ref, q_ref, *, n_heads):
    qc = _rms_norm(_dot(x_ref[...], wdq_ref[...]), g_ref[...]).astype(BF16)
    a = _dot(qc, w1_ref[...])
    b = _dot(qc, w2_ref[...])
    c = c_ref[...]
    s = s_ref[...]
    for h in range(n_heads):
        cols = slice(h * LANE, (h + 1) * LANE)
        q_ref[:, cols] = (a[:, cols] * c + b[:, cols] * s).astype(BF16)


def _q_proj(x, wdq, g, w1, w2, c_tab, s_tab, n_heads):
    r, d = x.shape
    ql = wdq.shape[1]
    row = lambda n: pl.BlockSpec((ROW_TILE, n), lambda i: (i, 0))
    full = lambda *shape: pl.BlockSpec(shape, lambda i: (0,) * len(shape))
    return pl.pallas_call(
        functools.partial(_q_kernel, n_heads=n_heads),
        grid=(r // ROW_TILE,),
        in_specs=[row(d), full(d, ql), full(1, ql), full(*w1.shape), full(*w2.shape), row(LANE), row(LANE)],
        out_specs=row(n_heads * LANE),
        out_shape=jax.ShapeDtypeStruct((r, n_heads * LANE), BF16),
        compiler_params=_cp(("arbitrary",), VMEM_LIMIT),
        name="q_proj",
    )(x, wdq, g, w1, w2, c_tab, s_tab)


def _q_sample_kernel(x_ref, wdq_ref, g_ref, w1_ref, wukt_ref, wpe_ref, wrot_ref, c_ref, s_ref,
                     qabs_ref, qpe_ref, *, n_heads, lora, scale):
    qc = _rms_norm(_dot(x_ref[...], wdq_ref[...]), g_ref[...]).astype(BF16)
    qn = (_dot(qc, w1_ref[...]) * scale).astype(BF16)
    for h in range(n_heads):
        qabs_ref[:, h * lora:(h + 1) * lora] = _dot(qn[:, h * LANE:(h + 1) * LANE], wukt_ref[h]).astype(BF16)
    qpe = _dot(qc, wpe_ref[...]) * c_ref[...] + _dot(qc, wrot_ref[...]) * s_ref[...]
    qpe_ref[...] = (qpe * scale).astype(BF16)


def _q_sample(x, wdq, g, w1, wukt, wpe, wrot, c_tab, s_tab, n_heads, lora, scale):
    ns = x.shape[0]
    args = (x, wdq, g, w1, wukt, wpe, wrot, c_tab, s_tab)
    full = lambda a: pl.BlockSpec(a.shape, lambda i, nd=a.ndim: (0,) * nd)
    return pl.pallas_call(
        functools.partial(_q_sample_kernel, n_heads=n_heads, lora=lora, scale=scale),
        grid=(1,),
        in_specs=[full(a) for a in args],
        out_specs=[pl.BlockSpec((ns, n_heads * lora), lambda i: (0, 0)),
                   pl.BlockSpec((ns, wpe.shape[1]), lambda i: (0, 0))],
        out_shape=[jax.ShapeDtypeStruct((ns, n_heads * lora), BF16),
                   jax.ShapeDtypeStruct((ns, wpe.shape[1]), BF16)],
        compiler_params=_cp(("arbitrary",), VMEM_LIMIT),
        name="q_sample",
    )(*args)


def _attn_step(m, acc, s, v):
    m_new = jnp.maximum(m, jnp.max(s, axis=-1, keepdims=True))
    p = jnp.exp2((s - m_new).astype(BF16))
    return m_new, jnp.exp2(m - m_new) * acc + _dot(p, v)


def _attn_prompt_kernel(q_ref, k_ref, v_ref, km_ref, vm_ref, o_ref, *, tq, tk, n_meta):
    i = pl.program_id(2)
    lane = lax.broadcasted_iota(jnp.int32, (tq, LANE), 1)
    heads = (slice(0, LANE), slice(LANE, 2 * LANE))
    qs = [q_ref[:, h] for h in heads]

    carry = []
    for q, h in zip(qs, heads):
        s = jnp.where(lane < n_meta, _dot_nt(q, km_ref[:, h]), -jnp.inf)
        m = jnp.max(s, axis=-1, keepdims=True)
        carry += [m, _dot(jnp.exp2((s - m).astype(BF16)), vm_ref[:, h])]

    def tile(j, carry, masked):
        off = pl.multiple_of(j * tk, tk)
        out = []
        for e, (q, h) in enumerate(zip(qs, heads)):
            s = _dot_nt(q, k_ref[pl.ds(off, tk), h])
            if masked:
                row = lax.broadcasted_iota(jnp.int32, (tq, tk), 0)
                col = lax.broadcasted_iota(jnp.int32, (tq, tk), 1)
                s = jnp.where(col <= row, s, -jnp.inf)
            out += _attn_step(carry[2 * e], carry[2 * e + 1], s, v_ref[pl.ds(off, tk), h])
        return out

    carry = lax.fori_loop(0, i, lambda j, c: tuple(tile(j, c, False)), tuple(carry))
    _, acc0, _, acc1 = tile(i, carry, True)
    l0 = jnp.sum(jnp.where(lane == LANE // 2, acc0, 0.0), axis=-1, keepdims=True)
    l1 = jnp.sum(jnp.where(lane == 0, acc1, 0.0), axis=-1, keepdims=True)
    o_ref[...] = jnp.where(lane < LANE // 2, acc0 * (1.0 / l0), acc1 * (1.0 / l1)).astype(BF16)


def _attn_prompt(q, k, v, bsz, seq, meta_row0, n_heads):
    tq, tk = ATTN_TQ, ATTN_TK
    assert tq == tk and seq % tq == 0 and meta_row0 % LANE == 0 and n_heads % 2 == 0
    assert v.shape[1] == n_heads * LANE
    nq = seq // tq
    mb = meta_row0 // LANE
    pair = 2 * LANE
    return pl.pallas_call(
        functools.partial(_attn_prompt_kernel, tq=tq, tk=tk, n_meta=N_META),
        grid=(bsz, n_heads // 2, nq),
        in_specs=[
            pl.BlockSpec((tq, pair), lambda b, hp, i: (b * nq + i, hp)),
            pl.BlockSpec((seq, pair), lambda b, hp, i: (b, hp)),
            pl.BlockSpec((seq, pair), lambda b, hp, i: (b, hp)),
            pl.BlockSpec((LANE, pair), lambda b, hp, i: (mb, hp)),
            pl.BlockSpec((LANE, pair), lambda b, hp, i: (mb, hp)),
        ],
        out_specs=pl.BlockSpec((tq, LANE), lambda b, hp, i: (b * nq + i, hp)),
        out_shape=jax.ShapeDtypeStruct((bsz * seq, n_heads * LANE // 2), BF16),
        compiler_params=_cp(("arbitrary", "arbitrary", "arbitrary"), VMEM_LIMIT),
        name="attn_prompt",
    )(q, k, v, k, v)


def _softmax_part(s, v):
    m = jnp.max(s, axis=-1, keepdims=True)
    p = jnp.exp2(s - m)
    return m, jnp.sum(p, axis=-1, keepdims=True), _dot(p.astype(BF16), v)


def _softmax_merge(parts):
    m = functools.reduce(jnp.maximum, [p[0] for p in parts])
    w = [jnp.exp2(p[0] - m) for p in parts]
    l = functools.reduce(jnp.add, [wi * p[1] for wi, p in zip(w, parts)])
    acc = functools.reduce(jnp.add, [wi * p[2] for wi, p in zip(w, parts)])
    return m, l, acc


def _attn_paged_kernel(pt_ref, qabs_ref, qpe_ref, cnew_ref, pnew_ref, ckv_hbm, kpe_hbm, o_ref,
                       cbuf, pbuf, sem, m_ref, l_ref, acc_ref, *, ppc, n_pairs, n_heads, dec_seq, sub):
    c = pl.program_id(1)
    pair = pl.program_id(0) * n_pairs + c
    n_total = pl.num_programs(0) * n_pairs

    def copies(chunk, sl):
        base = chunk * ppc
        out = []
        for p in range(ppc):
            page = pt_ref[base + p]
            keys = pl.ds(p * PAGE_SIZE, PAGE_SIZE)
            out.append(pltpu.make_async_copy(ckv_hbm.at[page], cbuf.at[sl, keys, :], sem.at[0, sl]))
            out.append(pltpu.make_async_copy(kpe_hbm.at[page], pbuf.at[sl, :, keys], sem.at[1, sl]))
        return out

    def wait_slot(sl):
        pltpu.make_async_copy(cbuf.at[sl], cbuf.at[sl], sem.at[0, sl]).wait()
        pltpu.make_async_copy(pbuf.at[sl], pbuf.at[sl], sem.at[1, sl]).wait()

    @pl.when(pair == 0)
    def _():
        for cp in copies(0, 0):
            cp.start()

    @pl.when(c == 0)
    def _():
        m_ref[...] = jnp.full(m_ref.shape, -jnp.inf, F32)
        l_ref[...] = jnp.zeros(l_ref.shape, F32)
        acc_ref[...] = jnp.zeros(acc_ref.shape, F32)

    qa = qabs_ref[0]
    qp = qpe_ref[0]
    n_sub = ppc * PAGE_SIZE // sub

    def slot_parts(sl):
        def scores(i):
            keys = slice(i * sub, (i + 1) * sub)
            kc = cbuf[sl, keys, :].astype(BF16)
            return _dot_nt(qa, kc) + _dot(qp, pbuf[sl, :, keys].astype(BF16)), kc

        out = []
        ahead = scores(0)
        for i in range(n_sub):
            s, kc = ahead
            if i + 1 < n_sub:
                ahead = scores(i + 1)
            out.append(_softmax_part(s, kc))
        return out

    wait_slot(0)
    for cp in copies(2 * pair + 1, 1):
        cp.start()
    parts = [(m_ref[...], l_ref[...], acc_ref[...])] + slot_parts(0)
    wait_slot(1)
    for cp in copies(jnp.where(pair + 1 < n_total, 2 * pair + 2, 0), 0):
        cp.start()
    m, l, acc = _softmax_merge(parts + slot_parts(1))
    m_ref[...] = m
    l_ref[...] = l
    acc_ref[...] = acc

    @pl.when(c == n_pairs - 1)
    def _():
        kn = cnew_ref[0]
        s = _dot_nt(qa, kn) + _dot_nt(qp, pnew_ref[0])
        rows = s.shape[0]
        tok = lax.broadcasted_iota(jnp.int32, (rows, LANE), 0) // n_heads
        col = lax.broadcasted_iota(jnp.int32, (rows, LANE), 1)
        s = jnp.where((col <= tok) & (col < dec_seq), s, -jnp.inf)
        _, l2, acc2 = _softmax_merge([(m, l, acc), _softmax_part(s, kn)])
        o_ref[0] = acc2 * (1.0 / l2)

    @pl.when(pair == n_total - 1)
    def _():
        wait_slot(0)


def _attn_paged(page_table, qabs, qpe, cnew, pnew, cache_ckv, cache_kpe_t, n_heads, dec_seq):
    dec_b, n_pages = page_table.shape
    assert n_pages % 2 == 0
    ppc = min(PAGES_PER_CHUNK, n_pages // 2)
    sub = min(PAGED_KEY_BLOCK, ppc * PAGE_SIZE)
    assert n_pages % (2 * ppc) == 0 and cache_ckv.shape[1] == PAGE_SIZE and (ppc * PAGE_SIZE) % sub == 0
    n_pairs = n_pages // (2 * ppc)
    rows, lora = qabs.shape[1], qabs.shape[2]
    rope = qpe.shape[2]
    kern = functools.partial(_attn_paged_kernel, ppc=ppc, n_pairs=n_pairs, n_heads=n_heads, dec_seq=dec_seq,
                             sub=sub)
    grid_spec = pltpu.PrefetchScalarGridSpec(
        num_scalar_prefetch=1,
        grid=(dec_b, n_pairs),
        in_specs=[
            pl.BlockSpec((1, rows, lora), lambda b, c, pt: (b, 0, 0)),
            pl.BlockSpec((1, rows, rope), lambda b, c, pt: (b, 0, 0)),
            pl.BlockSpec((1, LANE, lora), lambda b, c, pt: (b, 0, 0)),
            pl.BlockSpec((1, LANE, rope), lambda b, c, pt: (b, 0, 0)),
            pl.BlockSpec(memory_space=pl.ANY),
            pl.BlockSpec(memory_space=pl.ANY),
        ],
        out_specs=pl.BlockSpec((1, rows, lora), lambda b, c, pt: (b, 0, 0)),
        scratch_shapes=[
            pltpu.VMEM((2, ppc * PAGE_SIZE, lora), F32),
            pltpu.VMEM((2, rope, ppc * PAGE_SIZE), F32),
            pltpu.SemaphoreType.DMA((2, 2)),
            pltpu.VMEM((rows, 1), F32),
            pltpu.VMEM((rows, 1), F32),
            pltpu.VMEM((rows, lora), F32),
        ],
    )
    return pl.pallas_call(
        kern,
        grid_spec=grid_spec,
        out_shape=jax.ShapeDtypeStruct((dec_b, rows, lora), F32),
        compiler_params=_cp(("arbitrary", "arbitrary"), VMEM_LIMIT),
        name="attn_paged",
    )(page_table.reshape(-1), qabs, qpe, cnew, pnew, cache_ckv, cache_kpe_t)


def _uv_sample_kernel(o_ref, w_ref, out_ref, *, n_heads, lora):
    for p in range(n_heads // 2):
        acc = _dot(o_ref[:, (2 * p) * lora:(2 * p + 1) * lora].astype(BF16), w_ref[2 * p])
        acc = acc + _dot(o_ref[:, (2 * p + 1) * lora:(2 * p + 2) * lora].astype(BF16), w_ref[2 * p + 1])
        out_ref[:, p * LANE:(p + 1) * LANE] = acc.astype(BF16)


def _uv_sample(o_lat, wuv_pad, n_heads, lora):
    ns = o_lat.shape[0]
    nv = n_heads * LANE // 2
    return pl.pallas_call(
        functools.partial(_uv_sample_kernel, n_heads=n_heads, lora=lora),
        grid=(1,),
        in_specs=[pl.BlockSpec(o_lat.shape, lambda i: (0, 0)), pl.BlockSpec(wuv_pad.shape, lambda i: (0, 0, 0))],
        out_specs=pl.BlockSpec((ns, nv), lambda i: (0, 0)),
        out_shape=jax.ShapeDtypeStruct((ns, nv), BF16),
        compiler_params=_cp(("arbitrary",), VMEM_LIMIT),
        name="uv_sample",
    )(o_lat, wuv_pad)


def _router_kernel(x_ref, w_ref, b_ref, o_ref, *, n_experts):
    logits = _dot(x_ref[...], w_ref[...]) + b_ref[...]
    lane = lax.broadcasted_iota(jnp.int32, logits.shape, 1).astype(F32)
    logits = jnp.where(lane < n_experts, logits, -jnp.inf)
    m1 = jnp.max(logits, axis=-1, keepdims=True)
    i1 = jnp.min(jnp.where(logits == m1, lane, float(LANE)), axis=-1, keepdims=True)
    rest = jnp.where(lane == i1, -jnp.inf, logits)
    m2 = jnp.max(rest, axis=-1, keepdims=True)
    i2 = jnp.min(jnp.where(rest == m2, lane, float(LANE)), axis=-1, keepdims=True)
    e2 = jnp.exp(m2 - m1)
    den = 1.0 + e2
    g1 = 1.0 / den
    g2 = e2 / den
    o_ref[...] = jnp.where(lane == 0, i1, jnp.where(lane == 1, i2, jnp.where(lane == 2, g1, jnp.where(lane == 3, g2, 0.0))))


def _router(x, w_pad, b_pad, n_experts):
    r, d = x.shape
    return pl.pallas_call(
        functools.partial(_router_kernel, n_experts=n_experts),
        grid=(r // ROW_TILE,),
        in_specs=[pl.BlockSpec((ROW_TILE, d), lambda i: (i, 0)), pl.BlockSpec((d, LANE), lambda i: (0, 0)),
                  pl.BlockSpec((1, LANE), lambda i: (0, 0))],
        out_specs=pl.BlockSpec((ROW_TILE, LANE), lambda i: (i, 0)),
        out_shape=jax.ShapeDtypeStruct((r, LANE), F32),
        compiler_params=_cp(("arbitrary",), VMEM_LIMIT),
        name="router",
    )(x, w_pad, b_pad)


def _gather_rows_kernel(idx_ref, src_hbm, o_ref, buf, sem, *, tile):
    i = pl.program_id(0)
    n = pl.num_programs(0)
    slot = i % 2

    def issue(t, sl):
        def body(g, carry):
            for u in range(GATHER_UNROLL):
                r = g * GATHER_UNROLL + u
                row = idx_ref[t * tile + r]
                pltpu.make_async_copy(src_hbm.at[pl.ds(row, 1), :], buf.at[sl, pl.ds(r, 1), :], sem.at[sl]).start()
            return carry
        lax.fori_loop(0, tile // GATHER_UNROLL, body, 0)

    @pl.when(i == 0)
    def _():
        issue(i, slot)

    @pl.when(i + 1 < n)
    def _():
        issue(i + 1, 1 - slot)

    pltpu.make_async_copy(buf.at[slot], buf.at[slot], sem.at[slot]).wait()
    o_ref[...] = buf[slot].astype(o_ref.dtype)


def _gather_rows(idx, src, out_dtype):
    n = idx.shape[0]
    d = src.shape[1]
    tile = GATHER_TILE
    assert n % tile == 0 and tile % GATHER_UNROLL == 0
    grid_spec = pltpu.PrefetchScalarGridSpec(
        num_scalar_prefetch=1,
        grid=(n // tile,),
        in_specs=[pl.BlockSpec(memory_space=pl.ANY)],
        out_specs=pl.BlockSpec((tile, d), lambda i, idx: (i, 0)),
        scratch_shapes=[pltpu.VMEM((2, tile, d), src.dtype), pltpu.SemaphoreType.DMA((2,))],
    )
    return pl.pallas_call(
        functools.partial(_gather_rows_kernel, tile=tile),
        grid_spec=grid_spec,
        out_shape=jax.ShapeDtypeStruct((n, d), out_dtype),
        compiler_params=_cp(("arbitrary",), VMEM_LIMIT),
        name="gather_rows",
    )(idx, src)


def _moe_ffn_kernel(te_ref, nu_ref, x_ref, wg_ref, wu_ref, w2_ref, o_ref, acc_ref):
    t = pl.program_id(0)
    f = pl.program_id(1)
    nf = pl.num_programs(1)
    used = t < nu_ref[0]

    @pl.when(used)
    def _():
        x = x_ref[...]
        hcol = (_silu(_dot(x, wg_ref[0])) * _dot(x, wu_ref[0])).astype(BF16)
        y = _dot(hcol, w2_ref[0])

        @pl.when(f == 0)
        def _():
            acc_ref[...] = y

        @pl.when(f > 0)
        def _():
            acc_ref[...] += y

    @pl.when(f == nf - 1)
    def _():
        o_ref[...] = jnp.where(used, acc_ref[...], 0.0)


def _moe_ffn(tile_expert, n_used, x_sorted, w13, w2):
    n, d = x_sorted.shape
    n_exp, eff = w2.shape[0], w2.shape[1]
    tm, tf = MOE_ROW_TILE, min(MOE_FF_CHUNK, eff)
    assert n % tm == 0 and eff % tf == 0
    nf = eff // tf

    def widx(base):
        def index_map(t, f, te, nu):
            live = t < nu[0]
            return (te[t], 0, jnp.where(live, f, nf - 1) + base)
        return index_map

    def w2idx(t, f, te, nu):
        return (te[t], jnp.where(t < nu[0], f, nf - 1), 0)

    grid_spec = pltpu.PrefetchScalarGridSpec(
        num_scalar_prefetch=2,
        grid=(n // tm, nf),
        in_specs=[
            pl.BlockSpec((tm, d), lambda t, f, te, nu: (t, 0)),
            pl.BlockSpec((1, d, tf), widx(0)),
            pl.BlockSpec((1, d, tf), widx(nf)),
            pl.BlockSpec((1, tf, d), w2idx),
        ],
        out_specs=pl.BlockSpec((tm, d), lambda t, f, te, nu: (t, 0)),
        scratch_shapes=[pltpu.VMEM((tm, d), F32)],
    )
    return pl.pallas_call(
        _moe_ffn_kernel,
        grid_spec=grid_spec,
        out_shape=jax.ShapeDtypeStruct((n, d), F32),
        compiler_params=_cp(("arbitrary", "arbitrary"), VMEM_LIMIT),
        name="moe_ffn",
    )(tile_expert, n_used, x_sorted, w13, w13, w2)


def _combine_ln_kernel(p1_ref, p2_ref, y_hbm, h_ref, route_ref, g_ref, b_ref, o_ref, buf, sem, *, tile, alpha):
    i = pl.program_id(0)
    n = pl.num_programs(0)
    slot = i % 2

    def issue(t, sl):
        def body(g, carry):
            for u in range(GATHER_UNROLL):
                r = g * GATHER_UNROLL + u
                a = p1_ref[t * tile + r]
                b = p2_ref[t * tile + r]
                pltpu.make_async_copy(y_hbm.at[pl.ds(a, 1), :], buf.at[sl, 0, pl.ds(r, 1), :], sem.at[sl]).start()
                pltpu.make_async_copy(y_hbm.at[pl.ds(b, 1), :], buf.at[sl, 1, pl.ds(r, 1), :], sem.at[sl]).start()
            return carry
        lax.fori_loop(0, tile // GATHER_UNROLL, body, 0)

    @pl.when(i == 0)
    def _():
        issue(i, slot)

    @pl.when(i + 1 < n)
    def _():
        issue(i + 1, 1 - slot)

    pltpu.make_async_copy(buf.at[slot], buf.at[slot], sem.at[slot]).wait()
    route = route_ref[...]
    z = alpha * h_ref[...] + (route[:, 2:3] * buf[slot, 0] + route[:, 3:4] * buf[slot, 1])
    o_ref[...] = _layer_norm(z, g_ref[...], b_ref[...])


def _combine_ln(pos1, pos2, y_sorted, h, route, g, b, alpha):
    r, d = h.shape
    tile = GATHER_TILE
    assert r % tile == 0 and tile % GATHER_UNROLL == 0
    grid_spec = pltpu.PrefetchScalarGridSpec(
        num_scalar_prefetch=2,
        grid=(r // tile,),
        in_specs=[
            pl.BlockSpec(memory_space=pl.ANY),
            pl.BlockSpec((tile, d), lambda i, p1, p2: (i, 0)),
            pl.BlockSpec((tile, LANE), lambda i, p1, p2: (i, 0)),
            pl.BlockSpec((1, d), lambda i, p1, p2: (0, 0)),
            pl.BlockSpec((1, d), lambda i, p1, p2: (0, 0)),
        ],
        out_specs=pl.BlockSpec((tile, d), lambda i, p1, p2: (i, 0)),
        scratch_shapes=[pltpu.VMEM((2, 2, tile, d), F32), pltpu.SemaphoreType.DMA((2,))],
    )
    return pl.pallas_call(
        functools.partial(_combine_ln_kernel, tile=tile, alpha=alpha),
        grid_spec=grid_spec,
        out_shape=jax.ShapeDtypeStruct((r, d), F32),
        compiler_params=_cp(("arbitrary",), VMEM_LIMIT),
        name="combine_ln",
    )(pos1, pos2, y_sorted, h, route, g, b)


def _route_plan(route, n_experts, tm):
    r = route.shape[0]
    e = jnp.concatenate([route[:, 0], route[:, 1]]).astype(jnp.int32)
    onehot = (e[:, None] == jnp.arange(n_experts, dtype=jnp.int32)[None, :]).astype(jnp.int32)
    csum = jnp.cumsum(onehot, axis=0)
    rank = jnp.take_along_axis(csum, e[:, None], axis=1)[:, 0] - 1
    tiles = (csum[-1] + tm - 1) // tm
    tile_end = jnp.cumsum(tiles)
    pos = (tile_end - tiles)[e] * tm + rank
    n_tiles = (TOP_K * r) // tm + n_experts
    tok = jnp.arange(TOP_K * r, dtype=jnp.int32) % r
    src = jnp.zeros((n_tiles * tm,), jnp.int32).at[pos].set(tok, unique_indices=True)
    n_used = tile_end[-1]
    tile_id = jnp.minimum(jnp.arange(n_tiles, dtype=jnp.int32), n_used - 1)
    tile_expert = jnp.sum((tile_end[None, :] <= tile_id[:, None]).astype(jnp.int32), axis=1)
    return src, tile_expert, n_used.reshape(1).astype(jnp.int32), pos[:r], pos[r:]


def _pad_cols(w, offset, width):
    return jnp.pad(w, ((0, 0), (offset, width - offset - w.shape[1])))


def kernel(x_prompt, x_sample, state_ret, cache_ckv, cache_kpe, page_table, meta_tokens, ln_g, ln_b, ret_w_in, ret_gn_g, ret_w_o, mla_w_dq, mla_q_norm, mla_w_uq, mla_w_o, kv_w_a, kv_norm, kv_w_b, ffn_w13, ffn_w2, moe_w_r, moe_b_r, moe_w13, moe_w2):
    bsz, seq, d = x_prompt.shape
    dec_b, dec_seq, _ = x_sample.shape
    n_ret, _, ret_h, dk, dv = state_ret.shape
    depth = ln_g.shape[0]
    assert depth == 2 and n_ret == 1 and mla_w_dq.shape[0] == 1, "layer pattern: one retention layer then one MLA layer"
    past_len = page_table.shape[1] * PAGE_SIZE
    lora, mla_h, nope_v = kv_w_b.shape
    rope = kv_w_a.shape[1] - lora
    nope = mla_w_uq.shape[2] // mla_h - rope
    v_dim = nope_v - nope
    n_experts = moe_w_r.shape[2]
    assert nope == LANE // 2 and v_dim == LANE // 2 and rope <= LANE // 4
    alpha = (2 * depth) ** 0.25
    mla_scale = (nope + rope) ** -0.5
    half = rope // 2

    n_p, n_s = bsz * seq, dec_b * dec_seq
    meta_row0 = n_p + n_s
    r_real = meta_row0 + N_META
    r_pad = -(-r_real // ROW_TILE) * ROW_TILE
    assert r_pad - meta_row0 >= LANE

    x0 = jnp.concatenate([x_prompt.reshape(n_p, d), x_sample.reshape(n_s, d), meta_tokens,
                          jnp.zeros((r_pad - r_real, d), F32)], axis=0)
    pos = jnp.concatenate([jnp.tile(N_META + jnp.arange(seq), bsz), jnp.tile(past_len + jnp.arange(dec_seq), dec_b),
                           jnp.arange(N_META), jnp.zeros((r_pad - r_real,), jnp.int32)]).astype(F32)

    inv_r = ROPE_BASE ** (-jnp.arange(dk // 2, dtype=F32) / (dk // 2))
    ang_r = pos[:, None] * inv_r[None, :]
    cos_r, sin_r = jnp.cos(ang_r), jnp.sin(ang_r)
    inv_m = ROPE_BASE ** (-jnp.arange(half, dtype=F32) / half)
    ang_m = pos[:, None] * inv_m[None, :]
    cos_m = jnp.concatenate([jnp.cos(ang_m)] * 2, axis=1)
    sin_m = jnp.concatenate([jnp.sin(ang_m)] * 2, axis=1)
    cos_a, sin_a = _pad_cols(cos_m, 0, LANE), _pad_cols(sin_m, 0, LANE)
    cos_b, sin_b = _pad_cols(cos_m, nope, LANE), _pad_cols(sin_m, nope, LANE)
    q_scale = mla_scale * LOG2E
    q_ctab = (_pad_cols(jnp.ones((r_pad, nope), F32), 0, LANE) + cos_b) * q_scale
    q_stab = sin_b * q_scale

    def rot(w):
        return jnp.concatenate([-w[..., half:], w[..., :half]], axis=-1)

    qkvg = _ret_proj(x0, ret_w_in[0], cos_r, sin_r, ret_h, dk)
    gn = ret_gn_g[0].reshape(1, -1)
    mix_p, mix_meta, st_p = _ret_prompt(qkvg, gn, _decay_tables(ret_h, dk, RET_CHUNK, RET_CHUNK),
                                        bsz, seq, meta_row0, ret_h, dk, dv)
    mix_s, st_s = _ret_sample(qkvg, state_ret[0], gn,
                              _decay_tables(ret_h, dk, SAMPLE_BATCH_TILE * dec_seq, dec_seq),
                              n_p, dec_b, dec_seq, ret_h, dk, dv)
    mix = jnp.concatenate([mix_p, mix_s, mix_meta, jnp.zeros((r_pad - r_real, mix_p.shape[1]), BF16)], axis=0)
    row = lambda v: v.reshape(1, -1)
    h1, h1b = _proj_ln(mix, ret_w_o[0], x0, row(ln_g[0, 0]), row(ln_b[0, 0]), alpha)
    h2, h2b = _ffn_ln(h1b, ffn_w13[0].astype(BF16), ffn_w2[0].astype(BF16), h1, row(ln_g[0, 1]), row(ln_b[0, 1]), alpha)

    wc, wpe = kv_w_a[:, :lora], kv_w_a[:, lora:]
    wkv = jnp.concatenate([wc, _pad_cols(wpe, 0, LANE), _pad_cols(rot(wpe), 0, LANE),
                           _pad_cols(wpe, nope, LANE), _pad_cols(rot(wpe), nope, LANE)], axis=1).astype(BF16)
    wuk = jnp.pad(kv_w_b[:, :, :nope], ((0, 0), (0, 0), (0, LANE - nope))).reshape(lora, mla_h * LANE).astype(BF16)
    even = (jnp.arange(mla_h) % 2 == 0)[None, :, None]
    wuv_h = kv_w_b[:, :, nope:]
    wuv = jnp.where(even, jnp.pad(wuv_h, ((0, 0), (0, 0), (0, LANE - v_dim))),
                    jnp.pad(wuv_h, ((0, 0), (0, 0), (LANE - v_dim, 0)))).reshape(lora, mla_h * LANE).astype(BF16)
    lane_id = jnp.arange(LANE)[None, None, :]
    ones_row = jnp.where(even, lane_id == v_dim, lane_id == 0).astype(F32).reshape(1, mla_h * LANE)
    ckv, kpe, k_heads, v_heads = _kv_proj(h2b, wkv, row(kv_norm), (cos_a, sin_a, cos_b, sin_b), wuk, wuv, ones_row,
                                          lora, rope, mla_h)

    wuq = mla_w_uq[0].reshape(-1, mla_h, nope + rope)
    wq_n, wq_r = wuq[:, :, :nope], wuq[:, :, nope:]
    pad_h = lambda w, off: jnp.pad(w, ((0, 0), (0, 0), (off, LANE - off - w.shape[2]))).reshape(w.shape[0], mla_h * LANE)
    w_q1 = (pad_h(wq_n, 0) + pad_h(wq_r, nope)).astype(BF16)
    w_q2 = pad_h(rot(wq_r), nope).astype(BF16)
    wdq = mla_w_dq[0].astype(BF16)
    qn_g = row(mla_q_norm[0])
    q_heads = _q_proj(h2b, wdq, qn_g, w_q1, w_q2, q_ctab, q_stab, mla_h)
    attn_p = _attn_prompt(q_heads, k_heads, v_heads, bsz, seq, meta_row0, mla_h)

    wukt = jnp.pad(jnp.transpose(kv_w_b[:, :, :nope], (1, 2, 0)), ((0, 0), (0, LANE - nope), (0, 0))).astype(BF16)
    w_pe = wq_r.reshape(-1, mla_h * rope).astype(BF16)
    w_pe_rot = rot(wq_r).reshape(-1, mla_h * rope).astype(BF16)
    cs = jnp.tile(cos_m[n_p:n_p + n_s], (1, mla_h))
    ss = jnp.tile(sin_m[n_p:n_p + n_s], (1, mla_h))
    qabs, qpe = _q_sample(h2b[n_p:n_p + n_s], wdq, qn_g, pad_h(wq_n, 0).astype(BF16), wukt, w_pe, w_pe_rot, cs, ss,
                          mla_h, lora, q_scale)
    ckv_s, kpe_s = ckv[n_p:n_p + n_s], kpe[n_p:n_p + n_s]
    new_pad = lambda a: jnp.pad(a.reshape(dec_b, dec_seq, -1), ((0, 0), (0, LANE - dec_seq), (0, 0))).astype(BF16)
    o_lat = _attn_paged(page_table, qabs.reshape(dec_b, dec_seq * mla_h, lora), qpe.reshape(dec_b, dec_seq * mla_h, rope),
                        new_pad(ckv_s), new_pad(kpe_s), cache_ckv, jnp.swapaxes(cache_kpe, 1, 2), mla_h, dec_seq)
    wuv_t = jnp.transpose(wuv_h, (1, 0, 2))
    wuv_pad = jnp.where(jnp.transpose(even, (1, 0, 2)), jnp.pad(wuv_t, ((0, 0), (0, 0), (0, LANE - v_dim))),
                        jnp.pad(wuv_t, ((0, 0), (0, 0), (LANE - v_dim, 0)))).astype(BF16)
    attn_s = _uv_sample(o_lat.reshape(n_s, mla_h * lora), wuv_pad, mla_h, lora)
    attn = jnp.concatenate([attn_p, attn_s, jnp.zeros((r_pad - meta_row0, attn_p.shape[1]), BF16)], axis=0)
    h3, h3b = _proj_ln(attn, mla_w_o[0], h2, row(ln_g[1, 0]), row(ln_b[1, 0]), alpha)

    w_r = _pad_cols(moe_w_r[0], 0, LANE).astype(BF16)
    b_r = _pad_cols(moe_b_r[0].reshape(1, -1), 0, LANE)
    route = _router(h3b, w_r, b_r, n_experts)
    src, tile_expert, n_used, pos1, pos2 = _route_plan(route, n_experts, MOE_ROW_TILE)
    x_sorted = _gather_rows(src, h3, BF16)
    y_sorted = _moe_ffn(tile_expert, n_used, x_sorted, moe_w13[0].astype(BF16), moe_w2[0].astype(BF16))
    h4 = _combine_ln(pos1, pos2, y_sorted, h3, route, row(ln_g[1, 1]), row(ln_b[1, 1]), alpha)

    y_prompt = h4[:n_p].reshape(bsz, seq, d)
    y_sample = h4[n_p:n_p + n_s].reshape(dec_b, dec_seq, d)

    def with_meta(a):
        meta = jnp.broadcast_to(a[meta_row0:r_real][None], (bsz, N_META, a.shape[1]))
        return jnp.concatenate([meta, a[:n_p].reshape(bsz, seq, -1)], axis=1)

    return (y_prompt, y_sample, st_p[None], with_meta(ckv), with_meta(kpe), st_s[None],
            ckv_s.reshape(dec_b, dec_seq, lora), kpe_s.reshape(dec_b, dec_seq, rope))
```

```python
import functools
import math

import jax
import jax.numpy as jnp
from jax import lax
from jax.experimental import pallas as pl
from jax.experimental.pallas import tpu as pltpu

F32 = jnp.float32
BF16 = jnp.bfloat16

N_META = 16
TOP_K = 2
ROPE_BASE = 10000.0
LN_EPS = 1e-5
RMS_EPS = 1e-6
PAGE_SIZE = 128
RET_CHUNK = 128

LANE = 128
ROW_TILE = 512
RET_PROJ_TILE = 1024
ATTN_TQ = 512
ATTN_TK = 512
FFN_CHUNK = 256
MOE_ROW_TILE = 512
MOE_FF_CHUNK = 512
GATHER_TILE = 256
GATHER_UNROLL = 8
PAGES_PER_CHUNK = 32
PAGED_KEY_BLOCK = 1024
PAGED_SCORES_AHEAD = 8
LOG2E = math.log2(math.e)
SAMPLE_BATCH_TILE = 4
VMEM_LIMIT = 56 * 1024 * 1024


def _cp(sem, vmem=None):
    return pltpu.CompilerParams(dimension_semantics=sem, vmem_limit_bytes=vmem)


def _layer_norm(z, g, b):
    mu = jnp.mean(z, axis=-1, keepdims=True)
    zc = z - mu
    var = jnp.mean(zc * zc, axis=-1, keepdims=True)
    return zc * lax.rsqrt(var + LN_EPS) * g + b


def _rms_norm(c, g):
    return c * lax.rsqrt(jnp.mean(c * c, axis=-1, keepdims=True) + RMS_EPS) * g


def _silu(x):
    return x * (1.0 / (1.0 + jnp.exp(-x)))


def _dot(a, b):
    return jnp.dot(a, b, preferred_element_type=F32)


def _dot_nt(a, b):
    return lax.dot_general(a, b, (((1,), (1,)), ((), ())), preferred_element_type=F32)


def _dot_tn(a, b):
    return lax.dot_general(a, b, (((0,), (0,)), ((), ())), preferred_element_type=F32)


def _ret_proj_kernel(x_ref, w_ref, cos_ref, sin_ref, o_ref, wbf_ref, *, n_heads, dk, k_scale):
    j = pl.program_id(0)
    i = pl.program_id(1)

    @pl.when(i == 0)
    def _():
        wbf_ref[...] = w_ref[...].astype(BF16)

    y = _dot(x_ref[...].astype(BF16), wbf_ref[...])
    half = dk // 2

    @pl.when(j < 2)
    def _():
        c = cos_ref[...]
        s = sin_ref[...]
        scale = jnp.where(j == 1, k_scale, 1.0).astype(F32)
        for h in range(n_heads):
            x1 = y[:, h * dk:h * dk + half]
            x2 = y[:, h * dk + half:(h + 1) * dk]
            o_ref[:, h * dk:h * dk + half] = ((x1 * c - x2 * s) * scale).astype(BF16)
            o_ref[:, h * dk + half:(h + 1) * dk] = ((x1 * s + x2 * c) * scale).astype(BF16)

    @pl.when(j >= 2)
    def _():
        o_ref[...] = y.astype(BF16)


def _ret_proj(x, w_in, cos_r, sin_r, n_heads, dk):
    r, d = x.shape
    n = w_in.shape[1]
    qk = n_heads * dk
    assert n % qk == 0 and dk // 2 == LANE
    kern = functools.partial(_ret_proj_kernel, n_heads=n_heads, dk=dk, k_scale=dk ** -0.5)
    tile = RET_PROJ_TILE if r % RET_PROJ_TILE == 0 else ROW_TILE
    return pl.pallas_call(
        kern,
        grid=(n // qk, r // tile),
        in_specs=[
            pl.BlockSpec((tile, d), lambda j, i: (i, 0)),
            pl.BlockSpec((d, qk), lambda j, i: (0, j)),
            pl.BlockSpec((tile, LANE), lambda j, i: (i, 0)),
            pl.BlockSpec((tile, LANE), lambda j, i: (i, 0)),
        ],
        out_specs=pl.BlockSpec((tile, qk), lambda j, i: (i, j)),
        out_shape=jax.ShapeDtypeStruct((r, n), BF16),
        scratch_shapes=[pltpu.VMEM((d, qk), BF16)],
        compiler_params=_cp(("arbitrary", "arbitrary"), VMEM_LIMIT),
        name="ret_proj",
    )(x, w_in, cos_r, sin_r)


def _group_norm_gate(o, g, gn):
    mu = jnp.mean(o, axis=-1, keepdims=True)
    oc = o - mu
    var = jnp.mean(oc * oc, axis=-1, keepdims=True)
    return (_silu(g.astype(F32)) * (oc * lax.rsqrt(var + LN_EPS) * gn)).astype(BF16)


def _ret_prompt_kernel(q_ref, k_ref, v_ref, g_ref, qm_ref, km_ref, vm_ref, gm_ref, gn_ref,
                       dec_ref, rdec_ref, cdec_ref, cpow_ref,
                       o_ref, om_ref, st_ref, s_ref, *, n_heads, dk, dv, chunk, n_meta, tile):
    t = pl.program_id(1)
    nt = pl.num_programs(1)

    @pl.when(t == 0)
    def _():
        for h in range(n_heads):
            q = qm_ref[:, h * dk:(h + 1) * dk]
            k = km_ref[:, h * dk:(h + 1) * dk]
            v = vm_ref[:, h * dv:(h + 1) * dv]
            sc = _dot_nt(q, k) * dec_ref[h, :n_meta, :n_meta]
            o = _dot(sc.astype(BF16), v)
            om_ref[:, h * dv:(h + 1) * dv] = _group_norm_gate(
                o, gm_ref[:, h * dv:(h + 1) * dv], gn_ref[:, h * dv:(h + 1) * dv])
            k_out = (k.astype(F32) * cdec_ref[h, chunk - n_meta:, :]).astype(BF16)
            s_ref[h] = _dot_tn(k_out, v)

    for h in range(n_heads):
        for c in range(tile // chunk):
            rows = slice(c * chunk, (c + 1) * chunk)
            q = q_ref[rows, h * dk:(h + 1) * dk]
            k = k_ref[rows, h * dk:(h + 1) * dk]
            v = v_ref[rows, h * dv:(h + 1) * dv]
            s_old = s_ref[h]
            sc = _dot_nt(q, k) * dec_ref[h]
            q_in = (q.astype(F32) * rdec_ref[h]).astype(BF16)
            o = _dot(sc.astype(BF16), v) + _dot(q_in, s_old.astype(BF16))
            k_out = (k.astype(F32) * cdec_ref[h]).astype(BF16)
            s_ref[h] = cpow_ref[h] * s_old + _dot_tn(k_out, v)
            o_ref[rows, h * dv:(h + 1) * dv] = _group_norm_gate(
                o, g_ref[rows, h * dv:(h + 1) * dv], gn_ref[:, h * dv:(h + 1) * dv])

    @pl.when(t == nt - 1)
    def _():
        st_ref[0] = s_ref[...]


def _ret_prompt(qkvg, gn_g, tabs, bsz, seq, meta_row0, n_heads, dk, dv):
    qk, vw = n_heads * dk, n_heads * dv
    tile = ROW_TILE
    assert seq % tile == 0 and tile % RET_CHUNK == 0 and meta_row0 % N_META == 0 and vw == 2 * qk
    nt = seq // tile
    mb = meta_row0 // N_META
    dec, rdec, cdec, cpow = tabs
    kern = functools.partial(_ret_prompt_kernel, n_heads=n_heads, dk=dk, dv=dv, chunk=RET_CHUNK,
                             n_meta=N_META, tile=tile)
    full = lambda *shape: pl.BlockSpec(shape, lambda b, t: (0,) * len(shape))
    return pl.pallas_call(
        kern,
        grid=(bsz, nt),
        in_specs=[
            pl.BlockSpec((tile, qk), lambda b, t: (b * nt + t, 0)),
            pl.BlockSpec((tile, qk), lambda b, t: (b * nt + t, 1)),
            pl.BlockSpec((tile, vw), lambda b, t: (b * nt + t, 1)),
            pl.BlockSpec((tile, vw), lambda b, t: (b * nt + t, 2)),
            pl.BlockSpec((N_META, qk), lambda b, t: (mb, 0)),
            pl.BlockSpec((N_META, qk), lambda b, t: (mb, 1)),
            pl.BlockSpec((N_META, vw), lambda b, t: (mb, 1)),
            pl.BlockSpec((N_META, vw), lambda b, t: (mb, 2)),
            full(1, vw),
            full(n_heads, RET_CHUNK, RET_CHUNK),
            full(n_heads, RET_CHUNK, dk),
            full(n_heads, RET_CHUNK, dk),
            full(n_heads, 1, 1),
        ],
        out_specs=[
            pl.BlockSpec((tile, vw), lambda b, t: (b * nt + t, 0)),
            pl.BlockSpec((N_META, vw), lambda b, t: (0, 0)),
            pl.BlockSpec((1, n_heads, dk, dv), lambda b, t: (b, 0, 0, 0)),
        ],
        out_shape=[
            jax.ShapeDtypeStruct((bsz * seq, vw), BF16),
            jax.ShapeDtypeStruct((N_META, vw), BF16),
            jax.ShapeDtypeStruct((bsz, n_heads, dk, dv), F32),
        ],
        scratch_shapes=[pltpu.VMEM((n_heads, dk, dv), F32)],
        compiler_params=_cp(("arbitrary", "arbitrary"), VMEM_LIMIT),
        name="ret_prompt",
    )(qkvg, qkvg, qkvg, qkvg, qkvg, qkvg, qkvg, qkvg, gn_g, dec, rdec, cdec, cpow)


def _ret_sample_kernel(q_ref, k_ref, v_ref, g_ref, s0_ref, gn_ref, dec_ref, rdec_ref, cdec_ref, cpow_ref,
                       o_ref, st_ref, *, n_heads, dk, dv, nb, dec_seq):
    rows = nb * dec_seq
    row_b = lax.broadcasted_iota(jnp.int32, (rows, 1), 0) // dec_seq
    for h in range(n_heads):
        q = q_ref[:, h * dk:(h + 1) * dk]
        k = k_ref[:, h * dk:(h + 1) * dk]
        v = v_ref[:, h * dv:(h + 1) * dv]
        sc = _dot_nt(q, k) * dec_ref[h]
        o = _dot(sc.astype(BF16), v)
        q_in = q.astype(F32) * rdec_ref[h]
        k_out = k.astype(F32) * cdec_ref[h]
        for b in range(nb):
            s_old = s0_ref[0, b, h]
            sel = row_b == b
            o = o + _dot(jnp.where(sel, q_in, 0.0).astype(BF16), s_old.astype(BF16))
            st_ref[0, b, h] = cpow_ref[h] * s_old + _dot_tn(jnp.where(sel, k_out, 0.0).astype(BF16), v)
        o_ref[:, h * dv:(h + 1) * dv] = _group_norm_gate(
            o, g_ref[:, h * dv:(h + 1) * dv], gn_ref[:, h * dv:(h + 1) * dv])


def _ret_sample(qkvg, state, gn_g, tabs, row0, dec_b, dec_seq, n_heads, dk, dv):
    qk, vw = n_heads * dk, n_heads * dv
    nb = SAMPLE_BATCH_TILE
    rows = nb * dec_seq
    assert dec_b % nb == 0 and row0 % rows == 0 and rows % 16 == 0
    rb = row0 // rows
    dec, rdec, cdec, cpow = tabs
    state5 = state.reshape(dec_b // nb, nb, n_heads, dk, dv)
    kern = functools.partial(_ret_sample_kernel, n_heads=n_heads, dk=dk, dv=dv, nb=nb, dec_seq=dec_seq)
    full = lambda *shape: pl.BlockSpec(shape, lambda i: (0,) * len(shape))
    out, st = pl.pallas_call(
        kern,
        grid=(dec_b // nb,),
        in_specs=[
            pl.BlockSpec((rows, qk), lambda i: (rb + i, 0)),
            pl.BlockSpec((rows, qk), lambda i: (rb + i, 1)),
            pl.BlockSpec((rows, vw), lambda i: (rb + i, 1)),
            pl.BlockSpec((rows, vw), lambda i: (rb + i, 2)),
            pl.BlockSpec((1, nb, n_heads, dk, dv), lambda i: (i, 0, 0, 0, 0)),
            full(1, vw),
            full(n_heads, rows, rows),
            full(n_heads, rows, dk),
            full(n_heads, rows, dk),
            full(n_heads, 1, 1),
        ],
        out_specs=[
            pl.BlockSpec((rows, vw), lambda i: (i, 0)),
            pl.BlockSpec((1, nb, n_heads, dk, dv), lambda i: (i, 0, 0, 0, 0)),
        ],
        out_shape=[
            jax.ShapeDtypeStruct((dec_b * dec_seq, vw), BF16),
            jax.ShapeDtypeStruct(state5.shape, F32),
        ],
        compiler_params=_cp(("arbitrary",), VMEM_LIMIT),
        name="ret_sample",
    )(qkvg, qkvg, qkvg, qkvg, state5, gn_g, dec, rdec, cdec, cpow)
    return out, st.reshape(dec_b, n_heads, dk, dv)


def _decay_tables(n_heads, dk, chunk, group):
    lg = jnp.log1p(-jnp.exp2(-5.0 - jnp.arange(n_heads, dtype=F32)))
    r = jnp.arange(chunk)
    t = (r % group).astype(F32)
    diff = t[:, None] - t[None, :]
    same = (r[:, None] // group) == (r[None, :] // group)
    dec = jnp.where(same & (diff >= 0), jnp.exp(jnp.maximum(diff, 0.0)[None] * lg[:, None, None]), 0.0)
    rdec = jnp.exp((t[None, :] + 1.0) * lg[:, None])
    cdec = jnp.exp((group - 1.0 - t)[None, :] * lg[:, None])
    cpow = jnp.exp(group * lg)
    bc = lambda a: jnp.broadcast_to(a[:, :, None], (n_heads, chunk, dk))
    return dec, bc(rdec), bc(cdec), cpow.reshape(n_heads, 1, 1)


def _proj_ln_kernel(a_ref, w_ref, h_ref, g_ref, b_ref, o_ref, obf_ref, wbf_ref, *, alpha):
    @pl.when(pl.program_id(0) == 0)
    def _():
        wbf_ref[...] = w_ref[...].astype(BF16)

    z = alpha * h_ref[...] + _dot(a_ref[...], wbf_ref[...])
    y = _layer_norm(z, g_ref[...], b_ref[...])
    o_ref[...] = y
    obf_ref[...] = y.astype(BF16)


def _proj_ln(a, w, h, g, b, alpha):
    r, k = a.shape
    d = w.shape[1]
    row = lambda n: pl.BlockSpec((ROW_TILE, n), lambda i: (i, 0))
    full = lambda *shape: pl.BlockSpec(shape, lambda i: (0,) * len(shape))
    return pl.pallas_call(
        functools.partial(_proj_ln_kernel, alpha=alpha),
        grid=(r // ROW_TILE,),
        in_specs=[row(k), full(k, d), row(d), full(1, d), full(1, d)],
        out_specs=[row(d), row(d)],
        out_shape=[jax.ShapeDtypeStruct((r, d), F32), jax.ShapeDtypeStruct((r, d), BF16)],
        scratch_shapes=[pltpu.VMEM((k, d), BF16)],
        compiler_params=_cp(("arbitrary",), VMEM_LIMIT),
        name="proj_ln",
    )(a, w, h, g, b)


def _ffn_ln_kernel(x_ref, w13_ref, w2_ref, h_ref, g_ref, b_ref, o_ref, obf_ref, *, alpha, d_ff, chunk):
    x = x_ref[...]
    acc = alpha * h_ref[...]
    for c in range(d_ff // chunk):
        gate = _dot(x, w13_ref[:, c * chunk:(c + 1) * chunk])
        up = _dot(x, w13_ref[:, d_ff + c * chunk:d_ff + (c + 1) * chunk])
        acc = acc + _dot((_silu(gate) * up).astype(BF16), w2_ref[c * chunk:(c + 1) * chunk, :])
    y = _layer_norm(acc, g_ref[...], b_ref[...])
    o_ref[...] = y
    obf_ref[...] = y.astype(BF16)


def _ffn_ln(x, w13, w2, h, g, b, alpha):
    r, d = x.shape
    d_ff = w2.shape[0]
    assert d_ff % FFN_CHUNK == 0
    row = lambda n: pl.BlockSpec((ROW_TILE, n), lambda i: (i, 0))
    full = lambda *shape: pl.BlockSpec(shape, lambda i: (0,) * len(shape))
    return pl.pallas_call(
        functools.partial(_ffn_ln_kernel, alpha=alpha, d_ff=d_ff, chunk=FFN_CHUNK),
        grid=(r // ROW_TILE,),
        in_specs=[row(d), full(d, 2 * d_ff), full(d_ff, d), row(d), full(1, d), full(1, d)],
        out_specs=[row(d), row(d)],
        out_shape=[jax.ShapeDtypeStruct((r, d), F32), jax.ShapeDtypeStruct((r, d), BF16)],
        compiler_params=_cp(("arbitrary",), VMEM_LIMIT),
        name="ffn_ln",
    )(x, w13, w2, h, g, b)


def _kv_kernel(x_ref, wkv_ref, g_ref, ca_ref, sa_ref, cb_ref, sb_ref, wuk_ref, wuv_ref, ones_ref,
               ckv_ref, kpe_ref, k_ref, v_ref, *, lora, rope, n_heads):
    y = _dot(x_ref[...], wkv_ref[...])
    cn = _rms_norm(y[:, :lora], g_ref[...])
    ckv_ref[...] = cn
    kpe_a = y[:, lora:lora + LANE] * ca_ref[...] + y[:, lora + LANE:lora + 2 * LANE] * sa_ref[...]
    kpe_ref[...] = kpe_a[:, :rope]
    kpe_b = y[:, lora + 2 * LANE:lora + 3 * LANE] * cb_ref[...] + y[:, lora + 3 * LANE:lora + 4 * LANE] * sb_ref[...]
    cb = cn.astype(BF16)
    kn = _dot(cb, wuk_ref[...])
    for h in range(n_heads):
        k_ref[:, h * LANE:(h + 1) * LANE] = (kn[:, h * LANE:(h + 1) * LANE] + kpe_b).astype(BF16)
    v_ref[...] = (_dot(cb, wuv_ref[...]) + ones_ref[...]).astype(BF16)


def _kv_proj(x, wkv, g, tabs, wuk, wuv, ones_row, lora, rope, n_heads):
    r, d = x.shape
    nv = wuv.shape[1]
    row = lambda n: pl.BlockSpec((ROW_TILE, n), lambda i: (i, 0))
    full = lambda *shape: pl.BlockSpec(shape, lambda i: (0,) * len(shape))
    return pl.pallas_call(
        functools.partial(_kv_kernel, lora=lora, rope=rope, n_heads=n_heads),
        grid=(r // ROW_TILE,),
        in_specs=[row(d), full(*wkv.shape), full(1, lora), row(LANE), row(LANE), row(LANE), row(LANE),
                  full(*wuk.shape), full(*wuv.shape), full(1, nv)],
        out_specs=[row(lora), row(rope), row(n_heads * LANE), row(nv)],
        out_shape=[jax.ShapeDtypeStruct((r, lora), F32), jax.ShapeDtypeStruct((r, rope), F32),
                   jax.ShapeDtypeStruct((r, n_heads * LANE), BF16), jax.ShapeDtypeStruct((r, nv), BF16)],
        compiler_params=_cp(("arbitrary",), VMEM_LIMIT),
        name="kv_proj",
    )(x, wkv, g, *tabs, wuk, wuv, ones_row)


def _q_kernel(x_ref, wdq_ref, g_ref, w1_ref, w2_ref, c_ref, s_ref, q_ref, *, n_heads):
    qc = _rms_norm(_dot(x_ref[...], wdq_ref[...]), g_ref[...]).astype(BF16)
    a = _dot(qc, w1_ref[...])
    b = _dot(qc, w2_ref[...])
    c = c_ref[...]
    s = s_ref[...]
    for h in range(n_heads):
        cols = slice(h * LANE, (h + 1) * LANE)
        q_ref[:, cols] = (a[:, cols] * c + b[:, cols] * s).astype(BF16)


def _q_proj(x, wdq, g, w1, w2, c_tab, s_tab, n_heads):
    r, d = x.shape
    ql = wdq.shape[1]
    row = lambda n: pl.BlockSpec((ROW_TILE, n), lambda i: (i, 0))
    full = lambda *shape: pl.BlockSpec(shape, lambda i: (0,) * len(shape))
    return pl.pallas_call(
        functools.partial(_q_kernel, n_heads=n_heads),
        grid=(r // ROW_TILE,),
        in_specs=[row(d), full(d, ql), full(1, ql), full(*w1.shape), full(*w2.shape), row(LANE), row(LANE)],
        out_specs=row(n_heads * LANE),
        out_shape=jax.ShapeDtypeStruct((r, n_heads * LANE), BF16),
        compiler_params=_cp(("arbitrary",), VMEM_LIMIT),
        name="q_proj",
    )(x, wdq, g, w1, w2, c_tab, s_tab)


def _q_sample_kernel(x_ref, wdq_ref, g_ref, w1_ref, wukt_ref, wpe_ref, wrot_ref, c_ref, s_ref,
                     qabs_ref, qpe_ref, *, n_heads, lora, scale):
    qc = _rms_norm(_dot(x_ref[...], wdq_ref[...]), g_ref[...]).astype(BF16)
    qn = (_dot(qc, w1_ref[...]) * scale).astype(BF16)
    for h in range(n_heads):
        qabs_ref[:, h * lora:(h + 1) * lora] = _dot(qn[:, h * LANE:(h + 1) * LANE], wukt_ref[h]).astype(BF16)
    qpe = _dot(qc, wpe_ref[...]) * c_ref[...] + _dot(qc, wrot_ref[...]) * s_ref[...]
    qpe_ref[...] = (qpe * scale).astype(BF16)


def _q_sample(x, wdq, g, w1, wukt, wpe, wrot, c_tab, s_tab, n_heads, lora, scale):
    ns = x.shape[0]
    args = (x, wdq, g, w1, wukt, wpe, wrot, c_tab, s_tab)
    full = lambda a: pl.BlockSpec(a.shape, lambda i, nd=a.ndim: (0,) * nd)
    return pl.pallas_call(
        functools.partial(_q_sample_kernel, n_heads=n_heads, lora=lora, scale=scale),
        grid=(1,),
        in_specs=[full(a) for a in args],
        out_specs=[pl.BlockSpec((ns, n_heads * lora), lambda i: (0, 0)),
                   pl.BlockSpec((ns, wpe.shape[1]), lambda i: (0, 0))],
        out_shape=[jax.ShapeDtypeStruct((ns, n_heads * lora), BF16),
                   jax.ShapeDtypeStruct((ns, wpe.shape[1]), BF16)],
        compiler_params=_cp(("arbitrary",), VMEM_LIMIT),
        name="q_sample",
    )(*args)


def _attn_step(m, acc, s, v):
    m_new = jnp.maximum(m, jnp.max(s, axis=-1, keepdims=True))
    p = jnp.exp2((s - m_new).astype(BF16))
    return m_new, jnp.exp2(m - m_new) * acc + _dot(p, v)


def _attn_prompt_kernel(q_ref, k_ref, v_ref, km_ref, vm_ref, o_ref, *, tq, tk, n_meta):
    i = pl.program_id(2)
    lane = lax.broadcasted_iota(jnp.int32, (tq, LANE), 1)
    heads = (slice(0, LANE), slice(LANE, 2 * LANE))
    qs = [q_ref[:, h] for h in heads]

    carry = []
    for q, h in zip(qs, heads):
        s = jnp.where(lane < n_meta, _dot_nt(q, km_ref[:, h]), -jnp.inf)
        m = jnp.max(s, axis=-1, keepdims=True)
        carry += [m, _dot(jnp.exp2((s - m).astype(BF16)), vm_ref[:, h])]

    def tile(j, carry, masked):
        off = pl.multiple_of(j * tk, tk)
        out = []
        for e, (q, h) in enumerate(zip(qs, heads)):
            s = _dot_nt(q, k_ref[pl.ds(off, tk), h])
            if masked:
                row = lax.broadcasted_iota(jnp.int32, (tq, tk), 0)
                col = lax.broadcasted_iota(jnp.int32, (tq, tk), 1)
                s = jnp.where(col <= row, s, -jnp.inf)
            out += _attn_step(carry[2 * e], carry[2 * e + 1], s, v_ref[pl.ds(off, tk), h])
        return out

    carry = lax.fori_loop(0, i, lambda j, c: tuple(tile(j, c, False)), tuple(carry))
    _, acc0, _, acc1 = tile(i, carry, True)
    l0 = jnp.sum(jnp.where(lane == LANE // 2, acc0, 0.0), axis=-1, keepdims=True)
    l1 = jnp.sum(jnp.where(lane == 0, acc1, 0.0), axis=-1, keepdims=True)
    o_ref[...] = jnp.where(lane < LANE // 2, acc0 * (1.0 / l0), acc1 * (1.0 / l1)).astype(BF16)


def _attn_prompt(q, k, v, bsz, seq, meta_row0, n_heads):
    tq, tk = ATTN_TQ, ATTN_TK
    assert tq == tk and seq % tq == 0 and meta_row0 % LANE == 0 and n_heads % 2 == 0
    assert v.shape[1] == n_heads * LANE
    nq = seq // tq
    mb = meta_row0 // LANE
    pair = 2 * LANE
    return pl.pallas_call(
        functools.partial(_attn_prompt_kernel, tq=tq, tk=tk, n_meta=N_META),
        grid=(bsz, n_heads // 2, nq),
        in_specs=[
            pl.BlockSpec((tq, pair), lambda b, hp, i: (b * nq + i, hp)),
            pl.BlockSpec((seq, pair), lambda b, hp, i: (b, hp)),
            pl.BlockSpec((seq, pair), lambda b, hp, i: (b, hp)),
            pl.BlockSpec((LANE, pair), lambda b, hp, i: (mb, hp)),
            pl.BlockSpec((LANE, pair), lambda b, hp, i: (mb, hp)),
        ],
        out_specs=pl.BlockSpec((tq, LANE), lambda b, hp, i: (b * nq + i, hp)),
        out_shape=jax.ShapeDtypeStruct((bsz * seq, n_heads * LANE // 2), BF16),
        compiler_params=_cp(("arbitrary", "arbitrary", "arbitrary"), VMEM_LIMIT),
        name="attn_prompt",
    )(q, k, v, k, v)


def _softmax_part(s, v):
    m = jnp.max(s, axis=-1, keepdims=True)
    p = jnp.exp2(s - m)
    return m, jnp.sum(p, axis=-1, keepdims=True), _dot(p.astype(BF16), v)


def _softmax_merge(parts):
    m = functools.reduce(jnp.maximum, [p[0] for p in parts])
    w = [jnp.exp2(p[0] - m) for p in parts]
    l = functools.reduce(jnp.add, [wi * p[1] for wi, p in zip(w, parts)])
    acc = functools.reduce(jnp.add, [wi * p[2] for wi, p in zip(w, parts)])
    return m, l, acc


def _attn_paged_kernel(pt_ref, qabs_ref, qpe_ref, cnew_ref, pnew_ref, ckv_hbm, kpe_hbm, o_ref,
                       cbuf, pbuf, sem, m_ref, l_ref, acc_ref, *, ppc, n_pairs, n_heads, dec_seq, sub):
    c = pl.program_id(1)
    pair = pl.program_id(0) * n_pairs + c
    n_total = pl.num_programs(0) * n_pairs

    def copies(chunk, sl):
        base = chunk * ppc
        out = []
        for p in range(ppc):
            page = pt_ref[base + p]
            keys = pl.ds(p * PAGE_SIZE, PAGE_SIZE)
            out.append(pltpu.make_async_copy(ckv_hbm.at[page], cbuf.at[sl, keys, :], sem.at[0, sl]))
            out.append(pltpu.make_async_copy(kpe_hbm.at[page], pbuf.at[sl, :, keys], sem.at[1, sl]))
        return out

    def wait_slot(sl):
        pltpu.make_async_copy(cbuf.at[sl], cbuf.at[sl], sem.at[0, sl]).wait()
        pltpu.make_async_copy(pbuf.at[sl], pbuf.at[sl], sem.at[1, sl]).wait()

    @pl.when(pair == 0)
    def _():
        for cp in copies(0, 0):
            cp.start()

    @pl.when(c == 0)
    def _():
        m_ref[...] = jnp.full(m_ref.shape, -jnp.inf, F32)
        l_ref[...] = jnp.zeros(l_ref.shape, F32)
        acc_ref[...] = jnp.zeros(acc_ref.shape, F32)

    qa = qabs_ref[0]
    qp = qpe_ref[0]
    n_sub = ppc * PAGE_SIZE // sub

    def slot_parts(sl):
        def scores(i):
            keys = slice(i * sub, (i + 1) * sub)
            kc = cbuf[sl, keys, :].astype(BF16)
            return _dot_nt(qa, kc) + _dot(qp, pbuf[sl, :, keys].astype(BF16)), kc

        ahead = [scores(i) for i in range(min(PAGED_SCORES_AHEAD, n_sub))]
        out = []
        for i in range(n_sub):
            s, kc = ahead.pop(0)
            if i + PAGED_SCORES_AHEAD < n_sub:
                ahead.append(scores(i + PAGED_SCORES_AHEAD))
            out.append(_softmax_part(s, kc))
        return out

    wait_slot(0)
    for cp in copies(2 * pair + 1, 1):
        cp.start()
    parts = [(m_ref[...], l_ref[...], acc_ref[...])] + slot_parts(0)
    wait_slot(1)
    for cp in copies(jnp.where(pair + 1 < n_total, 2 * pair + 2, 0), 0):
        cp.start()
    m, l, acc = _softmax_merge(parts + slot_parts(1))
    m_ref[...] = m
    l_ref[...] = l
    acc_ref[...] = acc

    @pl.when(c == n_pairs - 1)
    def _():
        kn = cnew_ref[0]
        s = _dot_nt(qa, kn) + _dot_nt(qp, pnew_ref[0])
        rows = s.shape[0]
        tok = lax.broadcasted_iota(jnp.int32, (rows, LANE), 0) // n_heads
        col = lax.broadcasted_iota(jnp.int32, (rows, LANE), 1)
        s = jnp.where((col <= tok) & (col < dec_seq), s, -jnp.inf)
        _, l2, acc2 = _softmax_merge([(m, l, acc), _softmax_part(s, kn)])
        o_ref[0] = acc2 * (1.0 / l2)

    @pl.when(pair == n_total - 1)
    def _():
        wait_slot(0)


def _attn_paged(page_table, qabs, qpe, cnew, pnew, cache_ckv, cache_kpe_t, n_heads, dec_seq):
    dec_b, n_pages = page_table.shape
    assert n_pages % 2 == 0
    ppc = min(PAGES_PER_CHUNK, n_pages // 2)
    sub = min(PAGED_KEY_BLOCK, ppc * PAGE_SIZE)
    assert n_pages % (2 * ppc) == 0 and cache_ckv.shape[1] == PAGE_SIZE and (ppc * PAGE_SIZE) % sub == 0
    n_pairs = n_pages // (2 * ppc)
    rows, lora = qabs.shape[1], qabs.shape[2]
    rope = qpe.shape[2]
    kern = functools.partial(_attn_paged_kernel, ppc=ppc, n_pairs=n_pairs, n_heads=n_heads, dec_seq=dec_seq,
                             sub=sub)
    grid_spec = pltpu.PrefetchScalarGridSpec(
        num_scalar_prefetch=1,
        grid=(dec_b, n_pairs),
        in_specs=[
            pl.BlockSpec((1, rows, lora), lambda b, c, pt: (b, 0, 0)),
            pl.BlockSpec((1, rows, rope), lambda b, c, pt: (b, 0, 0)),
            pl.BlockSpec((1, LANE, lora), lambda b, c, pt: (b, 0, 0)),
            pl.BlockSpec((1, LANE, rope), lambda b, c, pt: (b, 0, 0)),
            pl.BlockSpec(memory_space=pl.ANY),
            pl.BlockSpec(memory_space=pl.ANY),
        ],
        out_specs=pl.BlockSpec((1, rows, lora), lambda b, c, pt: (b, 0, 0)),
        scratch_shapes=[
            pltpu.VMEM((2, ppc * PAGE_SIZE, lora), F32),
            pltpu.VMEM((2, rope, ppc * PAGE_SIZE), F32),
            pltpu.SemaphoreType.DMA((2, 2)),
            pltpu.VMEM((rows, 1), F32),
            pltpu.VMEM((rows, 1), F32),
            pltpu.VMEM((rows, lora), F32),
        ],
    )
    return pl.pallas_call(
        kern,
        grid_spec=grid_spec,
        out_shape=jax.ShapeDtypeStruct((dec_b, rows, lora), F32),
        compiler_params=_cp(("arbitrary", "arbitrary"), VMEM_LIMIT),
        name="attn_paged",
    )(page_table.reshape(-1), qabs, qpe, cnew, pnew, cache_ckv, cache_kpe_t)


def _uv_sample_kernel(o_ref, w_ref, out_ref, *, n_heads, lora):
    for p in range(n_heads // 2):
        acc = _dot(o_ref[:, (2 * p) * lora:(2 * p + 1) * lora].astype(BF16), w_ref[2 * p])
        acc = acc + _dot(o_ref[:, (2 * p + 1) * lora:(2 * p + 2) * lora].astype(BF16), w_ref[2 * p + 1])
        out_ref[:, p * LANE:(p + 1) * LANE] = acc.astype(BF16)


def _uv_sample(o_lat, wuv_pad, n_heads, lora):
    ns = o_lat.shape[0]
    nv = n_heads * LANE // 2
    return pl.pallas_call(
        functools.partial(_uv_sample_kernel, n_heads=n_heads, lora=lora),
        grid=(1,),
        in_specs=[pl.BlockSpec(o_lat.shape, lambda i: (0, 0)), pl.BlockSpec(wuv_pad.shape, lambda i: (0, 0, 0))],
        out_specs=pl.BlockSpec((ns, nv), lambda i: (0, 0)),
        out_shape=jax.ShapeDtypeStruct((ns, nv), BF16),
        compiler_params=_cp(("arbitrary",), VMEM_LIMIT),
        name="uv_sample",
    )(o_lat, wuv_pad)


def _router_kernel(x_ref, w_ref, b_ref, o_ref, *, n_experts):
    logits = _dot(x_ref[...], w_ref[...]) + b_ref[...]
    lane = lax.broadcasted_iota(jnp.int32, logits.shape, 1).astype(F32)
    logits = jnp.where(lane < n_experts, logits, -jnp.inf)
    m1 = jnp.max(logits, axis=-1, keepdims=True)
    i1 = jnp.min(jnp.where(logits == m1, lane, float(LANE)), axis=-1, keepdims=True)
    rest = jnp.where(lane == i1, -jnp.inf, logits)
    m2 = jnp.max(rest, axis=-1, keepdims=True)
    i2 = jnp.min(jnp.where(rest == m2, lane, float(LANE)), axis=-1, keepdims=True)
    e2 = jnp.exp(m2 - m1)
    den = 1.0 + e2
    g1 = 1.0 / den
    g2 = e2 / den
    o_ref[...] = jnp.where(lane == 0, i1, jnp.where(lane == 1, i2, jnp.where(lane == 2, g1, jnp.where(lane == 3, g2, 0.0))))


def _router(x, w_pad, b_pad, n_experts):
    r, d = x.shape
    return pl.pallas_call(
        functools.partial(_router_kernel, n_experts=n_experts),
        grid=(r // ROW_TILE,),
        in_specs=[pl.BlockSpec((ROW_TILE, d), lambda i: (i, 0)), pl.BlockSpec((d, LANE), lambda i: (0, 0)),
                  pl.BlockSpec((1, LANE), lambda i: (0, 0))],
        out_specs=pl.BlockSpec((ROW_TILE, LANE), lambda i: (i, 0)),
        out_shape=jax.ShapeDtypeStruct((r, LANE), F32),
        compiler_params=_cp(("arbitrary",), VMEM_LIMIT),
        name="router",
    )(x, w_pad, b_pad)


def _moe_ffn_kernel(te_ref, nu_ref, src_ref, x_hbm, wg_ref, wu_ref, w2_ref, o_ref, xbuf, xb_ref, acc_ref, sem,
                    *, tm, rows_per_step):
    t = pl.program_id(0)
    f = pl.program_id(1)
    nf = pl.num_programs(1)
    used = t < nu_ref[0]
    slot = t % 2
    n_rows = rows_per_step * nf

    def row_copy(tile, r, sl):
        row = src_ref[tile * tm + r]
        return pltpu.make_async_copy(x_hbm.at[pl.ds(row, 1), :], xbuf.at[sl, pl.ds(r, 1), :], sem.at[sl])

    @pl.when((t == 0) & (f == 0))
    def _():
        def body(g, carry):
            for u in range(GATHER_UNROLL):
                row_copy(0, g * GATHER_UNROLL + u, 0).start()
            return carry
        lax.fori_loop(0, n_rows // GATHER_UNROLL, body, 0)

    @pl.when(f == 0)
    def _():
        pltpu.make_async_copy(xbuf.at[slot, pl.ds(0, n_rows)], xbuf.at[slot, pl.ds(0, n_rows)], sem.at[slot]).wait()
        xb_ref[...] = xbuf[slot, :tm].astype(BF16)

    def prefetch():
        for u in range(rows_per_step):
            row_copy(t + 1, f * rows_per_step + u, 1 - slot).start()

    @pl.when(used)
    def _():
        prefetch()
        x = xb_ref[...]
        hcol = (_silu(_dot(x, wg_ref[0])) * _dot(x, wu_ref[0])).astype(BF16)
        y = _dot(hcol, w2_ref[0])

        @pl.when(f == 0)
        def _():
            acc_ref[...] = y

        @pl.when(f > 0)
        def _():
            acc_ref[...] += y

    @pl.when(jnp.logical_not(used))
    def _():
        prefetch()

    @pl.when(f == nf - 1)
    def _():
        o_ref[...] = jnp.where(used, acc_ref[...], 0.0)

    @pl.when((t == pl.num_programs(0) - 1) & (f == nf - 1))
    def _():
        pltpu.make_async_copy(xbuf.at[1 - slot, pl.ds(0, n_rows)], xbuf.at[1 - slot, pl.ds(0, n_rows)],
                              sem.at[1 - slot]).wait()


def _moe_ffn(tile_expert, n_used, src, x, w13, w2):
    d = x.shape[1]
    n_exp, eff = w2.shape[0], w2.shape[1]
    tm, tf = MOE_ROW_TILE, min(MOE_FF_CHUNK, eff)
    n = src.shape[0] - 2 * tm
    assert n % tm == 0 and eff % tf == 0
    nf = eff // tf
    rows_per_step = -(-tm // (nf * GATHER_UNROLL)) * GATHER_UNROLL
    assert rows_per_step * nf <= 2 * tm

    def widx(base):
        def index_map(t, f, te, nu, src):
            live = t < nu[0]
            return (te[t], 0, jnp.where(live, f, nf - 1) + base)
        return index_map

    def w2idx(t, f, te, nu, src):
        return (te[t], jnp.where(t < nu[0], f, nf - 1), 0)

    grid_spec = pltpu.PrefetchScalarGridSpec(
        num_scalar_prefetch=3,
        grid=(n // tm, nf),
        in_specs=[
            pl.BlockSpec(memory_space=pl.ANY),
            pl.BlockSpec((1, d, tf), widx(0)),
            pl.BlockSpec((1, d, tf), widx(nf)),
            pl.BlockSpec((1, tf, d), w2idx),
        ],
        out_specs=pl.BlockSpec((tm, d), lambda t, f, te, nu, src: (t, 0)),
        scratch_shapes=[
            pltpu.VMEM((2, rows_per_step * nf, d), x.dtype),
            pltpu.VMEM((tm, d), BF16),
            pltpu.VMEM((tm, d), F32),
            pltpu.SemaphoreType.DMA((2,)),
        ],
    )
    return pl.pallas_call(
        functools.partial(_moe_ffn_kernel, tm=tm, rows_per_step=rows_per_step),
        grid_spec=grid_spec,
        out_shape=jax.ShapeDtypeStruct((n, d), F32),
        compiler_params=_cp(("arbitrary", "arbitrary"), VMEM_LIMIT),
        name="moe_ffn",
    )(tile_expert, n_used, src, x, w13, w13, w2)


def _combine_ln_kernel(p1_ref, p2_ref, y_hbm, h_ref, route_ref, g_ref, b_ref, o_ref, buf, sem, *, tile, alpha):
    i = pl.program_id(0)
    n = pl.num_programs(0)
    slot = i % 2

    def issue(t, sl):
        def body(g, carry):
            for u in range(GATHER_UNROLL):
                r = g * GATHER_UNROLL + u
                a = p1_ref[t * tile + r]
                b = p2_ref[t * tile + r]
                pltpu.make_async_copy(y_hbm.at[pl.ds(a, 1), :], buf.at[sl, 0, pl.ds(r, 1), :], sem.at[sl]).start()
                pltpu.make_async_copy(y_hbm.at[pl.ds(b, 1), :], buf.at[sl, 1, pl.ds(r, 1), :], sem.at[sl]).start()
            return carry
        lax.fori_loop(0, tile // GATHER_UNROLL, body, 0)

    @pl.when(i == 0)
    def _():
        issue(i, slot)

    @pl.when(i + 1 < n)
    def _():
        issue(i + 1, 1 - slot)

    pltpu.make_async_copy(buf.at[slot], buf.at[slot], sem.at[slot]).wait()
    route = route_ref[...]
    z = alpha * h_ref[...] + (route[:, 2:3] * buf[slot, 0] + route[:, 3:4] * buf[slot, 1])
    o_ref[...] = _layer_norm(z, g_ref[...], b_ref[...])


def _combine_ln(pos1, pos2, y_sorted, h, route, g, b, alpha):
    r, d = h.shape
    tile = GATHER_TILE
    assert r % tile == 0 and tile % GATHER_UNROLL == 0
    grid_spec = pltpu.PrefetchScalarGridSpec(
        num_scalar_prefetch=2,
        grid=(r // tile,),
        in_specs=[
            pl.BlockSpec(memory_space=pl.ANY),
            pl.BlockSpec((tile, d), lambda i, p1, p2: (i, 0)),
            pl.BlockSpec((tile, LANE), lambda i, p1, p2: (i, 0)),
            pl.BlockSpec((1, d), lambda i, p1, p2: (0, 0)),
            pl.BlockSpec((1, d), lambda i, p1, p2: (0, 0)),
        ],
        out_specs=pl.BlockSpec((tile, d), lambda i, p1, p2: (i, 0)),
        scratch_shapes=[pltpu.VMEM((2, 2, tile, d), F32), pltpu.SemaphoreType.DMA((2,))],
    )
    return pl.pallas_call(
        functools.partial(_combine_ln_kernel, tile=tile, alpha=alpha),
        grid_spec=grid_spec,
        out_shape=jax.ShapeDtypeStruct((r, d), F32),
        compiler_params=_cp(("arbitrary",), VMEM_LIMIT),
        name="combine_ln",
    )(pos1, pos2, y_sorted, h, route, g, b)


def _route_plan(route, n_experts, tm):
    r = route.shape[0]
    e = jnp.concatenate([route[:, 0], route[:, 1]]).astype(jnp.int32)
    onehot = (e[:, None] == jnp.arange(n_experts, dtype=jnp.int32)[None, :]).astype(jnp.int32)
    csum = jnp.cumsum(onehot, axis=0)
    rank = jnp.take_along_axis(csum, e[:, None], axis=1)[:, 0] - 1
    tiles = (csum[-1] + tm - 1) // tm
    tile_end = jnp.cumsum(tiles)
    pos = (tile_end - tiles)[e] * tm + rank
    n_tiles = (TOP_K * r) // tm + n_experts
    tok = jnp.arange(TOP_K * r, dtype=jnp.int32) % r
    src = jnp.zeros(((n_tiles + 2) * tm,), jnp.int32).at[pos].set(tok, unique_indices=True)
    n_used = tile_end[-1]
    tile_id = jnp.minimum(jnp.arange(n_tiles, dtype=jnp.int32), n_used - 1)
    tile_expert = jnp.sum((tile_end[None, :] <= tile_id[:, None]).astype(jnp.int32), axis=1)
    return src, tile_expert, n_used.reshape(1).astype(jnp.int32), pos[:r], pos[r:]


def _pad_cols(w, offset, width):
    return jnp.pad(w, ((0, 0), (offset, width - offset - w.shape[1])))


def kernel(x_prompt, x_sample, state_ret, cache_ckv, cache_kpe, page_table, meta_tokens, ln_g, ln_b, ret_w_in, ret_gn_g, ret_w_o, mla_w_dq, mla_q_norm, mla_w_uq, mla_w_o, kv_w_a, kv_norm, kv_w_b, ffn_w13, ffn_w2, moe_w_r, moe_b_r, moe_w13, moe_w2):
    bsz, seq, d = x_prompt.shape
    dec_b, dec_seq, _ = x_sample.shape
    n_ret, _, ret_h, dk, dv = state_ret.shape
    depth = ln_g.shape[0]
    assert depth == 2 and n_ret == 1 and mla_w_dq.shape[0] == 1, "layer pattern: one retention layer then one MLA layer"
    past_len = page_table.shape[1] * PAGE_SIZE
    lora, mla_h, nope_v = kv_w_b.shape
    rope = kv_w_a.shape[1] - lora
    nope = mla_w_uq.shape[2] // mla_h - rope
    v_dim = nope_v - nope
    n_experts = moe_w_r.shape[2]
    assert nope == LANE // 2 and v_dim == LANE // 2 and rope <= LANE // 4
    alpha = (2 * depth) ** 0.25
    mla_scale = (nope + rope) ** -0.5
    half = rope // 2

    n_p, n_s = bsz * seq, dec_b * dec_seq
    meta_row0 = n_p + n_s
    r_real = meta_row0 + N_META
    r_pad = -(-r_real // ROW_TILE) * ROW_TILE
    assert r_pad - meta_row0 >= LANE

    x0 = jnp.concatenate([x_prompt.reshape(n_p, d), x_sample.reshape(n_s, d), meta_tokens,
                          jnp.zeros((r_pad - r_real, d), F32)], axis=0)
    pos = jnp.concatenate([jnp.tile(N_META + jnp.arange(seq), bsz), jnp.tile(past_len + jnp.arange(dec_seq), dec_b),
                           jnp.arange(N_META), jnp.zeros((r_pad - r_real,), jnp.int32)]).astype(F32)

    inv_r = ROPE_BASE ** (-jnp.arange(dk // 2, dtype=F32) / (dk // 2))
    ang_r = pos[:, None] * inv_r[None, :]
    cos_r, sin_r = jnp.cos(ang_r), jnp.sin(ang_r)
    inv_m = ROPE_BASE ** (-jnp.arange(half, dtype=F32) / half)
    ang_m = pos[:, None] * inv_m[None, :]
    cos_m = jnp.concatenate([jnp.cos(ang_m)] * 2, axis=1)
    sin_m = jnp.concatenate([jnp.sin(ang_m)] * 2, axis=1)
    cos_a, sin_a = _pad_cols(cos_m, 0, LANE), _pad_cols(sin_m, 0, LANE)
    cos_b, sin_b = _pad_cols(cos_m, nope, LANE), _pad_cols(sin_m, nope, LANE)
    q_scale = mla_scale * LOG2E
    q_ctab = (_pad_cols(jnp.ones((r_pad, nope), F32), 0, LANE) + cos_b) * q_scale
    q_stab = sin_b * q_scale

    def rot(w):
        return jnp.concatenate([-w[..., half:], w[..., :half]], axis=-1)

    qkvg = _ret_proj(x0, ret_w_in[0], cos_r, sin_r, ret_h, dk)
    gn = ret_gn_g[0].reshape(1, -1)
    mix_p, mix_meta, st_p = _ret_prompt(qkvg, gn, _decay_tables(ret_h, dk, RET_CHUNK, RET_CHUNK),
                                        bsz, seq, meta_row0, ret_h, dk, dv)
    mix_s, st_s = _ret_sample(qkvg, state_ret[0], gn,
                              _decay_tables(ret_h, dk, SAMPLE_BATCH_TILE * dec_seq, dec_seq),
                              n_p, dec_b, dec_seq, ret_h, dk, dv)
    mix = jnp.concatenate([mix_p, mix_s, mix_meta, jnp.zeros((r_pad - r_real, mix_p.shape[1]), BF16)], axis=0)
    row = lambda v: v.reshape(1, -1)
    h1, h1b = _proj_ln(mix, ret_w_o[0], x0, row(ln_g[0, 0]), row(ln_b[0, 0]), alpha)
    h2, h2b = _ffn_ln(h1b, ffn_w13[0].astype(BF16), ffn_w2[0].astype(BF16), h1, row(ln_g[0, 1]), row(ln_b[0, 1]), alpha)

    wc, wpe = kv_w_a[:, :lora], kv_w_a[:, lora:]
    wkv = jnp.concatenate([wc, _pad_cols(wpe, 0, LANE), _pad_cols(rot(wpe), 0, LANE),
                           _pad_cols(wpe, nope, LANE), _pad_cols(rot(wpe), nope, LANE)], axis=1).astype(BF16)
    wuk = jnp.pad(kv_w_b[:, :, :nope], ((0, 0), (0, 0), (0, LANE - nope))).reshape(lora, mla_h * LANE).astype(BF16)
    even = (jnp.arange(mla_h) % 2 == 0)[None, :, None]
    wuv_h = kv_w_b[:, :, nope:]
    wuv = jnp.where(even, jnp.pad(wuv_h, ((0, 0), (0, 0), (0, LANE - v_dim))),
                    jnp.pad(wuv_h, ((0, 0), (0, 0), (LANE - v_dim, 0)))).reshape(lora, mla_h * LANE).astype(BF16)
    lane_id = jnp.arange(LANE)[None, None, :]
    ones_row = jnp.where(even, lane_id == v_dim, lane_id == 0).astype(F32).reshape(1, mla_h * LANE)
    ckv, kpe, k_heads, v_heads = _kv_proj(h2b, wkv, row(kv_norm), (cos_a, sin_a, cos_b, sin_b), wuk, wuv, ones_row,
                                          lora, rope, mla_h)

    wuq = mla_w_uq[0].reshape(-1, mla_h, nope + rope)
    wq_n, wq_r = wuq[:, :, :nope], wuq[:, :, nope:]
    pad_h = lambda w, off: jnp.pad(w, ((0, 0), (0, 0), (off, LANE - off - w.shape[2]))).reshape(w.shape[0], mla_h * LANE)
    w_q1 = (pad_h(wq_n, 0) + pad_h(wq_r, nope)).astype(BF16)
    w_q2 = pad_h(rot(wq_r), nope).astype(BF16)
    wdq = mla_w_dq[0].astype(BF16)
    qn_g = row(mla_q_norm[0])
    q_heads = _q_proj(h2b, wdq, qn_g, w_q1, w_q2, q_ctab, q_stab, mla_h)
    attn_p = _attn_prompt(q_heads, k_heads, v_heads, bsz, seq, meta_row0, mla_h)

    wukt = jnp.pad(jnp.transpose(kv_w_b[:, :, :nope], (1, 2, 0)), ((0, 0), (0, LANE - nope), (0, 0))).astype(BF16)
    w_pe = wq_r.reshape(-1, mla_h * rope).astype(BF16)
    w_pe_rot = rot(wq_r).reshape(-1, mla_h * rope).astype(BF16)
    cs = jnp.tile(cos_m[n_p:n_p + n_s], (1, mla_h))
    ss = jnp.tile(sin_m[n_p:n_p + n_s], (1, mla_h))
    qabs, qpe = _q_sample(h2b[n_p:n_p + n_s], wdq, qn_g, pad_h(wq_n, 0).astype(BF16), wukt, w_pe, w_pe_rot, cs, ss,
                          mla_h, lora, q_scale)
    ckv_s, kpe_s = ckv[n_p:n_p + n_s], kpe[n_p:n_p + n_s]
    new_pad = lambda a: jnp.pad(a.reshape(dec_b, dec_seq, -1), ((0, 0), (0, LANE - dec_seq), (0, 0))).astype(BF16)
    o_lat = _attn_paged(page_table, qabs.reshape(dec_b, dec_seq * mla_h, lora), qpe.reshape(dec_b, dec_seq * mla_h, rope),
                        new_pad(ckv_s), new_pad(kpe_s), cache_ckv, jnp.swapaxes(cache_kpe, 1, 2), mla_h, dec_seq)
    wuv_t = jnp.transpose(wuv_h, (1, 0, 2))
    wuv_pad = jnp.where(jnp.transpose(even, (1, 0, 2)), jnp.pad(wuv_t, ((0, 0), (0, 0), (0, LANE - v_dim))),
                        jnp.pad(wuv_t, ((0, 0), (0, 0), (LANE - v_dim, 0)))).astype(BF16)
    attn_s = _uv_sample(o_lat.reshape(n_s, mla_h * lora), wuv_pad, mla_h, lora)
    attn = jnp.concatenate([attn_p, attn_s, jnp.zeros((r_pad - meta_row0, attn_p.shape[1]), BF16)], axis=0)
    h3, h3b = _proj_ln(attn, mla_w_o[0], h2, row(ln_g[1, 0]), row(ln_b[1, 0]), alpha)

    w_r = _pad_cols(moe_w_r[0], 0, LANE).astype(BF16)
    b_r = _pad_cols(moe_b_r[0].reshape(1, -1), 0, LANE)
    route = _router(h3b, w_r, b_r, n_experts)
    src, tile_expert, n_used, pos1, pos2 = _route_plan(route, n_experts, MOE_ROW_TILE)
    y_sorted = _moe_ffn(tile_expert, n_used, src, h3, moe_w13[0].astype(BF16), moe_w2[0].astype(BF16))
    h4 = _combine_ln(pos1, pos2, y_sorted, h3, route, row(ln_g[1, 1]), row(ln_b[1, 1]), alpha)

    y_prompt = h4[:n_p].reshape(bsz, seq, d)
    y_sample = h4[n_p:n_p + n_s].reshape(dec_b, dec_seq, d)

    def with_meta(a):
        meta = jnp.broadcast_to(a[meta_row0:r_real][None], (bsz, N_META, a.shape[1]))
        return jnp.concatenate([meta, a[:n_p].reshape(bsz, seq, -1)], axis=1)

    return (y_prompt, y_sample, st_p[None], with_meta(ckv), with_meta(kpe), st_s[None],
            ckv_s.reshape(dec_b, dec_seq, lora), kpe_s.reshape(dec_b, dec_seq, rope))
```

```python
import functools
import math

import jax
import jax.numpy as jnp
from jax import lax
from jax.experimental import pallas as pl
from jax.experimental.pallas import tpu as pltpu

F32 = jnp.float32
BF16 = jnp.bfloat16

N_META = 16
TOP_K = 2
ROPE_BASE = 10000.0
LN_EPS = 1e-5
RMS_EPS = 1e-6
PAGE_SIZE = 128
RET_CHUNK = 128

LANE = 128
ROW_TILE = 512
RET_PROJ_TILE = 1024
ATTN_TQ = 1024
ATTN_TK = 512
FFN_CHUNK = 256
MOE_ROW_TILE = 512
MOE_FF_CHUNK = 512
GATHER_TILE = 256
GATHER_UNROLL = 8
PAGES_PER_CHUNK = 32
PAGED_SLOTS = 4
PAGED_KEY_BLOCK = 1024
PAGED_SCORES_AHEAD = 8
LOG2E = math.log2(math.e)
SAMPLE_BATCH_TILE = 4
VMEM_LIMIT = 56 * 1024 * 1024


def _cp(sem, vmem=None):
    return pltpu.CompilerParams(dimension_semantics=sem, vmem_limit_bytes=vmem)


def _layer_norm(z, g, b):
    mu = jnp.mean(z, axis=-1, keepdims=True)
    zc = z - mu
    var = jnp.mean(zc * zc, axis=-1, keepdims=True)
    return zc * lax.rsqrt(var + LN_EPS) * g + b


def _rms_norm(c, g):
    return c * lax.rsqrt(jnp.mean(c * c, axis=-1, keepdims=True) + RMS_EPS) * g


def _silu(x):
    return x * (1.0 / (1.0 + jnp.exp(-x)))


def _dot(a, b):
    return jnp.dot(a, b, preferred_element_type=F32)


def _dot_nt(a, b):
    return lax.dot_general(a, b, (((1,), (1,)), ((), ())), preferred_element_type=F32)


def _dot_tn(a, b):
    return lax.dot_general(a, b, (((0,), (0,)), ((), ())), preferred_element_type=F32)


def _ret_proj_kernel(x_ref, w_ref, cos_ref, sin_ref, o_ref, wbf_ref, *, n_heads, dk, k_scale):
    j = pl.program_id(0)
    i = pl.program_id(1)

    @pl.when(i == 0)
    def _():
        wbf_ref[...] = w_ref[...].astype(BF16)

    y = _dot(x_ref[...].astype(BF16), wbf_ref[...])
    half = dk // 2

    @pl.when(j < 2)
    def _():
        c = cos_ref[...]
        s = sin_ref[...]
        scale = jnp.where(j == 1, k_scale, 1.0).astype(F32)
        for h in range(n_heads):
            x1 = y[:, h * dk:h * dk + half]
            x2 = y[:, h * dk + half:(h + 1) * dk]
            o_ref[:, h * dk:h * dk + half] = ((x1 * c - x2 * s) * scale).astype(BF16)
            o_ref[:, h * dk + half:(h + 1) * dk] = ((x1 * s + x2 * c) * scale).astype(BF16)

    @pl.when(j >= 2)
    def _():
        o_ref[...] = y.astype(BF16)


def _ret_proj(x, w_in, cos_r, sin_r, n_heads, dk):
    r, d = x.shape
    n = w_in.shape[1]
    qk = n_heads * dk
    assert n % qk == 0 and dk // 2 == LANE
    kern = functools.partial(_ret_proj_kernel, n_heads=n_heads, dk=dk, k_scale=dk ** -0.5)
    tile = RET_PROJ_TILE if r % RET_PROJ_TILE == 0 else ROW_TILE
    return pl.pallas_call(
        kern,
        grid=(n // qk, r // tile),
        in_specs=[
            pl.BlockSpec((tile, d), lambda j, i: (i, 0)),
            pl.BlockSpec((d, qk), lambda j, i: (0, j)),
            pl.BlockSpec((tile, LANE), lambda j, i: (i, 0)),
            pl.BlockSpec((tile, LANE), lambda j, i: (i, 0)),
        ],
        out_specs=pl.BlockSpec((tile, qk), lambda j, i: (i, j)),
        out_shape=jax.ShapeDtypeStruct((r, n), BF16),
        scratch_shapes=[pltpu.VMEM((d, qk), BF16)],
        compiler_params=_cp(("arbitrary", "arbitrary"), VMEM_LIMIT),
        name="ret_proj",
    )(x, w_in, cos_r, sin_r)


def _group_norm_gate(o, g, gn):
    mu = jnp.mean(o, axis=-1, keepdims=True)
    oc = o - mu
    var = jnp.mean(oc * oc, axis=-1, keepdims=True)
    return (_silu(g.astype(F32)) * (oc * lax.rsqrt(var + LN_EPS) * gn)).astype(BF16)


def _ret_prompt_kernel(q_ref, k_ref, v_ref, g_ref, qm_ref, km_ref, vm_ref, gm_ref, gn_ref,
                       dec_ref, rdec_ref, cdec_ref, cpow_ref,
                       o_ref, om_ref, st_ref, s_ref, *, n_heads, dk, dv, chunk, n_meta, tile):
    t = pl.program_id(1)
    nt = pl.num_programs(1)

    @pl.when(t == 0)
    def _():
        for h in range(n_heads):
            q = qm_ref[:, h * dk:(h + 1) * dk]
            k = km_ref[:, h * dk:(h + 1) * dk]
            v = vm_ref[:, h * dv:(h + 1) * dv]
            sc = _dot_nt(q, k) * dec_ref[h, :n_meta, :n_meta]
            o = _dot(sc.astype(BF16), v)
            om_ref[:, h * dv:(h + 1) * dv] = _group_norm_gate(
                o, gm_ref[:, h * dv:(h + 1) * dv], gn_ref[:, h * dv:(h + 1) * dv])
            k_out = (k.astype(F32) * cdec_ref[h, chunk - n_meta:, :]).astype(BF16)
            s_ref[h] = _dot_tn(k_out, v)

    for h in range(n_heads):
        for c in range(tile // chunk):
            rows = slice(c * chunk, (c + 1) * chunk)
            q = q_ref[rows, h * dk:(h + 1) * dk]
            k = k_ref[rows, h * dk:(h + 1) * dk]
            v = v_ref[rows, h * dv:(h + 1) * dv]
            s_old = s_ref[h]
            sc = _dot_nt(q, k) * dec_ref[h]
            q_in = (q.astype(F32) * rdec_ref[h]).astype(BF16)
            o = _dot(sc.astype(BF16), v) + _dot(q_in, s_old.astype(BF16))
            k_out = (k.astype(F32) * cdec_ref[h]).astype(BF16)
            s_ref[h] = cpow_ref[h] * s_old + _dot_tn(k_out, v)
            o_ref[rows, h * dv:(h + 1) * dv] = _group_norm_gate(
                o, g_ref[rows, h * dv:(h + 1) * dv], gn_ref[:, h * dv:(h + 1) * dv])

    @pl.when(t == nt - 1)
    def _():
        st_ref[0] = s_ref[...]


def _ret_prompt(qkvg, gn_g, tabs, bsz, seq, meta_row0, n_heads, dk, dv):
    qk, vw = n_heads * dk, n_heads * dv
    tile = ROW_TILE
    assert seq % tile == 0 and tile % RET_CHUNK == 0 and meta_row0 % N_META == 0 and vw == 2 * qk
    nt = seq // tile
    mb = meta_row0 // N_META
    dec, rdec, cdec, cpow = tabs
    kern = functools.partial(_ret_prompt_kernel, n_heads=n_heads, dk=dk, dv=dv, chunk=RET_CHUNK,
                             n_meta=N_META, tile=tile)
    full = lambda *shape: pl.BlockSpec(shape, lambda b, t: (0,) * len(shape))
    return pl.pallas_call(
        kern,
        grid=(bsz, nt),
        in_specs=[
            pl.BlockSpec((tile, qk), lambda b, t: (b * nt + t, 0)),
            pl.BlockSpec((tile, qk), lambda b, t: (b * nt + t, 1)),
            pl.BlockSpec((tile, vw), lambda b, t: (b * nt + t, 1)),
            pl.BlockSpec((tile, vw), lambda b, t: (b * nt + t, 2)),
            pl.BlockSpec((N_META, qk), lambda b, t: (mb, 0)),
            pl.BlockSpec((N_META, qk), lambda b, t: (mb, 1)),
            pl.BlockSpec((N_META, vw), lambda b, t: (mb, 1)),
            pl.BlockSpec((N_META, vw), lambda b, t: (mb, 2)),
            full(1, vw),
            full(n_heads, RET_CHUNK, RET_CHUNK),
            full(n_heads, RET_CHUNK, dk),
            full(n_heads, RET_CHUNK, dk),
            full(n_heads, 1, 1),
        ],
        out_specs=[
            pl.BlockSpec((tile, vw), lambda b, t: (b * nt + t, 0)),
            pl.BlockSpec((N_META, vw), lambda b, t: (0, 0)),
            pl.BlockSpec((1, n_heads, dk, dv), lambda b, t: (b, 0, 0, 0)),
        ],
        out_shape=[
            jax.ShapeDtypeStruct((bsz * seq, vw), BF16),
            jax.ShapeDtypeStruct((N_META, vw), BF16),
            jax.ShapeDtypeStruct((bsz, n_heads, dk, dv), F32),
        ],
        scratch_shapes=[pltpu.VMEM((n_heads, dk, dv), F32)],
        compiler_params=_cp(("arbitrary", "arbitrary"), VMEM_LIMIT),
        name="ret_prompt",
    )(qkvg, qkvg, qkvg, qkvg, qkvg, qkvg, qkvg, qkvg, gn_g, dec, rdec, cdec, cpow)


def _ret_sample_kernel(q_ref, k_ref, v_ref, g_ref, s0_ref, gn_ref, dec_ref, rdec_ref, cdec_ref, cpow_ref,
                       o_ref, st_ref, *, n_heads, dk, dv, nb, dec_seq):
    rows = nb * dec_seq
    row_b = lax.broadcasted_iota(jnp.int32, (rows, 1), 0) // dec_seq
    for h in range(n_heads):
        q = q_ref[:, h * dk:(h + 1) * dk]
        k = k_ref[:, h * dk:(h + 1) * dk]
        v = v_ref[:, h * dv:(h + 1) * dv]
        sc = _dot_nt(q, k) * dec_ref[h]
        o = _dot(sc.astype(BF16), v)
        q_in = q.astype(F32) * rdec_ref[h]
        k_out = k.astype(F32) * cdec_ref[h]
        for b in range(nb):
            s_old = s0_ref[0, b, h]
            sel = row_b == b
            o = o + _dot(jnp.where(sel, q_in, 0.0).astype(BF16), s_old.astype(BF16))
            st_ref[0, b, h] = cpow_ref[h] * s_old + _dot_tn(jnp.where(sel, k_out, 0.0).astype(BF16), v)
        o_ref[:, h * dv:(h + 1) * dv] = _group_norm_gate(
            o, g_ref[:, h * dv:(h + 1) * dv], gn_ref[:, h * dv:(h + 1) * dv])


def _ret_sample(qkvg, state, gn_g, tabs, row0, dec_b, dec_seq, n_heads, dk, dv):
    qk, vw = n_heads * dk, n_heads * dv
    nb = SAMPLE_BATCH_TILE
    rows = nb * dec_seq
    assert dec_b % nb == 0 and row0 % rows == 0 and rows % 16 == 0
    rb = row0 // rows
    dec, rdec, cdec, cpow = tabs
    state5 = state.reshape(dec_b // nb, nb, n_heads, dk, dv)
    kern = functools.partial(_ret_sample_kernel, n_heads=n_heads, dk=dk, dv=dv, nb=nb, dec_seq=dec_seq)
    full = lambda *shape: pl.BlockSpec(shape, lambda i: (0,) * len(shape))
    out, st = pl.pallas_call(
        kern,
        grid=(dec_b // nb,),
        in_specs=[
            pl.BlockSpec((rows, qk), lambda i: (rb + i, 0)),
            pl.BlockSpec((rows, qk), lambda i: (rb + i, 1)),
            pl.BlockSpec((rows, vw), lambda i: (rb + i, 1)),
            pl.BlockSpec((rows, vw), lambda i: (rb + i, 2)),
            pl.BlockSpec((1, nb, n_heads, dk, dv), lambda i: (i, 0, 0, 0, 0)),
            full(1, vw),
            full(n_heads, rows, rows),
            full(n_heads, rows, dk),
            full(n_heads, rows, dk),
            full(n_heads, 1, 1),
        ],
        out_specs=[
            pl.BlockSpec((rows, vw), lambda i: (i, 0)),
            pl.BlockSpec((1, nb, n_heads, dk, dv), lambda i: (i, 0, 0, 0, 0)),
        ],
        out_shape=[
            jax.ShapeDtypeStruct((dec_b * dec_seq, vw), BF16),
            jax.ShapeDtypeStruct(state5.shape, F32),
        ],
        compiler_params=_cp(("arbitrary",), VMEM_LIMIT),
        name="ret_sample",
    )(qkvg, qkvg, qkvg, qkvg, state5, gn_g, dec, rdec, cdec, cpow)
    return out, st.reshape(dec_b, n_heads, dk, dv)


def _decay_tables(n_heads, dk, chunk, group):
    lg = jnp.log1p(-jnp.exp2(-5.0 - jnp.arange(n_heads, dtype=F32)))
    r = jnp.arange(chunk)
    t = (r % group).astype(F32)
    diff = t[:, None] - t[None, :]
    same = (r[:, None] // group) == (r[None, :] // group)
    dec = jnp.where(same & (diff >= 0), jnp.exp(jnp.maximum(diff, 0.0)[None] * lg[:, None, None]), 0.0)
    rdec = jnp.exp((t[None, :] + 1.0) * lg[:, None])
    cdec = jnp.exp((group - 1.0 - t)[None, :] * lg[:, None])
    cpow = jnp.exp(group * lg)
    bc = lambda a: jnp.broadcast_to(a[:, :, None], (n_heads, chunk, dk))
    return dec, bc(rdec), bc(cdec), cpow.reshape(n_heads, 1, 1)


def _proj_ln_kernel(a_ref, w_ref, h_ref, g_ref, b_ref, o_ref, obf_ref, wbf_ref, *, alpha):
    @pl.when(pl.program_id(0) == 0)
    def _():
        wbf_ref[...] = w_ref[...].astype(BF16)

    z = alpha * h_ref[...] + _dot(a_ref[...], wbf_ref[...])
    y = _layer_norm(z, g_ref[...], b_ref[...])
    o_ref[...] = y
    obf_ref[...] = y.astype(BF16)


def _proj_ln(a, w, h, g, b, alpha):
    r, k = a.shape
    d = w.shape[1]
    row = lambda n: pl.BlockSpec((ROW_TILE, n), lambda i: (i, 0))
    full = lambda *shape: pl.BlockSpec(shape, lambda i: (0,) * len(shape))
    return pl.pallas_call(
        functools.partial(_proj_ln_kernel, alpha=alpha),
        grid=(r // ROW_TILE,),
        in_specs=[row(k), full(k, d), row(d), full(1, d), full(1, d)],
        out_specs=[row(d), row(d)],
        out_shape=[jax.ShapeDtypeStruct((r, d), F32), jax.ShapeDtypeStruct((r, d), BF16)],
        scratch_shapes=[pltpu.VMEM((k, d), BF16)],
        compiler_params=_cp(("arbitrary",), VMEM_LIMIT),
        name="proj_ln",
    )(a, w, h, g, b)


def _ffn_ln_kernel(x_ref, w13_ref, w2_ref, h_ref, g_ref, b_ref, o_ref, obf_ref, *, alpha, d_ff, chunk):
    x = x_ref[...]
    acc = alpha * h_ref[...]
    for c in range(d_ff // chunk):
        gate = _dot(x, w13_ref[:, c * chunk:(c + 1) * chunk])
        up = _dot(x, w13_ref[:, d_ff + c * chunk:d_ff + (c + 1) * chunk])
        acc = acc + _dot((_silu(gate) * up).astype(BF16), w2_ref[c * chunk:(c + 1) * chunk, :])
    y = _layer_norm(acc, g_ref[...], b_ref[...])
    o_ref[...] = y
    obf_ref[...] = y.astype(BF16)


def _ffn_ln(x, w13, w2, h, g, b, alpha):
    r, d = x.shape
    d_ff = w2.shape[0]
    assert d_ff % FFN_CHUNK == 0
    row = lambda n: pl.BlockSpec((ROW_TILE, n), lambda i: (i, 0))
    full = lambda *shape: pl.BlockSpec(shape, lambda i: (0,) * len(shape))
    return pl.pallas_call(
        functools.partial(_ffn_ln_kernel, alpha=alpha, d_ff=d_ff, chunk=FFN_CHUNK),
        grid=(r // ROW_TILE,),
        in_specs=[row(d), full(d, 2 * d_ff), full(d_ff, d), row(d), full(1, d), full(1, d)],
        out_specs=[row(d), row(d)],
        out_shape=[jax.ShapeDtypeStruct((r, d), F32), jax.ShapeDtypeStruct((r, d), BF16)],
        compiler_params=_cp(("arbitrary",), VMEM_LIMIT),
        name="ffn_ln",
    )(x, w13, w2, h, g, b)


def _kv_kernel(x_ref, wkv_ref, g_ref, ca_ref, sa_ref, cb_ref, sb_ref, wuk_ref, wuv_ref, ones_ref,
               ckv_ref, kpe_ref, k_ref, v_ref, *, lora, rope, n_heads):
    y = _dot(x_ref[...], wkv_ref[...])
    cn = _rms_norm(y[:, :lora], g_ref[...])
    ckv_ref[...] = cn
    kpe_a = y[:, lora:lora + LANE] * ca_ref[...] + y[:, lora + LANE:lora + 2 * LANE] * sa_ref[...]
    kpe_ref[...] = kpe_a[:, :rope]
    kpe_b = y[:, lora + 2 * LANE:lora + 3 * LANE] * cb_ref[...] + y[:, lora + 3 * LANE:lora + 4 * LANE] * sb_ref[...]
    cb = cn.astype(BF16)
    kn = _dot(cb, wuk_ref[...])
    for h in range(n_heads):
        k_ref[:, h * LANE:(h + 1) * LANE] = (kn[:, h * LANE:(h + 1) * LANE] + kpe_b).astype(BF16)
    v_ref[...] = (_dot(cb, wuv_ref[...]) + ones_ref[...]).astype(BF16)


def _kv_proj(x, wkv, g, tabs, wuk, wuv, ones_row, lora, rope, n_heads):
    r, d = x.shape
    nv = wuv.shape[1]
    row = lambda n: pl.BlockSpec((ROW_TILE, n), lambda i: (i, 0))
    full = lambda *shape: pl.BlockSpec(shape, lambda i: (0,) * len(shape))
    return pl.pallas_call(
        functools.partial(_kv_kernel, lora=lora, rope=rope, n_heads=n_heads),
        grid=(r // ROW_TILE,),
        in_specs=[row(d), full(*wkv.shape), full(1, lora), row(LANE), row(LANE), row(LANE), row(LANE),
                  full(*wuk.shape), full(*wuv.shape), full(1, nv)],
        out_specs=[row(lora), row(rope), row(n_heads * LANE), row(nv)],
        out_shape=[jax.ShapeDtypeStruct((r, lora), F32), jax.ShapeDtypeStruct((r, rope), F32),
                   jax.ShapeDtypeStruct((r, n_heads * LANE), BF16), jax.ShapeDtypeStruct((r, nv), BF16)],
        compiler_params=_cp(("arbitrary",), VMEM_LIMIT),
        name="kv_proj",
    )(x, wkv, g, *tabs, wuk, wuv, ones_row)


def _q_kernel(x_ref, wdq_ref, g_ref, w1_ref, w2_ref, c_ref, s_ref, q_ref, *, n_heads):
    qc = _rms_norm(_dot(x_ref[...], wdq_ref[...]), g_ref[...]).astype(BF16)
    a = _dot(qc, w1_ref[...])
    b = _dot(qc, w2_ref[...])
    c = c_ref[...]
    s = s_ref[...]
    for h in range(n_heads):
        cols = slice(h * LANE, (h + 1) * LANE)
        q_ref[:, cols] = (a[:, cols] * c + b[:, cols] * s).astype(BF16)


def _q_proj(x, wdq, g, w1, w2, c_tab, s_tab, n_heads):
    r, d = x.shape
    ql = wdq.shape[1]
    row = lambda n: pl.BlockSpec((ROW_TILE, n), lambda i: (i, 0))
    full = lambda *shape: pl.BlockSpec(shape, lambda i: (0,) * len(shape))
    return pl.pallas_call(
        functools.partial(_q_kernel, n_heads=n_heads),
        grid=(r // ROW_TILE,),
        in_specs=[row(d), full(d, ql), full(1, ql), full(*w1.shape), full(*w2.shape), row(LANE), row(LANE)],
        out_specs=row(n_heads * LANE),
        out_shape=jax.ShapeDtypeStruct((r, n_heads * LANE), BF16),
        compiler_params=_cp(("arbitrary",), VMEM_LIMIT),
        name="q_proj",
    )(x, wdq, g, w1, w2, c_tab, s_tab)


def _q_sample_kernel(x_ref, wdq_ref, g_ref, w1_ref, wukt_ref, wpe_ref, wrot_ref, c_ref, s_ref,
                     qabs_ref, qpe_ref, *, n_heads, lora, scale):
    qc = _rms_norm(_dot(x_ref[...], wdq_ref[...]), g_ref[...]).astype(BF16)
    qn = (_dot(qc, w1_ref[...]) * scale).astype(BF16)
    for h in range(n_heads):
        qabs_ref[:, h * lora:(h + 1) * lora] = _dot(qn[:, h * LANE:(h + 1) * LANE], wukt_ref[h]).astype(BF16)
    qpe = _dot(qc, wpe_ref[...]) * c_ref[...] + _dot(qc, wrot_ref[...]) * s_ref[...]
    qpe_ref[...] = (qpe * scale).astype(BF16)


def _q_sample(x, wdq, g, w1, wukt, wpe, wrot, c_tab, s_tab, n_heads, lora, scale):
    ns = x.shape[0]
    args = (x, wdq, g, w1, wukt, wpe, wrot, c_tab, s_tab)
    full = lambda a: pl.BlockSpec(a.shape, lambda i, nd=a.ndim: (0,) * nd)
    return pl.pallas_call(
        functools.partial(_q_sample_kernel, n_heads=n_heads, lora=lora, scale=scale),
        grid=(1,),
        in_specs=[full(a) for a in args],
        out_specs=[pl.BlockSpec((ns, n_heads * lora), lambda i: (0, 0)),
                   pl.BlockSpec((ns, wpe.shape[1]), lambda i: (0, 0))],
        out_shape=[jax.ShapeDtypeStruct((ns, n_heads * lora), BF16),
                   jax.ShapeDtypeStruct((ns, wpe.shape[1]), BF16)],
        compiler_params=_cp(("arbitrary",), VMEM_LIMIT),
        name="q_sample",
    )(*args)


def _attn_step(m, acc, s, v):
    m_new = jnp.maximum(m, jnp.max(s, axis=-1, keepdims=True))
    p = jnp.exp2((s - m_new).astype(BF16))
    return m_new, jnp.exp2(m - m_new) * acc + _dot(p, v)


def _attn_prompt_kernel(q_ref, k_ref, v_ref, km_ref, vm_ref, o_ref, *, tq, tk, n_meta):
    i = pl.program_id(2)
    lane = lax.broadcasted_iota(jnp.int32, (tq, LANE), 1)
    heads = (slice(0, LANE), slice(LANE, 2 * LANE))
    qs = [q_ref[:, h] for h in heads]

    carry = []
    for q, h in zip(qs, heads):
        s = jnp.where(lane < n_meta, _dot_nt(q, km_ref[:, h]), -jnp.inf)
        m = jnp.max(s, axis=-1, keepdims=True)
        carry += [m, _dot(jnp.exp2((s - m).astype(BF16)), vm_ref[:, h])]

    def tile(j, carry, diag):
        off = pl.multiple_of(j * tk, tk)
        lo = 0 if diag is None else diag * tk
        out = []
        for e, (q, h) in enumerate(zip(qs, heads)):
            s = _dot_nt(q[lo:], k_ref[pl.ds(off, tk), h])
            if diag is not None:
                row = lax.broadcasted_iota(jnp.int32, (tq - lo, tk), 0)
                col = lax.broadcasted_iota(jnp.int32, (tq - lo, tk), 1)
                s = jnp.where(col <= row, s, -jnp.inf)
            m, acc = carry[2 * e], carry[2 * e + 1]
            m_new, acc_new = _attn_step(m[lo:], acc[lo:], s, v_ref[pl.ds(off, tk), h])
            if lo:
                m_new = jnp.concatenate([m[:lo], m_new], axis=0)
                acc_new = jnp.concatenate([acc[:lo], acc_new], axis=0)
            out += [m_new, acc_new]
        return out

    per_q = tq // tk
    carry = lax.fori_loop(0, i * per_q, lambda j, c: tuple(tile(j, c, None)), tuple(carry))
    for dj in range(per_q):
        carry = tile(i * per_q + dj, carry, dj)
    _, acc0, _, acc1 = carry
    l0 = jnp.sum(jnp.where(lane == LANE // 2, acc0, 0.0), axis=-1, keepdims=True)
    l1 = jnp.sum(jnp.where(lane == 0, acc1, 0.0), axis=-1, keepdims=True)
    o_ref[...] = jnp.where(lane < LANE // 2, acc0 * (1.0 / l0), acc1 * (1.0 / l1)).astype(BF16)


def _attn_prompt(q, k, v, bsz, seq, meta_row0, n_heads):
    tq, tk = min(ATTN_TQ, seq), ATTN_TK
    assert tq % tk == 0 and seq % tq == 0 and meta_row0 % LANE == 0 and n_heads % 2 == 0
    assert v.shape[1] == n_heads * LANE
    nq = seq // tq
    mb = meta_row0 // LANE
    pair = 2 * LANE
    return pl.pallas_call(
        functools.partial(_attn_prompt_kernel, tq=tq, tk=tk, n_meta=N_META),
        grid=(bsz, n_heads // 2, nq),
        in_specs=[
            pl.BlockSpec((tq, pair), lambda b, hp, i: (b * nq + i, hp)),
            pl.BlockSpec((seq, pair), lambda b, hp, i: (b, hp)),
            pl.BlockSpec((seq, pair), lambda b, hp, i: (b, hp)),
            pl.BlockSpec((LANE, pair), lambda b, hp, i: (mb, hp)),
            pl.BlockSpec((LANE, pair), lambda b, hp, i: (mb, hp)),
        ],
        out_specs=pl.BlockSpec((tq, LANE), lambda b, hp, i: (b * nq + i, hp)),
        out_shape=jax.ShapeDtypeStruct((bsz * seq, n_heads * LANE // 2), BF16),
        compiler_params=_cp(("arbitrary", "arbitrary", "arbitrary"), VMEM_LIMIT),
        name="attn_prompt",
    )(q, k, v, k, v)


def _softmax_part(s, v):
    m = jnp.max(s, axis=-1, keepdims=True)
    p = jnp.exp2(s - m)
    return m, jnp.sum(p, axis=-1, keepdims=True), _dot(p.astype(BF16), v)


def _softmax_merge(parts):
    m = functools.reduce(jnp.maximum, [p[0] for p in parts])
    w = [jnp.exp2(p[0] - m) for p in parts]
    l = functools.reduce(jnp.add, [wi * p[1] for wi, p in zip(w, parts)])
    acc = functools.reduce(jnp.add, [wi * p[2] for wi, p in zip(w, parts)])
    return m, l, acc


def _attn_paged_kernel(pt_ref, qabs_ref, qpe_ref, cnew_ref, pnew_ref, ckv_hbm, kpe_hbm, o_ref,
                       cbuf, pbuf, sem, m_ref, l_ref, acc_ref, *, ppc, n_groups, n_slots, n_heads, dec_seq, sub):
    c = pl.program_id(1)
    group = pl.program_id(0) * n_groups + c
    n_chunks = pl.num_programs(0) * n_groups * n_slots

    def copies(chunk, sl):
        base = chunk * ppc
        out = []
        for p in range(ppc):
            page = pt_ref[base + p]
            keys = pl.ds(p * PAGE_SIZE, PAGE_SIZE)
            out.append(pltpu.make_async_copy(ckv_hbm.at[page], cbuf.at[sl, keys, :], sem.at[0, sl]))
            out.append(pltpu.make_async_copy(kpe_hbm.at[page], pbuf.at[sl, :, keys], sem.at[1, sl]))
        return out

    def wait_slot(sl):
        pltpu.make_async_copy(cbuf.at[sl], cbuf.at[sl], sem.at[0, sl]).wait()
        pltpu.make_async_copy(pbuf.at[sl], pbuf.at[sl], sem.at[1, sl]).wait()

    @pl.when(group == 0)
    def _():
        for k in range(n_slots - 1):
            for cp in copies(k, k):
                cp.start()

    @pl.when(c == 0)
    def _():
        m_ref[...] = jnp.full(m_ref.shape, -jnp.inf, F32)
        l_ref[...] = jnp.zeros(l_ref.shape, F32)
        acc_ref[...] = jnp.zeros(acc_ref.shape, F32)

    qa = qabs_ref[0]
    qp = qpe_ref[0]
    n_sub = ppc * PAGE_SIZE // sub

    def slot_parts(sl):
        def scores(i):
            keys = slice(i * sub, (i + 1) * sub)
            kc = cbuf[sl, keys, :].astype(BF16)
            return _dot_nt(qa, kc) + _dot(qp, pbuf[sl, :, keys].astype(BF16)), kc

        ahead = [scores(i) for i in range(min(PAGED_SCORES_AHEAD, n_sub))]
        out = []
        for i in range(n_sub):
            s, kc = ahead.pop(0)
            if i + PAGED_SCORES_AHEAD < n_sub:
                ahead.append(scores(i + PAGED_SCORES_AHEAD))
            out.append(_softmax_part(s, kc))
        return out

    parts = [(m_ref[...], l_ref[...], acc_ref[...])]
    for k in range(n_slots):
        wait_slot(k)
        nxt = group * n_slots + (k + n_slots - 1)
        for cp in copies(jnp.where(nxt < n_chunks, nxt, 0), (k + n_slots - 1) % n_slots):
            cp.start()
        parts += slot_parts(k)
    m, l, acc = _softmax_merge(parts)
    m_ref[...] = m
    l_ref[...] = l
    acc_ref[...] = acc

    @pl.when(c == n_groups - 1)
    def _():
        kn = cnew_ref[0]
        s = _dot_nt(qa, kn) + _dot_nt(qp, pnew_ref[0])
        rows = s.shape[0]
        tok = lax.broadcasted_iota(jnp.int32, (rows, LANE), 0) // n_heads
        col = lax.broadcasted_iota(jnp.int32, (rows, LANE), 1)
        s = jnp.where((col <= tok) & (col < dec_seq), s, -jnp.inf)
        _, l2, acc2 = _softmax_merge([(m, l, acc), _softmax_part(s, kn)])
        o_ref[0] = acc2 * (1.0 / l2)

    @pl.when(group == pl.num_programs(0) * n_groups - 1)
    def _():
        for k in range(n_slots - 1):
            wait_slot(k)


def _attn_paged(page_table, qabs, qpe, cnew, pnew, cache_ckv, cache_kpe_t, n_heads, dec_seq):
    dec_b, n_pages = page_table.shape
    n_slots = PAGED_SLOTS
    assert n_pages % n_slots == 0
    ppc = min(PAGES_PER_CHUNK, n_pages // n_slots)
    sub = min(PAGED_KEY_BLOCK, ppc * PAGE_SIZE)
    assert n_pages % (n_slots * ppc) == 0 and cache_ckv.shape[1] == PAGE_SIZE and (ppc * PAGE_SIZE) % sub == 0
    n_groups = n_pages // (n_slots * ppc)
    rows, lora = qabs.shape[1], qabs.shape[2]
    rope = qpe.shape[2]
    kern = functools.partial(_attn_paged_kernel, ppc=ppc, n_groups=n_groups, n_slots=n_slots, n_heads=n_heads,
                             dec_seq=dec_seq, sub=sub)
    grid_spec = pltpu.PrefetchScalarGridSpec(
        num_scalar_prefetch=1,
        grid=(dec_b, n_groups),
        in_specs=[
            pl.BlockSpec((1, rows, lora), lambda b, c, pt: (b, 0, 0)),
            pl.BlockSpec((1, rows, rope), lambda b, c, pt: (b, 0, 0)),
            pl.BlockSpec((1, LANE, lora), lambda b, c, pt: (b, 0, 0)),
            pl.BlockSpec((1, LANE, rope), lambda b, c, pt: (b, 0, 0)),
            pl.BlockSpec(memory_space=pl.ANY),
            pl.BlockSpec(memory_space=pl.ANY),
        ],
        out_specs=pl.BlockSpec((1, rows, lora), lambda b, c, pt: (b, 0, 0)),
        scratch_shapes=[
            pltpu.VMEM((n_slots, ppc * PAGE_SIZE, lora), F32),
            pltpu.VMEM((n_slots, rope, ppc * PAGE_SIZE), F32),
            pltpu.SemaphoreType.DMA((2, n_slots)),
            pltpu.VMEM((rows, 1), F32),
            pltpu.VMEM((rows, 1), F32),
            pltpu.VMEM((rows, lora), F32),
        ],
    )
    return pl.pallas_call(
        kern,
        grid_spec=grid_spec,
        out_shape=jax.ShapeDtypeStruct((dec_b, rows, lora), F32),
        compiler_params=_cp(("arbitrary", "arbitrary"), VMEM_LIMIT),
        name="attn_paged",
    )(page_table.reshape(-1), qabs, qpe, cnew, pnew, cache_ckv, cache_kpe_t)


def _uv_sample_kernel(o_ref, w_ref, out_ref, *, n_heads, lora):
    for p in range(n_heads // 2):
        acc = _dot(o_ref[:, (2 * p) * lora:(2 * p + 1) * lora].astype(BF16), w_ref[2 * p])
        acc = acc + _dot(o_ref[:, (2 * p + 1) * lora:(2 * p + 2) * lora].astype(BF16), w_ref[2 * p + 1])
        out_ref[:, p * LANE:(p + 1) * LANE] = acc.astype(BF16)


def _uv_sample(o_lat, wuv_pad, n_heads, lora):
    ns = o_lat.shape[0]
    nv = n_heads * LANE // 2
    return pl.pallas_call(
        functools.partial(_uv_sample_kernel, n_heads=n_heads, lora=lora),
        grid=(1,),
        in_specs=[pl.BlockSpec(o_lat.shape, lambda i: (0, 0)), pl.BlockSpec(wuv_pad.shape, lambda i: (0, 0, 0))],
        out_specs=pl.BlockSpec((ns, nv), lambda i: (0, 0)),
        out_shape=jax.ShapeDtypeStruct((ns, nv), BF16),
        compiler_params=_cp(("arbitrary",), VMEM_LIMIT),
        name="uv_sample",
    )(o_lat, wuv_pad)


def _router_kernel(x_ref, w_ref, b_ref, o_ref, *, n_experts):
    logits = _dot(x_ref[...], w_ref[...]) + b_ref[...]
    lane = lax.broadcasted_iota(jnp.int32, logits.shape, 1).astype(F32)
    logits = jnp.where(lane < n_experts, logits, -jnp.inf)
    m1 = jnp.max(logits, axis=-1, keepdims=True)
    i1 = jnp.min(jnp.where(logits == m1, lane, float(LANE)), axis=-1, keepdims=True)
    rest = jnp.where(lane == i1, -jnp.inf, logits)
    m2 = jnp.max(rest, axis=-1, keepdims=True)
    i2 = jnp.min(jnp.where(rest == m2, lane, float(LANE)), axis=-1, keepdims=True)
    e2 = jnp.exp(m2 - m1)
    den = 1.0 + e2
    g1 = 1.0 / den
    g2 = e2 / den
    o_ref[...] = jnp.where(lane == 0, i1, jnp.where(lane == 1, i2, jnp.where(lane == 2, g1, jnp.where(lane == 3, g2, 0.0))))


def _router(x, w_pad, b_pad, n_experts):
    r, d = x.shape
    return pl.pallas_call(
        functools.partial(_router_kernel, n_experts=n_experts),
        grid=(r // ROW_TILE,),
        in_specs=[pl.BlockSpec((ROW_TILE, d), lambda i: (i, 0)), pl.BlockSpec((d, LANE), lambda i: (0, 0)),
                  pl.BlockSpec((1, LANE), lambda i: (0, 0))],
        out_specs=pl.BlockSpec((ROW_TILE, LANE), lambda i: (i, 0)),
        out_shape=jax.ShapeDtypeStruct((r, LANE), F32),
        compiler_params=_cp(("arbitrary",), VMEM_LIMIT),
        name="router",
    )(x, w_pad, b_pad)


def _moe_ffn_kernel(te_ref, nu_ref, src_ref, x_hbm, wg_ref, wu_ref, w2_ref, o_ref, xbuf, xb_ref, acc_ref, sem,
                    *, tm, rows_per_step):
    t = pl.program_id(0)
    f = pl.program_id(1)
    nf = pl.num_programs(1)
    used = t < nu_ref[0]
    slot = t % 2
    n_rows = rows_per_step * nf

    def row_copy(tile, r, sl):
        row = src_ref[tile * tm + r]
        return pltpu.make_async_copy(x_hbm.at[pl.ds(row, 1), :], xbuf.at[sl, pl.ds(r, 1), :], sem.at[sl])

    @pl.when((t == 0) & (f == 0))
    def _():
        def body(g, carry):
            for u in range(GATHER_UNROLL):
                row_copy(0, g * GATHER_UNROLL + u, 0).start()
            return carry
        lax.fori_loop(0, n_rows // GATHER_UNROLL, body, 0)

    @pl.when(f == 0)
    def _():
        pltpu.make_async_copy(xbuf.at[slot, pl.ds(0, n_rows)], xbuf.at[slot, pl.ds(0, n_rows)], sem.at[slot]).wait()
        xb_ref[...] = xbuf[slot, :tm].astype(BF16)

    def prefetch():
        for u in range(rows_per_step):
            row_copy(t + 1, f * rows_per_step + u, 1 - slot).start()

    @pl.when(used)
    def _():
        prefetch()
        x = xb_ref[...]
        hcol = (_silu(_dot(x, wg_ref[0])) * _dot(x, wu_ref[0])).astype(BF16)
        y = _dot(hcol, w2_ref[0])

        @pl.when(f == 0)
        def _():
            acc_ref[...] = y

        @pl.when(f > 0)
        def _():
            acc_ref[...] += y

    @pl.when(jnp.logical_not(used))
    def _():
        prefetch()

    @pl.when(f == nf - 1)
    def _():
        o_ref[...] = jnp.where(used, acc_ref[...], 0.0)

    @pl.when((t == pl.num_programs(0) - 1) & (f == nf - 1))
    def _():
        pltpu.make_async_copy(xbuf.at[1 - slot, pl.ds(0, n_rows)], xbuf.at[1 - slot, pl.ds(0, n_rows)],
                              sem.at[1 - slot]).wait()


def _moe_ffn(tile_expert, n_used, src, x, w13, w2):
    d = x.shape[1]
    n_exp, eff = w2.shape[0], w2.shape[1]
    tm, tf = MOE_ROW_TILE, min(MOE_FF_CHUNK, eff)
    n = src.shape[0] - 2 * tm
    assert n % tm == 0 and eff % tf == 0
    nf = eff // tf
    rows_per_step = -(-tm // (nf * GATHER_UNROLL)) * GATHER_UNROLL
    assert rows_per_step * nf <= 2 * tm

    def widx(base):
        def index_map(t, f, te, nu, src):
            live = t < nu[0]
            return (te[t], 0, jnp.where(live, f, nf - 1) + base)
        return index_map

    def w2idx(t, f, te, nu, src):
        return (te[t], jnp.where(t < nu[0], f, nf - 1), 0)

    grid_spec = pltpu.PrefetchScalarGridSpec(
        num_scalar_prefetch=3,
        grid=(n // tm, nf),
        in_specs=[
            pl.BlockSpec(memory_space=pl.ANY),
            pl.BlockSpec((1, d, tf), widx(0)),
            pl.BlockSpec((1, d, tf), widx(nf)),
            pl.BlockSpec((1, tf, d), w2idx),
        ],
        out_specs=pl.BlockSpec((tm, d), lambda t, f, te, nu, src: (t, 0)),
        scratch_shapes=[
            pltpu.VMEM((2, rows_per_step * nf, d), x.dtype),
            pltpu.VMEM((tm, d), BF16),
            pltpu.VMEM((tm, d), F32),
            pltpu.SemaphoreType.DMA((2,)),
        ],
    )
    return pl.pallas_call(
        functools.partial(_moe_ffn_kernel, tm=tm, rows_per_step=rows_per_step),
        grid_spec=grid_spec,
        out_shape=jax.ShapeDtypeStruct((n, d), F32),
        compiler_params=_cp(("arbitrary", "arbitrary"), VMEM_LIMIT),
        name="moe_ffn",
    )(tile_expert, n_used, src, x, w13, w13, w2)


def _combine_ln_kernel(p1_ref, p2_ref, y_hbm, h_ref, route_ref, g_ref, b_ref, o_ref, buf, sem, *, tile, alpha):
    i = pl.program_id(0)
    n = pl.num_programs(0)
    slot = i % 2

    def issue(t, sl):
        def body(g, carry):
            for u in range(GATHER_UNROLL):
                r = g * GATHER_UNROLL + u
                a = p1_ref[t * tile + r]
                b = p2_ref[t * tile + r]
                pltpu.make_async_copy(y_hbm.at[pl.ds(a, 1), :], buf.at[sl, 0, pl.ds(r, 1), :], sem.at[sl]).start()
                pltpu.make_async_copy(y_hbm.at[pl.ds(b, 1), :], buf.at[sl, 1, pl.ds(r, 1), :], sem.at[sl]).start()
            return carry
        lax.fori_loop(0, tile // GATHER_UNROLL, body, 0)

    @pl.when(i == 0)
    def _():
        issue(i, slot)

    @pl.when(i + 1 < n)
    def _():
        issue(i + 1, 1 - slot)

    pltpu.make_async_copy(buf.at[slot], buf.at[slot], sem.at[slot]).wait()
    route = route_ref[...]
    z = alpha * h_ref[...] + (route[:, 2:3] * buf[slot, 0] + route[:, 3:4] * buf[slot, 1])
    o_ref[...] = _layer_norm(z, g_ref[...], b_ref[...])


def _combine_ln(pos1, pos2, y_sorted, h, route, g, b, alpha):
    r, d = h.shape
    tile = GATHER_TILE
    assert r % tile == 0 and tile % GATHER_UNROLL == 0
    grid_spec = pltpu.PrefetchScalarGridSpec(
        num_scalar_prefetch=2,
        grid=(r // tile,),
        in_specs=[
            pl.BlockSpec(memory_space=pl.ANY),
            pl.BlockSpec((tile, d), lambda i, p1, p2: (i, 0)),
            pl.BlockSpec((tile, LANE), lambda i, p1, p2: (i, 0)),
            pl.BlockSpec((1, d), lambda i, p1, p2: (0, 0)),
            pl.BlockSpec((1, d), lambda i, p1, p2: (0, 0)),
        ],
        out_specs=pl.BlockSpec((tile, d), lambda i, p1, p2: (i, 0)),
        scratch_shapes=[pltpu.VMEM((2, 2, tile, d), F32), pltpu.SemaphoreType.DMA((2,))],
    )
    return pl.pallas_call(
        functools.partial(_combine_ln_kernel, tile=tile, alpha=alpha),
        grid_spec=grid_spec,
        out_shape=jax.ShapeDtypeStruct((r, d), F32),
        compiler_params=_cp(("arbitrary",), VMEM_LIMIT),
        name="combine_ln",
    )(pos1, pos2, y_sorted, h, route, g, b)


def _route_plan(route, n_experts, tm):
    r = route.shape[0]
    e = jnp.concatenate([route[:, 0], route[:, 1]]).astype(jnp.int32)
    onehot = (e[:, None] == jnp.arange(n_experts, dtype=jnp.int32)[None, :]).astype(jnp.int32)
    csum = jnp.cumsum(onehot, axis=0)
    rank = jnp.take_along_axis(csum, e[:, None], axis=1)[:, 0] - 1
    tiles = (csum[-1] + tm - 1) // tm
    tile_end = jnp.cumsum(tiles)
    pos = (tile_end - tiles)[e] * tm + rank
    n_tiles = (TOP_K * r) // tm + n_experts
    tok = jnp.arange(TOP_K * r, dtype=jnp.int32) % r
    src = jnp.zeros(((n_tiles + 2) * tm,), jnp.int32).at[pos].set(tok, unique_indices=True)
    n_used = tile_end[-1]
    tile_id = jnp.minimum(jnp.arange(n_tiles, dtype=jnp.int32), n_used - 1)
    tile_expert = jnp.sum((tile_end[None, :] <= tile_id[:, None]).astype(jnp.int32), axis=1)
    return src, tile_expert, n_used.reshape(1).astype(jnp.int32), pos[:r], pos[r:]


def _pad_cols(w, offset, width):
    return jnp.pad(w, ((0, 0), (offset, width - offset - w.shape[1])))


def kernel(x_prompt, x_sample, state_ret, cache_ckv, cache_kpe, page_table, meta_tokens, ln_g, ln_b, ret_w_in, ret_gn_g, ret_w_o, mla_w_dq, mla_q_norm, mla_w_uq, mla_w_o, kv_w_a, kv_norm, kv_w_b, ffn_w13, ffn_w2, moe_w_r, moe_b_r, moe_w13, moe_w2):
    bsz, seq, d = x_prompt.shape
    dec_b, dec_seq, _ = x_sample.shape
    n_ret, _, ret_h, dk, dv = state_ret.shape
    depth = ln_g.shape[0]
    assert depth == 2 and n_ret == 1 and mla_w_dq.shape[0] == 1, "layer pattern: one retention layer then one MLA layer"
    past_len = page_table.shape[1] * PAGE_SIZE
    lora, mla_h, nope_v = kv_w_b.shape
    rope = kv_w_a.shape[1] - lora
    nope = mla_w_uq.shape[2] // mla_h - rope
    v_dim = nope_v - nope
    n_experts = moe_w_r.shape[2]
    assert nope == LANE // 2 and v_dim == LANE // 2 and rope <= LANE // 4
    alpha = (2 * depth) ** 0.25
    mla_scale = (nope + rope) ** -0.5
    half = rope // 2

    n_p, n_s = bsz * seq, dec_b * dec_seq
    meta_row0 = n_p + n_s
    r_real = meta_row0 + N_META
    r_pad = -(-r_real // ROW_TILE) * ROW_TILE
    assert r_pad - meta_row0 >= LANE

    x0 = jnp.concatenate([x_prompt.reshape(n_p, d), x_sample.reshape(n_s, d), meta_tokens,
                          jnp.zeros((r_pad - r_real, d), F32)], axis=0)
    pos = jnp.concatenate([jnp.tile(N_META + jnp.arange(seq), bsz), jnp.tile(past_len + jnp.arange(dec_seq), dec_b),
                           jnp.arange(N_META), jnp.zeros((r_pad - r_real,), jnp.int32)]).astype(F32)

    inv_r = ROPE_BASE ** (-jnp.arange(dk // 2, dtype=F32) / (dk // 2))
    ang_r = pos[:, None] * inv_r[None, :]
    cos_r, sin_r = jnp.cos(ang_r), jnp.sin(ang_r)
    inv_m = ROPE_BASE ** (-jnp.arange(half, dtype=F32) / half)
    ang_m = pos[:, None] * inv_m[None, :]
    cos_m = jnp.concatenate([jnp.cos(ang_m)] * 2, axis=1)
    sin_m = jnp.concatenate([jnp.sin(ang_m)] * 2, axis=1)
    cos_a, sin_a = _pad_cols(cos_m, 0, LANE), _pad_cols(sin_m, 0, LANE)
    cos_b, sin_b = _pad_cols(cos_m, nope, LANE), _pad_cols(sin_m, nope, LANE)
    q_scale = mla_scale * LOG2E
    q_ctab = (_pad_cols(jnp.ones((r_pad, nope), F32), 0, LANE) + cos_b) * q_scale
    q_stab = sin_b * q_scale

    def rot(w):
        return jnp.concatenate([-w[..., half:], w[..., :half]], axis=-1)

    qkvg = _ret_proj(x0, ret_w_in[0], cos_r, sin_r, ret_h, dk)
    gn = ret_gn_g[0].reshape(1, -1)
    mix_p, mix_meta, st_p = _ret_prompt(qkvg, gn, _decay_tables(ret_h, dk, RET_CHUNK, RET_CHUNK),
                                        bsz, seq, meta_row0, ret_h, dk, dv)
    mix_s, st_s = _ret_sample(qkvg, state_ret[0], gn,
                              _decay_tables(ret_h, dk, SAMPLE_BATCH_TILE * dec_seq, dec_seq),
                              n_p, dec_b, dec_seq, ret_h, dk, dv)
    mix = jnp.concatenate([mix_p, mix_s, mix_meta, jnp.zeros((r_pad - r_real, mix_p.shape[1]), BF16)], axis=0)
    row = lambda v: v.reshape(1, -1)
    h1, h1b = _proj_ln(mix, ret_w_o[0], x0, row(ln_g[0, 0]), row(ln_b[0, 0]), alpha)
    h2, h2b = _ffn_ln(h1b, ffn_w13[0].astype(BF16), ffn_w2[0].astype(BF16), h1, row(ln_g[0, 1]), row(ln_b[0, 1]), alpha)

    wc, wpe = kv_w_a[:, :lora], kv_w_a[:, lora:]
    wkv = jnp.concatenate([wc, _pad_cols(wpe, 0, LANE), _pad_cols(rot(wpe), 0, LANE),
                           _pad_cols(wpe, nope, LANE), _pad_cols(rot(wpe), nope, LANE)], axis=1).astype(BF16)
    wuk = jnp.pad(kv_w_b[:, :, :nope], ((0, 0), (0, 0), (0, LANE - nope))).reshape(lora, mla_h * LANE).astype(BF16)
    even = (jnp.arange(mla_h) % 2 == 0)[None, :, None]
    wuv_h = kv_w_b[:, :, nope:]
    wuv = jnp.where(even, jnp.pad(wuv_h, ((0, 0), (0, 0), (0, LANE - v_dim))),
                    jnp.pad(wuv_h, ((0, 0), (0, 0), (LANE - v_dim, 0)))).reshape(lora, mla_h * LANE).astype(BF16)
    lane_id = jnp.arange(LANE)[None, None, :]
    ones_row = jnp.where(even, lane_id == v_dim, lane_id == 0).astype(F32).reshape(1, mla_h * LANE)
    ckv, kpe, k_heads, v_heads = _kv_proj(h2b, wkv, row(kv_norm), (cos_a, sin_a, cos_b, sin_b), wuk, wuv, ones_row,
                                          lora, rope, mla_h)

    wuq = mla_w_uq[0].reshape(-1, mla_h, nope + rope)
    wq_n, wq_r = wuq[:, :, :nope], wuq[:, :, nope:]
    pad_h = lambda w, off: jnp.pad(w, ((0, 0), (0, 0), (off, LANE - off - w.shape[2]))).reshape(w.shape[0], mla_h * LANE)
    w_q1 = (pad_h(wq_n, 0) + pad_h(wq_r, nope)).astype(BF16)
    w_q2 = pad_h(rot(wq_r), nope).astype(BF16)
    wdq = mla_w_dq[0].astype(BF16)
    qn_g = row(mla_q_norm[0])
    q_heads = _q_proj(h2b, wdq, qn_g, w_q1, w_q2, q_ctab, q_stab, mla_h)
    attn_p = _attn_prompt(q_heads, k_heads, v_heads, bsz, seq, meta_row0, mla_h)

    wukt = jnp.pad(jnp.transpose(kv_w_b[:, :, :nope], (1, 2, 0)), ((0, 0), (0, LANE - nope), (0, 0))).astype(BF16)
    w_pe = wq_r.reshape(-1, mla_h * rope).astype(BF16)
    w_pe_rot = rot(wq_r).reshape(-1, mla_h * rope).astype(BF16)
    cs = jnp.tile(cos_m[n_p:n_p + n_s], (1, mla_h))
    ss = jnp.tile(sin_m[n_p:n_p + n_s], (1, mla_h))
    qabs, qpe = _q_sample(h2b[n_p:n_p + n_s], wdq, qn_g, pad_h(wq_n, 0).astype(BF16), wukt, w_pe, w_pe_rot, cs, ss,
                          mla_h, lora, q_scale)
    ckv_s, kpe_s = ckv[n_p:n_p + n_s], kpe[n_p:n_p + n_s]
    new_pad = lambda a: jnp.pad(a.reshape(dec_b, dec_seq, -1), ((0, 0), (0, LANE - dec_seq), (0, 0))).astype(BF16)
    o_lat = _attn_paged(page_table, qabs.reshape(dec_b, dec_seq * mla_h, lora), qpe.reshape(dec_b, dec_seq * mla_h, rope),
                        new_pad(ckv_s), new_pad(kpe_s), cache_ckv, jnp.swapaxes(cache_kpe, 1, 2), mla_h, dec_seq)
    wuv_t = jnp.transpose(wuv_h, (1, 0, 2))
    wuv_pad = jnp.where(jnp.transpose(even, (1, 0, 2)), jnp.pad(wuv_t, ((0, 0), (0, 0), (0, LANE - v_dim))),
                        jnp.pad(wuv_t, ((0, 0), (0, 0), (LANE - v_dim, 0)))).astype(BF16)
    attn_s = _uv_sample(o_lat.reshape(n_s, mla_h * lora), wuv_pad, mla_h, lora)
    attn = jnp.concatenate([attn_p, attn_s, jnp.zeros((r_pad - meta_row0, attn_p.shape[1]), BF16)], axis=0)
    h3, h3b = _proj_ln(attn, mla_w_o[0], h2, row(ln_g[1, 0]), row(ln_b[1, 0]), alpha)

    w_r = _pad_cols(moe_w_r[0], 0, LANE).astype(BF16)
    b_r = _pad_cols(moe_b_r[0].reshape(1, -1), 0, LANE)
    route = _router(h3b, w_r, b_r, n_experts)
    src, tile_expert, n_used, pos1, pos2 = _route_plan(route, n_experts, MOE_ROW_TILE)
    y_sorted = _moe_ffn(tile_expert, n_used, src, h3, moe_w13[0].astype(BF16), moe_w2[0].astype(BF16))
    h4 = _combine_ln(pos1, pos2, y_sorted, h3, route, row(ln_g[1, 1]), row(ln_b[1, 1]), alpha)

    y_prompt = h4[:n_p].reshape(bsz, seq, d)
    y_sample = h4[n_p:n_p + n_s].reshape(dec_b, dec_seq, d)

    def with_meta(a):
        meta = jnp.broadcast_to(a[meta_row0:r_real][None], (bsz, N_META, a.shape[1]))
        return jnp.concatenate([meta, a[:n_p].reshape(bsz, seq, -1)], axis=1)

    return (y_prompt, y_sample, st_p[None], with_meta(ckv), with_meta(kpe), st_s[None],
            ckv_s.reshape(dec_b, dec_seq, lora), kpe_s.reshape(dec_b, dec_seq, rope))
```

```python
import functools
import math

import jax
import jax.numpy as jnp
from jax import lax
from jax.experimental import pallas as pl
from jax.experimental.pallas import tpu as pltpu

F32 = jnp.float32
BF16 = jnp.bfloat16

N_META = 16
TOP_K = 2
ROPE_BASE = 10000.0
LN_EPS = 1e-5
RMS_EPS = 1e-6
PAGE_SIZE = 128
RET_CHUNK = 128

LANE = 128
SUBLANES = 8
ROW_TILE = 512
RET_PROJ_TILE = 1024
ATTN_TQ = 1024
ATTN_TK = 512
FFN_CHUNK = 256
MOE_ROW_TILE = 512
MOE_FF_CHUNK = 512
GATHER_TILE = 256
GATHER_UNROLL = 8
PAGES_PER_CHUNK = 32
PAGED_SLOTS = 4
PAGED_KEY_BLOCK = 1024
PAGED_SCORES_AHEAD = 8
LOG2E = math.log2(math.e)
SAMPLE_BATCH_TILE = 4
VMEM_LIMIT = 56 * 1024 * 1024


def _cp(sem, vmem=None):
    return pltpu.CompilerParams(dimension_semantics=sem, vmem_limit_bytes=vmem)


def _layer_norm(z, g, b):
    mu = jnp.mean(z, axis=-1, keepdims=True)
    zc = z - mu
    var = jnp.mean(zc * zc, axis=-1, keepdims=True)
    return zc * lax.rsqrt(var + LN_EPS) * g + b


def _rms_norm(c, g):
    return c * lax.rsqrt(jnp.mean(c * c, axis=-1, keepdims=True) + RMS_EPS) * g


def _silu(x):
    return x * (1.0 / (1.0 + jnp.exp(-x)))


def _dot(a, b):
    return jnp.dot(a, b, preferred_element_type=F32)


def _dot_nt(a, b):
    return lax.dot_general(a, b, (((1,), (1,)), ((), ())), preferred_element_type=F32)


def _dot_tn(a, b):
    return lax.dot_general(a, b, (((0,), (0,)), ((), ())), preferred_element_type=F32)


def _ret_proj_kernel(x_ref, w_ref, cos_ref, sin_ref, o_ref, wbf_ref, *, n_heads, dk, k_scale):
    j = pl.program_id(0)
    i = pl.program_id(1)

    @pl.when(i == 0)
    def _():
        wbf_ref[...] = w_ref[...].astype(BF16)

    y = _dot(x_ref[...].astype(BF16), wbf_ref[...])
    half = dk // 2

    @pl.when(j < 2)
    def _():
        c = cos_ref[...]
        s = sin_ref[...]
        scale = jnp.where(j == 1, k_scale, 1.0).astype(F32)
        for h in range(n_heads):
            x1 = y[:, h * dk:h * dk + half]
            x2 = y[:, h * dk + half:(h + 1) * dk]
            o_ref[:, h * dk:h * dk + half] = ((x1 * c - x2 * s) * scale).astype(BF16)
            o_ref[:, h * dk + half:(h + 1) * dk] = ((x1 * s + x2 * c) * scale).astype(BF16)

    @pl.when(j >= 2)
    def _():
        o_ref[...] = y.astype(BF16)


def _ret_proj(x, w_in, cos_r, sin_r, n_heads, dk):
    r, d = x.shape
    n = w_in.shape[1]
    qk = n_heads * dk
    assert n % qk == 0 and dk // 2 == LANE
    kern = functools.partial(_ret_proj_kernel, n_heads=n_heads, dk=dk, k_scale=dk ** -0.5)
    tile = RET_PROJ_TILE if r % RET_PROJ_TILE == 0 else ROW_TILE
    return pl.pallas_call(
        kern,
        grid=(n // qk, r // tile),
        in_specs=[
            pl.BlockSpec((tile, d), lambda j, i: (i, 0)),
            pl.BlockSpec((d, qk), lambda j, i: (0, j)),
            pl.BlockSpec((tile, LANE), lambda j, i: (i, 0)),
            pl.BlockSpec((tile, LANE), lambda j, i: (i, 0)),
        ],
        out_specs=pl.BlockSpec((tile, qk), lambda j, i: (i, j)),
        out_shape=jax.ShapeDtypeStruct((r, n), BF16),
        scratch_shapes=[pltpu.VMEM((d, qk), BF16)],
        compiler_params=_cp(("arbitrary", "arbitrary"), VMEM_LIMIT),
        name="ret_proj",
    )(x, w_in, cos_r, sin_r)


def _group_norm_gate(o, g, gn):
    mu = jnp.mean(o, axis=-1, keepdims=True)
    oc = o - mu
    var = jnp.mean(oc * oc, axis=-1, keepdims=True)
    return (_silu(g.astype(F32)) * (oc * lax.rsqrt(var + LN_EPS) * gn)).astype(BF16)


def _ret_prompt_kernel(q_ref, k_ref, v_ref, g_ref, qm_ref, km_ref, vm_ref, gm_ref, gn_ref,
                       dec_ref, rdec_ref, cdec_ref, cpow_ref,
                       o_ref, om_ref, st_ref, s_ref, *, n_heads, dk, dv, chunk, n_meta, tile):
    t = pl.program_id(1)
    nt = pl.num_programs(1)

    @pl.when(t == 0)
    def _():
        for h in range(n_heads):
            q = qm_ref[:, h * dk:(h + 1) * dk]
            k = km_ref[:, h * dk:(h + 1) * dk]
            v = vm_ref[:, h * dv:(h + 1) * dv]
            sc = _dot_nt(q, k) * dec_ref[h, :n_meta, :n_meta]
            o = _dot(sc.astype(BF16), v)
            om_ref[:, h * dv:(h + 1) * dv] = _group_norm_gate(
                o, gm_ref[:, h * dv:(h + 1) * dv], gn_ref[:, h * dv:(h + 1) * dv])
            k_out = (k.astype(F32) * cdec_ref[h, chunk - n_meta:, :]).astype(BF16)
            s_ref[h] = _dot_tn(k_out, v)

    for h in range(n_heads):
        for c in range(tile // chunk):
            rows = slice(c * chunk, (c + 1) * chunk)
            q = q_ref[rows, h * dk:(h + 1) * dk]
            k = k_ref[rows, h * dk:(h + 1) * dk]
            v = v_ref[rows, h * dv:(h + 1) * dv]
            s_old = s_ref[h]
            sc = _dot_nt(q, k) * dec_ref[h]
            q_in = (q.astype(F32) * rdec_ref[h]).astype(BF16)
            o = _dot(sc.astype(BF16), v) + _dot(q_in, s_old.astype(BF16))
            k_out = (k.astype(F32) * cdec_ref[h]).astype(BF16)
            s_ref[h] = cpow_ref[h] * s_old + _dot_tn(k_out, v)
            o_ref[rows, h * dv:(h + 1) * dv] = _group_norm_gate(
                o, g_ref[rows, h * dv:(h + 1) * dv], gn_ref[:, h * dv:(h + 1) * dv])

    @pl.when(t == nt - 1)
    def _():
        st_ref[0] = s_ref[...]


def _ret_prompt(qkvg, gn_g, tabs, bsz, seq, meta_row0, n_heads, dk, dv):
    qk, vw = n_heads * dk, n_heads * dv
    tile = ROW_TILE
    assert seq % tile == 0 and tile % RET_CHUNK == 0 and meta_row0 % N_META == 0 and vw == 2 * qk
    nt = seq // tile
    mb = meta_row0 // N_META
    dec, rdec, cdec, cpow = tabs
    kern = functools.partial(_ret_prompt_kernel, n_heads=n_heads, dk=dk, dv=dv, chunk=RET_CHUNK,
                             n_meta=N_META, tile=tile)
    full = lambda *shape: pl.BlockSpec(shape, lambda b, t: (0,) * len(shape))
    return pl.pallas_call(
        kern,
        grid=(bsz, nt),
        in_specs=[
            pl.BlockSpec((tile, qk), lambda b, t: (b * nt + t, 0)),
            pl.BlockSpec((tile, qk), lambda b, t: (b * nt + t, 1)),
            pl.BlockSpec((tile, vw), lambda b, t: (b * nt + t, 1)),
            pl.BlockSpec((tile, vw), lambda b, t: (b * nt + t, 2)),
            pl.BlockSpec((N_META, qk), lambda b, t: (mb, 0)),
            pl.BlockSpec((N_META, qk), lambda b, t: (mb, 1)),
            pl.BlockSpec((N_META, vw), lambda b, t: (mb, 1)),
            pl.BlockSpec((N_META, vw), lambda b, t: (mb, 2)),
            full(1, vw),
            full(n_heads, RET_CHUNK, RET_CHUNK),
            full(n_heads, RET_CHUNK, dk),
            full(n_heads, RET_CHUNK, dk),
            full(n_heads, 1, 1),
        ],
        out_specs=[
            pl.BlockSpec((tile, vw), lambda b, t: (b * nt + t, 0)),
            pl.BlockSpec((N_META, vw), lambda b, t: (0, 0)),
            pl.BlockSpec((1, n_heads, dk, dv), lambda b, t: (b, 0, 0, 0)),
        ],
        out_shape=[
            jax.ShapeDtypeStruct((bsz * seq, vw), BF16),
            jax.ShapeDtypeStruct((N_META, vw), BF16),
            jax.ShapeDtypeStruct((bsz, n_heads, dk, dv), F32),
        ],
        scratch_shapes=[pltpu.VMEM((n_heads, dk, dv), F32)],
        compiler_params=_cp(("arbitrary", "arbitrary"), VMEM_LIMIT),
        name="ret_prompt",
    )(qkvg, qkvg, qkvg, qkvg, qkvg, qkvg, qkvg, qkvg, gn_g, dec, rdec, cdec, cpow)


def _ret_sample_kernel(q_ref, k_ref, v_ref, g_ref, s0_ref, gn_ref, dec_ref, rdec_ref, cdec_ref, cpow_ref,
                       o_ref, st_ref, *, n_heads, dk, dv, nb, dec_seq):
    rows = nb * dec_seq
    row_b = lax.broadcasted_iota(jnp.int32, (rows, 1), 0) // dec_seq
    for h in range(n_heads):
        q = q_ref[:, h * dk:(h + 1) * dk]
        k = k_ref[:, h * dk:(h + 1) * dk]
        v = v_ref[:, h * dv:(h + 1) * dv]
        sc = _dot_nt(q, k) * dec_ref[h]
        o = _dot(sc.astype(BF16), v)
        q_in = q.astype(F32) * rdec_ref[h]
        k_out = k.astype(F32) * cdec_ref[h]
        for b in range(nb):
            s_old = s0_ref[0, b, h]
            sel = row_b == b
            o = o + _dot(jnp.where(sel, q_in, 0.0).astype(BF16), s_old.astype(BF16))
            st_ref[0, b, h] = cpow_ref[h] * s_old + _dot_tn(jnp.where(sel, k_out, 0.0).astype(BF16), v)
        o_ref[:, h * dv:(h + 1) * dv] = _group_norm_gate(
            o, g_ref[:, h * dv:(h + 1) * dv], gn_ref[:, h * dv:(h + 1) * dv])


def _ret_sample(qkvg, state, gn_g, tabs, row0, dec_b, dec_seq, n_heads, dk, dv):
    qk, vw = n_heads * dk, n_heads * dv
    nb = SAMPLE_BATCH_TILE
    rows = nb * dec_seq
    assert dec_b % nb == 0 and row0 % rows == 0 and rows % 16 == 0
    rb = row0 // rows
    dec, rdec, cdec, cpow = tabs
    state5 = state.reshape(dec_b // nb, nb, n_heads, dk, dv)
    kern = functools.partial(_ret_sample_kernel, n_heads=n_heads, dk=dk, dv=dv, nb=nb, dec_seq=dec_seq)
    full = lambda *shape: pl.BlockSpec(shape, lambda i: (0,) * len(shape))
    out, st = pl.pallas_call(
        kern,
        grid=(dec_b // nb,),
        in_specs=[
            pl.BlockSpec((rows, qk), lambda i: (rb + i, 0)),
            pl.BlockSpec((rows, qk), lambda i: (rb + i, 1)),
            pl.BlockSpec((rows, vw), lambda i: (rb + i, 1)),
            pl.BlockSpec((rows, vw), lambda i: (rb + i, 2)),
            pl.BlockSpec((1, nb, n_heads, dk, dv), lambda i: (i, 0, 0, 0, 0)),
            full(1, vw),
            full(n_heads, rows, rows),
            full(n_heads, rows, dk),
            full(n_heads, rows, dk),
            full(n_heads, 1, 1),
        ],
        out_specs=[
            pl.BlockSpec((rows, vw), lambda i: (i, 0)),
            pl.BlockSpec((1, nb, n_heads, dk, dv), lambda i: (i, 0, 0, 0, 0)),
        ],
        out_shape=[
            jax.ShapeDtypeStruct((dec_b * dec_seq, vw), BF16),
            jax.ShapeDtypeStruct(state5.shape, F32),
        ],
        compiler_params=_cp(("arbitrary",), VMEM_LIMIT),
        name="ret_sample",
    )(qkvg, qkvg, qkvg, qkvg, state5, gn_g, dec, rdec, cdec, cpow)
    return out, st.reshape(dec_b, n_heads, dk, dv)


def _decay_tables(n_heads, dk, chunk, group):
    lg = jnp.log1p(-jnp.exp2(-5.0 - jnp.arange(n_heads, dtype=F32)))
    r = jnp.arange(chunk)
    t = (r % group).astype(F32)
    diff = t[:, None] - t[None, :]
    same = (r[:, None] // group) == (r[None, :] // group)
    dec = jnp.where(same & (diff >= 0), jnp.exp(jnp.maximum(diff, 0.0)[None] * lg[:, None, None]), 0.0)
    rdec = jnp.exp((t[None, :] + 1.0) * lg[:, None])
    cdec = jnp.exp((group - 1.0 - t)[None, :] * lg[:, None])
    cpow = jnp.exp(group * lg)
    bc = lambda a: jnp.broadcast_to(a[:, :, None], (n_heads, chunk, dk))
    return dec, bc(rdec), bc(cdec), cpow.reshape(n_heads, 1, 1)


def _store_row_tiles(ref, y):
    rows = y.shape[0]
    for s in range(SUBLANES):
        ref[pl.ds(s, rows, stride=SUBLANES), :] = y[:, s * LANE:(s + 1) * LANE]


def _load_row_tiles(ref, lead, rows):
    return [ref[lead + (pl.ds(s, rows, stride=SUBLANES), slice(None))] for s in range(SUBLANES)]


def _proj_ln_kernel(a_ref, w_ref, h_ref, g_ref, b_ref, o_ref, obf_ref, *rest, alpha, row_tiles):
    wbf_ref = rest[-1]

    @pl.when(pl.program_id(0) == 0)
    def _():
        wbf_ref[...] = w_ref[...].astype(BF16)

    z = alpha * h_ref[...] + _dot(a_ref[...], wbf_ref[...])
    y = _layer_norm(z, g_ref[...], b_ref[...])
    o_ref[...] = y
    obf_ref[...] = y.astype(BF16)
    if row_tiles:
        _store_row_tiles(rest[0], y)


def _proj_ln(a, w, h, g, b, alpha, row_tiles=False):
    r, k = a.shape
    d = w.shape[1]
    assert not row_tiles or d == SUBLANES * LANE
    row = lambda n: pl.BlockSpec((ROW_TILE, n), lambda i: (i, 0))
    full = lambda *shape: pl.BlockSpec(shape, lambda i: (0,) * len(shape))
    out_specs = [row(d), row(d)]
    out_shape = [jax.ShapeDtypeStruct((r, d), F32), jax.ShapeDtypeStruct((r, d), BF16)]
    if row_tiles:
        out_specs.append(pl.BlockSpec((ROW_TILE * SUBLANES, LANE), lambda i: (i, 0)))
        out_shape.append(jax.ShapeDtypeStruct((r * SUBLANES, LANE), F32))
    return pl.pallas_call(
        functools.partial(_proj_ln_kernel, alpha=alpha, row_tiles=row_tiles),
        grid=(r // ROW_TILE,),
        in_specs=[row(k), full(k, d), row(d), full(1, d), full(1, d)],
        out_specs=out_specs,
        out_shape=out_shape,
        scratch_shapes=[pltpu.VMEM((k, d), BF16)],
        compiler_params=_cp(("arbitrary",), VMEM_LIMIT),
        name="proj_ln",
    )(a, w, h, g, b)


def _ffn_ln_kernel(x_ref, w13_ref, w2_ref, h_ref, g_ref, b_ref, o_ref, obf_ref, *, alpha, d_ff, chunk):
    x = x_ref[...]
    acc = alpha * h_ref[...]
    for c in range(d_ff // chunk):
        gate = _dot(x, w13_ref[:, c * chunk:(c + 1) * chunk])
        up = _dot(x, w13_ref[:, d_ff + c * chunk:d_ff + (c + 1) * chunk])
        acc = acc + _dot((_silu(gate) * up).astype(BF16), w2_ref[c * chunk:(c + 1) * chunk, :])
    y = _layer_norm(acc, g_ref[...], b_ref[...])
    o_ref[...] = y
    obf_ref[...] = y.astype(BF16)


def _ffn_ln(x, w13, w2, h, g, b, alpha):
    r, d = x.shape
    d_ff = w2.shape[0]
    assert d_ff % FFN_CHUNK == 0
    row = lambda n: pl.BlockSpec((ROW_TILE, n), lambda i: (i, 0))
    full = lambda *shape: pl.BlockSpec(shape, lambda i: (0,) * len(shape))
    return pl.pallas_call(
        functools.partial(_ffn_ln_kernel, alpha=alpha, d_ff=d_ff, chunk=FFN_CHUNK),
        grid=(r // ROW_TILE,),
        in_specs=[row(d), full(d, 2 * d_ff), full(d_ff, d), row(d), full(1, d), full(1, d)],
        out_specs=[row(d), row(d)],
        out_shape=[jax.ShapeDtypeStruct((r, d), F32), jax.ShapeDtypeStruct((r, d), BF16)],
        compiler_params=_cp(("arbitrary",), VMEM_LIMIT),
        name="ffn_ln",
    )(x, w13, w2, h, g, b)


def _kv_kernel(x_ref, wkv_ref, g_ref, ca_ref, sa_ref, cb_ref, sb_ref, wuk_ref, wuv_ref, ones_ref,
               ckv_ref, kpe_ref, k_ref, v_ref, *, lora, rope, n_heads):
    y = _dot(x_ref[...], wkv_ref[...])
    cn = _rms_norm(y[:, :lora], g_ref[...])
    ckv_ref[...] = cn
    kpe_a = y[:, lora:lora + LANE] * ca_ref[...] + y[:, lora + LANE:lora + 2 * LANE] * sa_ref[...]
    kpe_ref[...] = kpe_a[:, :rope]
    kpe_b = y[:, lora + 2 * LANE:lora + 3 * LANE] * cb_ref[...] + y[:, lora + 3 * LANE:lora + 4 * LANE] * sb_ref[...]
    cb = cn.astype(BF16)
    kn = _dot(cb, wuk_ref[...])
    for h in range(n_heads):
        k_ref[:, h * LANE:(h + 1) * LANE] = (kn[:, h * LANE:(h + 1) * LANE] + kpe_b).astype(BF16)
    v_ref[...] = (_dot(cb, wuv_ref[...]) + ones_ref[...]).astype(BF16)


def _kv_proj(x, wkv, g, tabs, wuk, wuv, ones_row, lora, rope, n_heads):
    r, d = x.shape
    nv = wuv.shape[1]
    row = lambda n: pl.BlockSpec((ROW_TILE, n), lambda i: (i, 0))
    full = lambda *shape: pl.BlockSpec(shape, lambda i: (0,) * len(shape))
    return pl.pallas_call(
        functools.partial(_kv_kernel, lora=lora, rope=rope, n_heads=n_heads),
        grid=(r // ROW_TILE,),
        in_specs=[row(d), full(*wkv.shape), full(1, lora), row(LANE), row(LANE), row(LANE), row(LANE),
                  full(*wuk.shape), full(*wuv.shape), full(1, nv)],
        out_specs=[row(lora), row(rope), row(n_heads * LANE), row(nv)],
        out_shape=[jax.ShapeDtypeStruct((r, lora), F32), jax.ShapeDtypeStruct((r, rope), F32),
                   jax.ShapeDtypeStruct((r, n_heads * LANE), BF16), jax.ShapeDtypeStruct((r, nv), BF16)],
        compiler_params=_cp(("arbitrary",), VMEM_LIMIT),
        name="kv_proj",
    )(x, wkv, g, *tabs, wuk, wuv, ones_row)


def _q_kernel(x_ref, wdq_ref, g_ref, w1_ref, w2_ref, c_ref, s_ref, q_ref, *, n_heads):
    qc = _rms_norm(_dot(x_ref[...], wdq_ref[...]), g_ref[...]).astype(BF16)
    a = _dot(qc, w1_ref[...])
    b = _dot(qc, w2_ref[...])
    c = c_ref[...]
    s = s_ref[...]
    for h in range(n_heads):
        cols = slice(h * LANE, (h + 1) * LANE)
        q_ref[:, cols] = (a[:, cols] * c + b[:, cols] * s).astype(BF16)


def _q_proj(x, wdq, g, w1, w2, c_tab, s_tab, n_heads):
    r, d = x.shape
    ql = wdq.shape[1]
    row = lambda n: pl.BlockSpec((ROW_TILE, n), lambda i: (i, 0))
    full = lambda *shape: pl.BlockSpec(shape, lambda i: (0,) * len(shape))
    return pl.pallas_call(
        functools.partial(_q_kernel, n_heads=n_heads),
        grid=(r // ROW_TILE,),
        in_specs=[row(d), full(d, ql), full(1, ql), full(*w1.shape), full(*w2.shape), row(LANE), row(LANE)],
        out_specs=row(n_heads * LANE),
        out_shape=jax.ShapeDtypeStruct((r, n_heads * LANE), BF16),
        compiler_params=_cp(("arbitrary",), VMEM_LIMIT),
        name="q_proj",
    )(x, wdq, g, w1, w2, c_tab, s_tab)


def _q_sample_kernel(x_ref, wdq_ref, g_ref, w1_ref, wukt_ref, wpe_ref, wrot_ref, c_ref, s_ref,
                     qabs_ref, qpe_ref, *, n_heads, lora, scale):
    qc = _rms_norm(_dot(x_ref[...], wdq_ref[...]), g_ref[...]).astype(BF16)
    qn = (_dot(qc, w1_ref[...]) * scale).astype(BF16)
    for h in range(n_heads):
        qabs_ref[:, h * lora:(h + 1) * lora] = _dot(qn[:, h * LANE:(h + 1) * LANE], wukt_ref[h]).astype(BF16)
    qpe = _dot(qc, wpe_ref[...]) * c_ref[...] + _dot(qc, wrot_ref[...]) * s_ref[...]
    qpe_ref[...] = (qpe * scale).astype(BF16)


def _q_sample(x, wdq, g, w1, wukt, wpe, wrot, c_tab, s_tab, n_heads, lora, scale):
    ns = x.shape[0]
    args = (x, wdq, g, w1, wukt, wpe, wrot, c_tab, s_tab)
    full = lambda a: pl.BlockSpec(a.shape, lambda i, nd=a.ndim: (0,) * nd)
    return pl.pallas_call(
        functools.partial(_q_sample_kernel, n_heads=n_heads, lora=lora, scale=scale),
        grid=(1,),
        in_specs=[full(a) for a in args],
        out_specs=[pl.BlockSpec((ns, n_heads * lora), lambda i: (0, 0)),
                   pl.BlockSpec((ns, wpe.shape[1]), lambda i: (0, 0))],
        out_shape=[jax.ShapeDtypeStruct((ns, n_heads * lora), BF16),
                   jax.ShapeDtypeStruct((ns, wpe.shape[1]), BF16)],
        compiler_params=_cp(("arbitrary",), VMEM_LIMIT),
        name="q_sample",
    )(*args)


def _attn_step(m, acc, s, v):
    m_new = jnp.maximum(m, jnp.max(s, axis=-1, keepdims=True))
    p = jnp.exp2((s - m_new).astype(BF16))
    return m_new, jnp.exp2(m - m_new) * acc + _dot(p, v)


def _attn_prompt_kernel(q_ref, k_ref, v_ref, km_ref, vm_ref, o_ref, *, tq, tk, n_meta):
    i = pl.program_id(2)
    lane = lax.broadcasted_iota(jnp.int32, (tq, LANE), 1)
    heads = (slice(0, LANE), slice(LANE, 2 * LANE))
    qs = [q_ref[:, h] for h in heads]

    carry = []
    for q, h in zip(qs, heads):
        s = jnp.where(lane < n_meta, _dot_nt(q, km_ref[:, h]), -jnp.inf)
        m = jnp.max(s, axis=-1, keepdims=True)
        carry += [m, _dot(jnp.exp2((s - m).astype(BF16)), vm_ref[:, h])]

    def tile(j, carry, diag):
        off = pl.multiple_of(j * tk, tk)
        lo = 0 if diag is None else diag * tk
        out = []
        for e, (q, h) in enumerate(zip(qs, heads)):
            s = _dot_nt(q[lo:], k_ref[pl.ds(off, tk), h])
            if diag is not None:
                row = lax.broadcasted_iota(jnp.int32, (tq - lo, tk), 0)
                col = lax.broadcasted_iota(jnp.int32, (tq - lo, tk), 1)
                s = jnp.where(col <= row, s, -jnp.inf)
            m, acc = carry[2 * e], carry[2 * e + 1]
            m_new, acc_new = _attn_step(m[lo:], acc[lo:], s, v_ref[pl.ds(off, tk), h])
            if lo:
                m_new = jnp.concatenate([m[:lo], m_new], axis=0)
                acc_new = jnp.concatenate([acc[:lo], acc_new], axis=0)
            out += [m_new, acc_new]
        return out

    per_q = tq // tk
    carry = lax.fori_loop(0, i * per_q, lambda j, c: tuple(tile(j, c, None)), tuple(carry))
    for dj in range(per_q):
        carry = tile(i * per_q + dj, carry, dj)
    _, acc0, _, acc1 = carry
    l0 = jnp.sum(jnp.where(lane == LANE // 2, acc0, 0.0), axis=-1, keepdims=True)
    l1 = jnp.sum(jnp.where(lane == 0, acc1, 0.0), axis=-1, keepdims=True)
    o_ref[...] = jnp.where(lane < LANE // 2, acc0 * (1.0 / l0), acc1 * (1.0 / l1)).astype(BF16)


def _attn_prompt(q, k, v, bsz, seq, meta_row0, n_heads):
    tq, tk = min(ATTN_TQ, seq), ATTN_TK
    assert tq % tk == 0 and seq % tq == 0 and meta_row0 % LANE == 0 and n_heads % 2 == 0
    assert v.shape[1] == n_heads * LANE
    nq = seq // tq
    mb = meta_row0 // LANE
    pair = 2 * LANE
    return pl.pallas_call(
        functools.partial(_attn_prompt_kernel, tq=tq, tk=tk, n_meta=N_META),
        grid=(bsz, n_heads // 2, nq),
        in_specs=[
            pl.BlockSpec((tq, pair), lambda b, hp, i: (b * nq + i, hp)),
            pl.BlockSpec((seq, pair), lambda b, hp, i: (b, hp)),
            pl.BlockSpec((seq, pair), lambda b, hp, i: (b, hp)),
            pl.BlockSpec((LANE, pair), lambda b, hp, i: (mb, hp)),
            pl.BlockSpec((LANE, pair), lambda b, hp, i: (mb, hp)),
        ],
        out_specs=pl.BlockSpec((tq, LANE), lambda b, hp, i: (b * nq + i, hp)),
        out_shape=jax.ShapeDtypeStruct((bsz * seq, n_heads * LANE // 2), BF16),
        compiler_params=_cp(("arbitrary", "arbitrary", "arbitrary"), VMEM_LIMIT),
        name="attn_prompt",
    )(q, k, v, k, v)


def _softmax_part(s, v):
    m = jnp.max(s, axis=-1, keepdims=True)
    p = jnp.exp2(s - m)
    return m, jnp.sum(p, axis=-1, keepdims=True), _dot(p.astype(BF16), v)


def _softmax_merge(parts):
    m = functools.reduce(jnp.maximum, [p[0] for p in parts])
    w = [jnp.exp2(p[0] - m) for p in parts]
    l = functools.reduce(jnp.add, [wi * p[1] for wi, p in zip(w, parts)])
    acc = functools.reduce(jnp.add, [wi * p[2] for wi, p in zip(w, parts)])
    return m, l, acc


def _attn_paged_kernel(pt_ref, qabs_ref, qpe_ref, cnew_ref, pnew_ref, ckv_hbm, kpe_hbm, o_ref,
                       cbuf, pbuf, sem, m_ref, l_ref, acc_ref, *, ppc, n_groups, n_slots, n_heads, dec_seq, sub):
    c = pl.program_id(1)
    group = pl.program_id(0) * n_groups + c
    n_chunks = pl.num_programs(0) * n_groups * n_slots

    def copies(chunk, sl):
        base = chunk * ppc
        out = []
        for p in range(ppc):
            page = pt_ref[base + p]
            keys = pl.ds(p * PAGE_SIZE, PAGE_SIZE)
            out.append(pltpu.make_async_copy(ckv_hbm.at[page], cbuf.at[sl, keys, :], sem.at[0, sl]))
            out.append(pltpu.make_async_copy(kpe_hbm.at[page], pbuf.at[sl, :, keys], sem.at[1, sl]))
        return out

    def wait_slot(sl):
        pltpu.make_async_copy(cbuf.at[sl], cbuf.at[sl], sem.at[0, sl]).wait()
        pltpu.make_async_copy(pbuf.at[sl], pbuf.at[sl], sem.at[1, sl]).wait()

    @pl.when(group == 0)
    def _():
        for k in range(n_slots - 1):
            for cp in copies(k, k):
                cp.start()

    @pl.when(c == 0)
    def _():
        m_ref[...] = jnp.full(m_ref.shape, -jnp.inf, F32)
        l_ref[...] = jnp.zeros(l_ref.shape, F32)
        acc_ref[...] = jnp.zeros(acc_ref.shape, F32)

    qa = qabs_ref[0]
    qp = qpe_ref[0]
    n_sub = ppc * PAGE_SIZE // sub

    def slot_parts(sl):
        def scores(i):
            keys = slice(i * sub, (i + 1) * sub)
            kc = cbuf[sl, keys, :].astype(BF16)
            return _dot_nt(qa, kc) + _dot(qp, pbuf[sl, :, keys].astype(BF16)), kc

        ahead = [scores(i) for i in range(min(PAGED_SCORES_AHEAD, n_sub))]
        out = []
        for i in range(n_sub):
            s, kc = ahead.pop(0)
            if i + PAGED_SCORES_AHEAD < n_sub:
                ahead.append(scores(i + PAGED_SCORES_AHEAD))
            out.append(_softmax_part(s, kc))
        return out

    parts = [(m_ref[...], l_ref[...], acc_ref[...])]
    for k in range(n_slots):
        wait_slot(k)
        nxt = group * n_slots + (k + n_slots - 1)
        for cp in copies(jnp.where(nxt < n_chunks, nxt, 0), (k + n_slots - 1) % n_slots):
            cp.start()
        parts += slot_parts(k)
    m, l, acc = _softmax_merge(parts)
    m_ref[...] = m
    l_ref[...] = l
    acc_ref[...] = acc

    @pl.when(c == n_groups - 1)
    def _():
        kn = cnew_ref[0]
        s = _dot_nt(qa, kn) + _dot_nt(qp, pnew_ref[0])
        rows = s.shape[0]
        tok = lax.broadcasted_iota(jnp.int32, (rows, LANE), 0) // n_heads
        col = lax.broadcasted_iota(jnp.int32, (rows, LANE), 1)
        s = jnp.where((col <= tok) & (col < dec_seq), s, -jnp.inf)
        _, l2, acc2 = _softmax_merge([(m, l, acc), _softmax_part(s, kn)])
        o_ref[0] = acc2 * (1.0 / l2)

    @pl.when(group == pl.num_programs(0) * n_groups - 1)
    def _():
        for k in range(n_slots - 1):
            wait_slot(k)


def _attn_paged(page_table, qabs, qpe, cnew, pnew, cache_ckv, cache_kpe_t, n_heads, dec_seq):
    dec_b, n_pages = page_table.shape
    n_slots = PAGED_SLOTS
    assert n_pages % n_slots == 0
    ppc = min(PAGES_PER_CHUNK, n_pages // n_slots)
    sub = min(PAGED_KEY_BLOCK, ppc * PAGE_SIZE)
    assert n_pages % (n_slots * ppc) == 0 and cache_ckv.shape[1] == PAGE_SIZE and (ppc * PAGE_SIZE) % sub == 0
    n_groups = n_pages // (n_slots * ppc)
    rows, lora = qabs.shape[1], qabs.shape[2]
    rope = qpe.shape[2]
    kern = functools.partial(_attn_paged_kernel, ppc=ppc, n_groups=n_groups, n_slots=n_slots, n_heads=n_heads,
                             dec_seq=dec_seq, sub=sub)
    grid_spec = pltpu.PrefetchScalarGridSpec(
        num_scalar_prefetch=1,
        grid=(dec_b, n_groups),
        in_specs=[
            pl.BlockSpec((1, rows, lora), lambda b, c, pt: (b, 0, 0)),
            pl.BlockSpec((1, rows, rope), lambda b, c, pt: (b, 0, 0)),
            pl.BlockSpec((1, LANE, lora), lambda b, c, pt: (b, 0, 0)),
            pl.BlockSpec((1, LANE, rope), lambda b, c, pt: (b, 0, 0)),
            pl.BlockSpec(memory_space=pl.ANY),
            pl.BlockSpec(memory_space=pl.ANY),
        ],
        out_specs=pl.BlockSpec((1, rows, lora), lambda b, c, pt: (b, 0, 0)),
        scratch_shapes=[
            pltpu.VMEM((n_slots, ppc * PAGE_SIZE, lora), F32),
            pltpu.VMEM((n_slots, rope, ppc * PAGE_SIZE), F32),
            pltpu.SemaphoreType.DMA((2, n_slots)),
            pltpu.VMEM((rows, 1), F32),
            pltpu.VMEM((rows, 1), F32),
            pltpu.VMEM((rows, lora), F32),
        ],
    )
    return pl.pallas_call(
        kern,
        grid_spec=grid_spec,
        out_shape=jax.ShapeDtypeStruct((dec_b, rows, lora), F32),
        compiler_params=_cp(("arbitrary", "arbitrary"), VMEM_LIMIT),
        name="attn_paged",
    )(page_table.reshape(-1), qabs, qpe, cnew, pnew, cache_ckv, cache_kpe_t)


def _uv_sample_kernel(o_ref, w_ref, out_ref, *, n_heads, lora):
    for p in range(n_heads // 2):
        acc = _dot(o_ref[:, (2 * p) * lora:(2 * p + 1) * lora].astype(BF16), w_ref[2 * p])
        acc = acc + _dot(o_ref[:, (2 * p + 1) * lora:(2 * p + 2) * lora].astype(BF16), w_ref[2 * p + 1])
        out_ref[:, p * LANE:(p + 1) * LANE] = acc.astype(BF16)


def _uv_sample(o_lat, wuv_pad, n_heads, lora):
    ns = o_lat.shape[0]
    nv = n_heads * LANE // 2
    return pl.pallas_call(
        functools.partial(_uv_sample_kernel, n_heads=n_heads, lora=lora),
        grid=(1,),
        in_specs=[pl.BlockSpec(o_lat.shape, lambda i: (0, 0)), pl.BlockSpec(wuv_pad.shape, lambda i: (0, 0, 0))],
        out_specs=pl.BlockSpec((ns, nv), lambda i: (0, 0)),
        out_shape=jax.ShapeDtypeStruct((ns, nv), BF16),
        compiler_params=_cp(("arbitrary",), VMEM_LIMIT),
        name="uv_sample",
    )(o_lat, wuv_pad)


def _router_kernel(x_ref, w_ref, b_ref, o_ref, *, n_experts):
    logits = _dot(x_ref[...], w_ref[...]) + b_ref[...]
    lane = lax.broadcasted_iota(jnp.int32, logits.shape, 1).astype(F32)
    logits = jnp.where(lane < n_experts, logits, -jnp.inf)
    m1 = jnp.max(logits, axis=-1, keepdims=True)
    i1 = jnp.min(jnp.where(logits == m1, lane, float(LANE)), axis=-1, keepdims=True)
    rest = jnp.where(lane == i1, -jnp.inf, logits)
    m2 = jnp.max(rest, axis=-1, keepdims=True)
    i2 = jnp.min(jnp.where(rest == m2, lane, float(LANE)), axis=-1, keepdims=True)
    e2 = jnp.exp(m2 - m1)
    den = 1.0 + e2
    g1 = 1.0 / den
    g2 = e2 / den
    o_ref[...] = jnp.where(lane == 0, i1, jnp.where(lane == 1, i2, jnp.where(lane == 2, g1, jnp.where(lane == 3, g2, 0.0))))


def _router(x, w_pad, b_pad, n_experts):
    r, d = x.shape
    return pl.pallas_call(
        functools.partial(_router_kernel, n_experts=n_experts),
        grid=(r // ROW_TILE,),
        in_specs=[pl.BlockSpec((ROW_TILE, d), lambda i: (i, 0)), pl.BlockSpec((d, LANE), lambda i: (0, 0)),
                  pl.BlockSpec((1, LANE), lambda i: (0, 0))],
        out_specs=pl.BlockSpec((ROW_TILE, LANE), lambda i: (i, 0)),
        out_shape=jax.ShapeDtypeStruct((r, LANE), F32),
        compiler_params=_cp(("arbitrary",), VMEM_LIMIT),
        name="router",
    )(x, w_pad, b_pad)


def _moe_ffn_kernel(te_ref, nu_ref, src_ref, x_hbm, wg_ref, wu_ref, w2_ref, o_ref, xbuf, xb_ref, acc_ref, sem,
                    *, tm, rows_per_step):
    t = pl.program_id(0)
    f = pl.program_id(1)
    nf = pl.num_programs(1)
    used = t < nu_ref[0]
    slot = t % 2
    n_rows = rows_per_step * nf

    def row_copy(tile, r, sl):
        row = src_ref[tile * tm + r]
        return pltpu.make_async_copy(x_hbm.at[pl.ds(row * SUBLANES, SUBLANES), :],
                                     xbuf.at[sl, pl.ds(r * SUBLANES, SUBLANES), :], sem.at[sl])

    @pl.when((t == 0) & (f == 0))
    def _():
        def body(g, carry):
            for u in range(GATHER_UNROLL):
                row_copy(0, g * GATHER_UNROLL + u, 0).start()
            return carry
        lax.fori_loop(0, n_rows // GATHER_UNROLL, body, 0)

    @pl.when(f == 0)
    def _():
        pltpu.make_async_copy(xbuf.at[slot], xbuf.at[slot], sem.at[slot]).wait()
        for s, slab in enumerate(_load_row_tiles(xbuf, (slot,), tm)):
            xb_ref[:, s * LANE:(s + 1) * LANE] = slab.astype(BF16)

    def prefetch():
        for u in range(rows_per_step):
            row_copy(t + 1, f * rows_per_step + u, 1 - slot).start()

    @pl.when(used)
    def _():
        prefetch()
        x = xb_ref[...]
        hcol = (_silu(_dot(x, wg_ref[0])) * _dot(x, wu_ref[0])).astype(BF16)
        y = _dot(hcol, w2_ref[0])

        @pl.when(f == 0)
        def _():
            acc_ref[...] = y

        @pl.when(f > 0)
        def _():
            acc_ref[...] += y

    @pl.when(jnp.logical_not(used))
    def _():
        prefetch()

    @pl.when(f == nf - 1)
    def _():
        _store_row_tiles(o_ref, jnp.where(used, acc_ref[...], 0.0))

    @pl.when((t == pl.num_programs(0) - 1) & (f == nf - 1))
    def _():
        pltpu.make_async_copy(xbuf.at[1 - slot], xbuf.at[1 - slot], sem.at[1 - slot]).wait()


def _moe_ffn(tile_expert, n_used, src, x, w13, w2):
    d = w13.shape[1]
    assert x.shape[1] == LANE and d == SUBLANES * LANE
    n_exp, eff = w2.shape[0], w2.shape[1]
    tm, tf = MOE_ROW_TILE, min(MOE_FF_CHUNK, eff)
    n = src.shape[0] - 2 * tm
    assert n % tm == 0 and eff % tf == 0
    nf = eff // tf
    rows_per_step = -(-tm // (nf * GATHER_UNROLL)) * GATHER_UNROLL
    assert rows_per_step * nf <= 2 * tm

    def widx(base):
        def index_map(t, f, te, nu, src):
            live = t < nu[0]
            return (te[t], 0, jnp.where(live, f, nf - 1) + base)
        return index_map

    def w2idx(t, f, te, nu, src):
        return (te[t], jnp.where(t < nu[0], f, nf - 1), 0)

    grid_spec = pltpu.PrefetchScalarGridSpec(
        num_scalar_prefetch=3,
        grid=(n // tm, nf),
        in_specs=[
            pl.BlockSpec(memory_space=pl.ANY),
            pl.BlockSpec((1, d, tf), widx(0)),
            pl.BlockSpec((1, d, tf), widx(nf)),
            pl.BlockSpec((1, tf, d), w2idx),
        ],
        out_specs=pl.BlockSpec((tm * SUBLANES, LANE), lambda t, f, te, nu, src: (t, 0)),
        scratch_shapes=[
            pltpu.VMEM((2, rows_per_step * nf * SUBLANES, LANE), F32),
            pltpu.VMEM((tm, d), BF16),
            pltpu.VMEM((tm, d), F32),
            pltpu.SemaphoreType.DMA((2,)),
        ],
    )
    return pl.pallas_call(
        functools.partial(_moe_ffn_kernel, tm=tm, rows_per_step=rows_per_step),
        grid_spec=grid_spec,
        out_shape=jax.ShapeDtypeStruct((n * SUBLANES, LANE), F32),
        compiler_params=_cp(("arbitrary", "arbitrary"), VMEM_LIMIT),
        name="moe_ffn",
    )(tile_expert, n_used, src, x, w13, w13, w2)


def _combine_ln_kernel(p1_ref, p2_ref, y_hbm, h_ref, route_ref, g_ref, b_ref, o_ref, buf, sem, *, tile, alpha):
    i = pl.program_id(0)
    n = pl.num_programs(0)
    slot = i % 2

    def issue(t, sl):
        def body(g, carry):
            for u in range(GATHER_UNROLL):
                r = g * GATHER_UNROLL + u
                a = p1_ref[t * tile + r]
                b = p2_ref[t * tile + r]
                dst = pl.ds(r * SUBLANES, SUBLANES)
                pltpu.make_async_copy(y_hbm.at[pl.ds(a * SUBLANES, SUBLANES), :], buf.at[sl, 0, dst, :],
                                      sem.at[sl]).start()
                pltpu.make_async_copy(y_hbm.at[pl.ds(b * SUBLANES, SUBLANES), :], buf.at[sl, 1, dst, :],
                                      sem.at[sl]).start()
            return carry
        lax.fori_loop(0, tile // GATHER_UNROLL, body, 0)

    @pl.when(i == 0)
    def _():
        issue(i, slot)

    @pl.when(i + 1 < n)
    def _():
        issue(i + 1, 1 - slot)

    pltpu.make_async_copy(buf.at[slot], buf.at[slot], sem.at[slot]).wait()
    route = route_ref[...]
    g1, g2 = route[:, 2:3], route[:, 3:4]
    ya = _load_row_tiles(buf, (slot, 0), tile)
    yb = _load_row_tiles(buf, (slot, 1), tile)
    z = jnp.concatenate([alpha * h_ref[:, s * LANE:(s + 1) * LANE] + (g1 * ya[s] + g2 * yb[s])
                         for s in range(SUBLANES)], axis=1)
    o_ref[...] = _layer_norm(z, g_ref[...], b_ref[...])


def _combine_ln(pos1, pos2, y_sorted, h, route, g, b, alpha):
    r, d = h.shape
    tile = GATHER_TILE
    assert r % tile == 0 and tile % GATHER_UNROLL == 0 and d == SUBLANES * LANE
    grid_spec = pltpu.PrefetchScalarGridSpec(
        num_scalar_prefetch=2,
        grid=(r // tile,),
        in_specs=[
            pl.BlockSpec(memory_space=pl.ANY),
            pl.BlockSpec((tile, d), lambda i, p1, p2: (i, 0)),
            pl.BlockSpec((tile, LANE), lambda i, p1, p2: (i, 0)),
            pl.BlockSpec((1, d), lambda i, p1, p2: (0, 0)),
            pl.BlockSpec((1, d), lambda i, p1, p2: (0, 0)),
        ],
        out_specs=pl.BlockSpec((tile, d), lambda i, p1, p2: (i, 0)),
        scratch_shapes=[pltpu.VMEM((2, 2, tile * SUBLANES, LANE), F32), pltpu.SemaphoreType.DMA((2,))],
    )
    return pl.pallas_call(
        functools.partial(_combine_ln_kernel, tile=tile, alpha=alpha),
        grid_spec=grid_spec,
        out_shape=jax.ShapeDtypeStruct((r, d), F32),
        compiler_params=_cp(("arbitrary",), VMEM_LIMIT),
        name="combine_ln",
    )(pos1, pos2, y_sorted, h, route, g, b)


def _route_plan(route, n_experts, tm):
    r = route.shape[0]
    e = jnp.concatenate([route[:, 0], route[:, 1]]).astype(jnp.int32)
    onehot = (e[:, None] == jnp.arange(n_experts, dtype=jnp.int32)[None, :]).astype(jnp.int32)
    csum = jnp.cumsum(onehot, axis=0)
    rank = jnp.take_along_axis(csum, e[:, None], axis=1)[:, 0] - 1
    tiles = (csum[-1] + tm - 1) // tm
    tile_end = jnp.cumsum(tiles)
    pos = (tile_end - tiles)[e] * tm + rank
    n_tiles = (TOP_K * r) // tm + n_experts
    tok = jnp.arange(TOP_K * r, dtype=jnp.int32) % r
    src = jnp.zeros(((n_tiles + 2) * tm,), jnp.int32).at[pos].set(tok, unique_indices=True)
    n_used = tile_end[-1]
    tile_id = jnp.minimum(jnp.arange(n_tiles, dtype=jnp.int32), n_used - 1)
    tile_expert = jnp.sum((tile_end[None, :] <= tile_id[:, None]).astype(jnp.int32), axis=1)
    return src, tile_expert, n_used.reshape(1).astype(jnp.int32), pos[:r], pos[r:]


def _pad_cols(w, offset, width):
    return jnp.pad(w, ((0, 0), (offset, width - offset - w.shape[1])))


def kernel(x_prompt, x_sample, state_ret, cache_ckv, cache_kpe, page_table, meta_tokens, ln_g, ln_b, ret_w_in, ret_gn_g, ret_w_o, mla_w_dq, mla_q_norm, mla_w_uq, mla_w_o, kv_w_a, kv_norm, kv_w_b, ffn_w13, ffn_w2, moe_w_r, moe_b_r, moe_w13, moe_w2):
    bsz, seq, d = x_prompt.shape
    dec_b, dec_seq, _ = x_sample.shape
    n_ret, _, ret_h, dk, dv = state_ret.shape
    depth = ln_g.shape[0]
    assert depth == 2 and n_ret == 1 and mla_w_dq.shape[0] == 1, "layer pattern: one retention layer then one MLA layer"
    past_len = page_table.shape[1] * PAGE_SIZE
    lora, mla_h, nope_v = kv_w_b.shape
    rope = kv_w_a.shape[1] - lora
    nope = mla_w_uq.shape[2] // mla_h - rope
    v_dim = nope_v - nope
    n_experts = moe_w_r.shape[2]
    assert nope == LANE // 2 and v_dim == LANE // 2 and rope <= LANE // 4
    alpha = (2 * depth) ** 0.25
    mla_scale = (nope + rope) ** -0.5
    half = rope // 2

    n_p, n_s = bsz * seq, dec_b * dec_seq
    meta_row0 = n_p + n_s
    r_real = meta_row0 + N_META
    r_pad = -(-r_real // ROW_TILE) * ROW_TILE
    assert r_pad - meta_row0 >= LANE

    x0 = jnp.concatenate([x_prompt.reshape(n_p, d), x_sample.reshape(n_s, d), meta_tokens,
                          jnp.zeros((r_pad - r_real, d), F32)], axis=0)
    pos = jnp.concatenate([jnp.tile(N_META + jnp.arange(seq), bsz), jnp.tile(past_len + jnp.arange(dec_seq), dec_b),
                           jnp.arange(N_META), jnp.zeros((r_pad - r_real,), jnp.int32)]).astype(F32)

    inv_r = ROPE_BASE ** (-jnp.arange(dk // 2, dtype=F32) / (dk // 2))
    ang_r = pos[:, None] * inv_r[None, :]
    cos_r, sin_r = jnp.cos(ang_r), jnp.sin(ang_r)
    inv_m = ROPE_BASE ** (-jnp.arange(half, dtype=F32) / half)
    ang_m = pos[:, None] * inv_m[None, :]
    cos_m = jnp.concatenate([jnp.cos(ang_m)] * 2, axis=1)
    sin_m = jnp.concatenate([jnp.sin(ang_m)] * 2, axis=1)
    cos_a, sin_a = _pad_cols(cos_m, 0, LANE), _pad_cols(sin_m, 0, LANE)
    cos_b, sin_b = _pad_cols(cos_m, nope, LANE), _pad_cols(sin_m, nope, LANE)
    q_scale = mla_scale * LOG2E
    q_ctab = (_pad_cols(jnp.ones((r_pad, nope), F32), 0, LANE) + cos_b) * q_scale
    q_stab = sin_b * q_scale

    def rot(w):
        return jnp.concatenate([-w[..., half:], w[..., :half]], axis=-1)

    qkvg = _ret_proj(x0, ret_w_in[0], cos_r, sin_r, ret_h, dk)
    gn = ret_gn_g[0].reshape(1, -1)
    mix_p, mix_meta, st_p = _ret_prompt(qkvg, gn, _decay_tables(ret_h, dk, RET_CHUNK, RET_CHUNK),
                                        bsz, seq, meta_row0, ret_h, dk, dv)
    mix_s, st_s = _ret_sample(qkvg, state_ret[0], gn,
                              _decay_tables(ret_h, dk, SAMPLE_BATCH_TILE * dec_seq, dec_seq),
                              n_p, dec_b, dec_seq, ret_h, dk, dv)
    mix = jnp.concatenate([mix_p, mix_s, mix_meta, jnp.zeros((r_pad - r_real, mix_p.shape[1]), BF16)], axis=0)
    row = lambda v: v.reshape(1, -1)
    h1, h1b = _proj_ln(mix, ret_w_o[0], x0, row(ln_g[0, 0]), row(ln_b[0, 0]), alpha)
    h2, h2b = _ffn_ln(h1b, ffn_w13[0].astype(BF16), ffn_w2[0].astype(BF16), h1, row(ln_g[0, 1]), row(ln_b[0, 1]), alpha)

    wc, wpe = kv_w_a[:, :lora], kv_w_a[:, lora:]
    wkv = jnp.concatenate([wc, _pad_cols(wpe, 0, LANE), _pad_cols(rot(wpe), 0, LANE),
                           _pad_cols(wpe, nope, LANE), _pad_cols(rot(wpe), nope, LANE)], axis=1).astype(BF16)
    wuk = jnp.pad(kv_w_b[:, :, :nope], ((0, 0), (0, 0), (0, LANE - nope))).reshape(lora, mla_h * LANE).astype(BF16)
    even = (jnp.arange(mla_h) % 2 == 0)[None, :, None]
    wuv_h = kv_w_b[:, :, nope:]
    wuv = jnp.where(even, jnp.pad(wuv_h, ((0, 0), (0, 0), (0, LANE - v_dim))),
                    jnp.pad(wuv_h, ((0, 0), (0, 0), (LANE - v_dim, 0)))).reshape(lora, mla_h * LANE).astype(BF16)
    lane_id = jnp.arange(LANE)[None, None, :]
    ones_row = jnp.where(even, lane_id == v_dim, lane_id == 0).astype(F32).reshape(1, mla_h * LANE)
    ckv, kpe, k_heads, v_heads = _kv_proj(h2b, wkv, row(kv_norm), (cos_a, sin_a, cos_b, sin_b), wuk, wuv, ones_row,
                                          lora, rope, mla_h)

    wuq = mla_w_uq[0].reshape(-1, mla_h, nope + rope)
    wq_n, wq_r = wuq[:, :, :nope], wuq[:, :, nope:]
    pad_h = lambda w, off: jnp.pad(w, ((0, 0), (0, 0), (off, LANE - off - w.shape[2]))).reshape(w.shape[0], mla_h * LANE)
    w_q1 = (pad_h(wq_n, 0) + pad_h(wq_r, nope)).astype(BF16)
    w_q2 = pad_h(rot(wq_r), nope).astype(BF16)
    wdq = mla_w_dq[0].astype(BF16)
    qn_g = row(mla_q_norm[0])
    q_heads = _q_proj(h2b, wdq, qn_g, w_q1, w_q2, q_ctab, q_stab, mla_h)
    attn_p = _attn_prompt(q_heads, k_heads, v_heads, bsz, seq, meta_row0, mla_h)

    wukt = jnp.pad(jnp.transpose(kv_w_b[:, :, :nope], (1, 2, 0)), ((0, 0), (0, LANE - nope), (0, 0))).astype(BF16)
    w_pe = wq_r.reshape(-1, mla_h * rope).astype(BF16)
    w_pe_rot = rot(wq_r).reshape(-1, mla_h * rope).astype(BF16)
    cs = jnp.tile(cos_m[n_p:n_p + n_s], (1, mla_h))
    ss = jnp.tile(sin_m[n_p:n_p + n_s], (1, mla_h))
    qabs, qpe = _q_sample(h2b[n_p:n_p + n_s], wdq, qn_g, pad_h(wq_n, 0).astype(BF16), wukt, w_pe, w_pe_rot, cs, ss,
                          mla_h, lora, q_scale)
    ckv_s, kpe_s = ckv[n_p:n_p + n_s], kpe[n_p:n_p + n_s]
    new_pad = lambda a: jnp.pad(a.reshape(dec_b, dec_seq, -1), ((0, 0), (0, LANE - dec_seq), (0, 0))).astype(BF16)
    o_lat = _attn_paged(page_table, qabs.reshape(dec_b, dec_seq * mla_h, lora), qpe.reshape(dec_b, dec_seq * mla_h, rope),
                        new_pad(ckv_s), new_pad(kpe_s), cache_ckv, jnp.swapaxes(cache_kpe, 1, 2), mla_h, dec_seq)
    wuv_t = jnp.transpose(wuv_h, (1, 0, 2))
    wuv_pad = jnp.where(jnp.transpose(even, (1, 0, 2)), jnp.pad(wuv_t, ((0, 0), (0, 0), (0, LANE - v_dim))),
                        jnp.pad(wuv_t, ((0, 0), (0, 0), (LANE - v_dim, 0)))).astype(BF16)
    attn_s = _uv_sample(o_lat.reshape(n_s, mla_h * lora), wuv_pad, mla_h, lora)
    attn = jnp.concatenate([attn_p, attn_s, jnp.zeros((r_pad - meta_row0, attn_p.shape[1]), BF16)], axis=0)
    h3, h3b, h3_tiles = _proj_ln(attn, mla_w_o[0], h2, row(ln_g[1, 0]), row(ln_b[1, 0]), alpha, row_tiles=True)

    w_r = _pad_cols(moe_w_r[0], 0, LANE).astype(BF16)
    b_r = _pad_cols(moe_b_r[0].reshape(1, -1), 0, LANE)
    route = _router(h3b, w_r, b_r, n_experts)
    src, tile_expert, n_used, pos1, pos2 = _route_plan(route, n_experts, MOE_ROW_TILE)
    y_sorted = _moe_ffn(tile_expert, n_used, src, h3_tiles, moe_w13[0].astype(BF16), moe_w2[0].astype(BF16))
    h4 = _combine_ln(pos1, pos2, y_sorted, h3, route, row(ln_g[1, 1]), row(ln_b[1, 1]), alpha)

    y_prompt = h4[:n_p].reshape(bsz, seq, d)
    y_sample = h4[n_p:n_p + n_s].reshape(dec_b, dec_seq, d)

    def with_meta(a):
        meta = jnp.broadcast_to(a[meta_row0:r_real][None], (bsz, N_META, a.shape[1]))
        return jnp.concatenate([meta, a[:n_p].reshape(bsz, seq, -1)], axis=1)

    return (y_prompt, y_sample, st_p[None], with_meta(ckv), with_meta(kpe), st_s[None],
            ckv_s.reshape(dec_b, dec_seq, lora), kpe_s.reshape(dec_b, dec_seq, rope))
```

```python
import functools
import math

import jax
import jax.numpy as jnp
from jax import lax
from jax.experimental import pallas as pl
from jax.experimental.pallas import tpu as pltpu

F32 = jnp.float32
BF16 = jnp.bfloat16

N_META = 16
TOP_K = 2
ROPE_BASE = 10000.0
LN_EPS = 1e-5
RMS_EPS = 1e-6
PAGE_SIZE = 128
RET_CHUNK = 128

LANE = 128
SUBLANES = 8
ROW_TILE = 512
RET_PROJ_TILE = 1024
ATTN_TQ = 1024
ATTN_TK = 512
FFN_CHUNK = 256
MOE_ROW_TILE = 512
MOE_FF_CHUNK = 1792
GATHER_TILE = 256
GATHER_UNROLL = 8
PAGES_PER_CHUNK = 32
PAGED_SLOTS = 4
PAGED_KEY_BLOCK = 1024
PAGED_SCORES_AHEAD = 8
LOG2E = math.log2(math.e)
SAMPLE_BATCH_TILE = 4
VMEM_LIMIT = 56 * 1024 * 1024


def _cp(sem, vmem=None):
    return pltpu.CompilerParams(dimension_semantics=sem, vmem_limit_bytes=vmem)


def _layer_norm(z, g, b):
    mu = jnp.mean(z, axis=-1, keepdims=True)
    zc = z - mu
    var = jnp.mean(zc * zc, axis=-1, keepdims=True)
    return zc * lax.rsqrt(var + LN_EPS) * g + b


def _rms_norm(c, g):
    return c * lax.rsqrt(jnp.mean(c * c, axis=-1, keepdims=True) + RMS_EPS) * g


def _silu(x):
    return x * (1.0 / (1.0 + jnp.exp(-x)))


def _dot(a, b):
    return jnp.dot(a, b, preferred_element_type=F32)


def _dot_nt(a, b):
    return lax.dot_general(a, b, (((1,), (1,)), ((), ())), preferred_element_type=F32)


def _dot_tn(a, b):
    return lax.dot_general(a, b, (((0,), (0,)), ((), ())), preferred_element_type=F32)


def _ret_proj_kernel(x_ref, w_ref, cos_ref, sin_ref, o_ref, wbf_ref, *, n_heads, dk, k_scale):
    j = pl.program_id(0)
    i = pl.program_id(1)

    @pl.when(i == 0)
    def _():
        wbf_ref[...] = w_ref[...].astype(BF16)

    y = _dot(x_ref[...].astype(BF16), wbf_ref[...])
    half = dk // 2

    @pl.when(j < 2)
    def _():
        c = cos_ref[...]
        s = sin_ref[...]
        scale = jnp.where(j == 1, k_scale, 1.0).astype(F32)
        for h in range(n_heads):
            x1 = y[:, h * dk:h * dk + half]
            x2 = y[:, h * dk + half:(h + 1) * dk]
            o_ref[:, h * dk:h * dk + half] = ((x1 * c - x2 * s) * scale).astype(BF16)
            o_ref[:, h * dk + half:(h + 1) * dk] = ((x1 * s + x2 * c) * scale).astype(BF16)

    @pl.when(j >= 2)
    def _():
        o_ref[...] = y.astype(BF16)


def _ret_proj(x, w_in, cos_r, sin_r, n_heads, dk):
    r, d = x.shape
    n = w_in.shape[1]
    qk = n_heads * dk
    assert n % qk == 0 and dk // 2 == LANE
    kern = functools.partial(_ret_proj_kernel, n_heads=n_heads, dk=dk, k_scale=dk ** -0.5)
    tile = RET_PROJ_TILE if r % RET_PROJ_TILE == 0 else ROW_TILE
    return pl.pallas_call(
        kern,
        grid=(n // qk, r // tile),
        in_specs=[
            pl.BlockSpec((tile, d), lambda j, i: (i, 0)),
            pl.BlockSpec((d, qk), lambda j, i: (0, j)),
            pl.BlockSpec((tile, LANE), lambda j, i: (i, 0)),
            pl.BlockSpec((tile, LANE), lambda j, i: (i, 0)),
        ],
        out_specs=pl.BlockSpec((tile, qk), lambda j, i: (i, j)),
        out_shape=jax.ShapeDtypeStruct((r, n), BF16),
        scratch_shapes=[pltpu.VMEM((d, qk), BF16)],
        compiler_params=_cp(("arbitrary", "arbitrary"), VMEM_LIMIT),
        name="ret_proj",
    )(x, w_in, cos_r, sin_r)


def _group_norm_gate(o, g, gn):
    mu = jnp.mean(o, axis=-1, keepdims=True)
    oc = o - mu
    var = jnp.mean(oc * oc, axis=-1, keepdims=True)
    return (_silu(g.astype(F32)) * (oc * lax.rsqrt(var + LN_EPS) * gn)).astype(BF16)


def _ret_prompt_kernel(q_ref, k_ref, v_ref, g_ref, qm_ref, km_ref, vm_ref, gm_ref, gn_ref,
                       dec_ref, rdec_ref, cdec_ref, cpow_ref,
                       o_ref, om_ref, st_ref, s_ref, *, n_heads, dk, dv, chunk, n_meta, tile):
    t = pl.program_id(1)
    nt = pl.num_programs(1)

    @pl.when(t == 0)
    def _():
        for h in range(n_heads):
            q = qm_ref[:, h * dk:(h + 1) * dk]
            k = km_ref[:, h * dk:(h + 1) * dk]
            v = vm_ref[:, h * dv:(h + 1) * dv]
            sc = _dot_nt(q, k) * dec_ref[h, :n_meta, :n_meta]
            o = _dot(sc.astype(BF16), v)
            om_ref[:, h * dv:(h + 1) * dv] = _group_norm_gate(
                o, gm_ref[:, h * dv:(h + 1) * dv], gn_ref[:, h * dv:(h + 1) * dv])
            k_out = (k.astype(F32) * cdec_ref[h, chunk - n_meta:, :]).astype(BF16)
            s_ref[h] = _dot_tn(k_out, v)

    for h in range(n_heads):
        for c in range(tile // chunk):
            rows = slice(c * chunk, (c + 1) * chunk)
            q = q_ref[rows, h * dk:(h + 1) * dk]
            k = k_ref[rows, h * dk:(h + 1) * dk]
            v = v_ref[rows, h * dv:(h + 1) * dv]
            s_old = s_ref[h]
            sc = _dot_nt(q, k) * dec_ref[h]
            q_in = (q.astype(F32) * rdec_ref[h]).astype(BF16)
            o = _dot(sc.astype(BF16), v) + _dot(q_in, s_old.astype(BF16))
            k_out = (k.astype(F32) * cdec_ref[h]).astype(BF16)
            s_ref[h] = cpow_ref[h] * s_old + _dot_tn(k_out, v)
            o_ref[rows, h * dv:(h + 1) * dv] = _group_norm_gate(
                o, g_ref[rows, h * dv:(h + 1) * dv], gn_ref[:, h * dv:(h + 1) * dv])

    @pl.when(t == nt - 1)
    def _():
        st_ref[0] = s_ref[...]


def _ret_prompt(qkvg, gn_g, tabs, bsz, seq, meta_row0, n_heads, dk, dv):
    qk, vw = n_heads * dk, n_heads * dv
    tile = ROW_TILE
    assert seq % tile == 0 and tile % RET_CHUNK == 0 and meta_row0 % N_META == 0 and vw == 2 * qk
    nt = seq // tile
    mb = meta_row0 // N_META
    dec, rdec, cdec, cpow = tabs
    kern = functools.partial(_ret_prompt_kernel, n_heads=n_heads, dk=dk, dv=dv, chunk=RET_CHUNK,
                             n_meta=N_META, tile=tile)
    full = lambda *shape: pl.BlockSpec(shape, lambda b, t: (0,) * len(shape))
    return pl.pallas_call(
        kern,
        grid=(bsz, nt),
        in_specs=[
            pl.BlockSpec((tile, qk), lambda b, t: (b * nt + t, 0)),
            pl.BlockSpec((tile, qk), lambda b, t: (b * nt + t, 1)),
            pl.BlockSpec((tile, vw), lambda b, t: (b * nt + t, 1)),
            pl.BlockSpec((tile, vw), lambda b, t: (b * nt + t, 2)),
            pl.BlockSpec((N_META, qk), lambda b, t: (mb, 0)),
            pl.BlockSpec((N_META, qk), lambda b, t: (mb, 1)),
            pl.BlockSpec((N_META, vw), lambda b, t: (mb, 1)),
            pl.BlockSpec((N_META, vw), lambda b, t: (mb, 2)),
            full(1, vw),
            full(n_heads, RET_CHUNK, RET_CHUNK),
            full(n_heads, RET_CHUNK, dk),
            full(n_heads, RET_CHUNK, dk),
            full(n_heads, 1, 1),
        ],
        out_specs=[
            pl.BlockSpec((tile, vw), lambda b, t: (b * nt + t, 0)),
            pl.BlockSpec((N_META, vw), lambda b, t: (0, 0)),
            pl.BlockSpec((1, n_heads, dk, dv), lambda b, t: (b, 0, 0, 0)),
        ],
        out_shape=[
            jax.ShapeDtypeStruct((bsz * seq, vw), BF16),
            jax.ShapeDtypeStruct((N_META, vw), BF16),
            jax.ShapeDtypeStruct((bsz, n_heads, dk, dv), F32),
        ],
        scratch_shapes=[pltpu.VMEM((n_heads, dk, dv), F32)],
        compiler_params=_cp(("arbitrary", "arbitrary"), VMEM_LIMIT),
        name="ret_prompt",
    )(qkvg, qkvg, qkvg, qkvg, qkvg, qkvg, qkvg, qkvg, gn_g, dec, rdec, cdec, cpow)


def _ret_sample_kernel(q_ref, k_ref, v_ref, g_ref, s0_ref, gn_ref, dec_ref, rdec_ref, cdec_ref, cpow_ref,
                       o_ref, st_ref, *, n_heads, dk, dv, nb, dec_seq):
    rows = nb * dec_seq
    row_b = lax.broadcasted_iota(jnp.int32, (rows, 1), 0) // dec_seq
    for h in range(n_heads):
        q = q_ref[:, h * dk:(h + 1) * dk]
        k = k_ref[:, h * dk:(h + 1) * dk]
        v = v_ref[:, h * dv:(h + 1) * dv]
        sc = _dot_nt(q, k) * dec_ref[h]
        o = _dot(sc.astype(BF16), v)
        q_in = q.astype(F32) * rdec_ref[h]
        k_out = k.astype(F32) * cdec_ref[h]
        for b in range(nb):
            s_old = s0_ref[0, b, h]
            sel = row_b == b
            o = o + _dot(jnp.where(sel, q_in, 0.0).astype(BF16), s_old.astype(BF16))
            st_ref[0, b, h] = cpow_ref[h] * s_old + _dot_tn(jnp.where(sel, k_out, 0.0).astype(BF16), v)
        o_ref[:, h * dv:(h + 1) * dv] = _group_norm_gate(
            o, g_ref[:, h * dv:(h + 1) * dv], gn_ref[:, h * dv:(h + 1) * dv])


def _ret_sample(qkvg, state, gn_g, tabs, row0, dec_b, dec_seq, n_heads, dk, dv):
    qk, vw = n_heads * dk, n_heads * dv
    nb = SAMPLE_BATCH_TILE
    rows = nb * dec_seq
    assert dec_b % nb == 0 and row0 % rows == 0 and rows % 16 == 0
    rb = row0 // rows
    dec, rdec, cdec, cpow = tabs
    state5 = state.reshape(dec_b // nb, nb, n_heads, dk, dv)
    kern = functools.partial(_ret_sample_kernel, n_heads=n_heads, dk=dk, dv=dv, nb=nb, dec_seq=dec_seq)
    full = lambda *shape: pl.BlockSpec(shape, lambda i: (0,) * len(shape))
    out, st = pl.pallas_call(
        kern,
        grid=(dec_b // nb,),
        in_specs=[
            pl.BlockSpec((rows, qk), lambda i: (rb + i, 0)),
            pl.BlockSpec((rows, qk), lambda i: (rb + i, 1)),
            pl.BlockSpec((rows, vw), lambda i: (rb + i, 1)),
            pl.BlockSpec((rows, vw), lambda i: (rb + i, 2)),
            pl.BlockSpec((1, nb, n_heads, dk, dv), lambda i: (i, 0, 0, 0, 0)),
            full(1, vw),
            full(n_heads, rows, rows),
            full(n_heads, rows, dk),
            full(n_heads, rows, dk),
            full(n_heads, 1, 1),
        ],
        out_specs=[
            pl.BlockSpec((rows, vw), lambda i: (i, 0)),
            pl.BlockSpec((1, nb, n_heads, dk, dv), lambda i: (i, 0, 0, 0, 0)),
        ],
        out_shape=[
            jax.ShapeDtypeStruct((dec_b * dec_seq, vw), BF16),
            jax.ShapeDtypeStruct(state5.shape, F32),
        ],
        compiler_params=_cp(("arbitrary",), VMEM_LIMIT),
        name="ret_sample",
    )(qkvg, qkvg, qkvg, qkvg, state5, gn_g, dec, rdec, cdec, cpow)
    return out, st.reshape(dec_b, n_heads, dk, dv)


def _decay_tables(n_heads, dk, chunk, group):
    lg = jnp.log1p(-jnp.exp2(-5.0 - jnp.arange(n_heads, dtype=F32)))
    r = jnp.arange(chunk)
    t = (r % group).astype(F32)
    diff = t[:, None] - t[None, :]
    same = (r[:, None] // group) == (r[None, :] // group)
    dec = jnp.where(same & (diff >= 0), jnp.exp(jnp.maximum(diff, 0.0)[None] * lg[:, None, None]), 0.0)
    rdec = jnp.exp((t[None, :] + 1.0) * lg[:, None])
    cdec = jnp.exp((group - 1.0 - t)[None, :] * lg[:, None])
    cpow = jnp.exp(group * lg)
    bc = lambda a: jnp.broadcast_to(a[:, :, None], (n_heads, chunk, dk))
    return dec, bc(rdec), bc(cdec), cpow.reshape(n_heads, 1, 1)


def _store_row_tiles(ref, y):
    rows = y.shape[0]
    for s in range(SUBLANES):
        ref[pl.ds(s, rows, stride=SUBLANES), :] = y[:, s * LANE:(s + 1) * LANE]


def _load_row_tiles(ref, lead, rows):
    return [ref[lead + (pl.ds(s, rows, stride=SUBLANES), slice(None))] for s in range(SUBLANES)]


def _proj_ln_kernel(a_ref, w_ref, h_ref, g_ref, b_ref, o_ref, obf_ref, *rest, alpha, row_tiles):
    wbf_ref = rest[-1]

    @pl.when(pl.program_id(0) == 0)
    def _():
        wbf_ref[...] = w_ref[...].astype(BF16)

    z = alpha * h_ref[...] + _dot(a_ref[...], wbf_ref[...])
    y = _layer_norm(z, g_ref[...], b_ref[...])
    o_ref[...] = y
    obf_ref[...] = y.astype(BF16)
    if row_tiles:
        _store_row_tiles(rest[0], y)


def _proj_ln(a, w, h, g, b, alpha, row_tiles=False):
    r, k = a.shape
    d = w.shape[1]
    assert not row_tiles or d == SUBLANES * LANE
    row = lambda n: pl.BlockSpec((ROW_TILE, n), lambda i: (i, 0))
    full = lambda *shape: pl.BlockSpec(shape, lambda i: (0,) * len(shape))
    out_specs = [row(d), row(d)]
    out_shape = [jax.ShapeDtypeStruct((r, d), F32), jax.ShapeDtypeStruct((r, d), BF16)]
    if row_tiles:
        out_specs.append(pl.BlockSpec((ROW_TILE * SUBLANES, LANE), lambda i: (i, 0)))
        out_shape.append(jax.ShapeDtypeStruct((r * SUBLANES, LANE), F32))
    return pl.pallas_call(
        functools.partial(_proj_ln_kernel, alpha=alpha, row_tiles=row_tiles),
        grid=(r // ROW_TILE,),
        in_specs=[row(k), full(k, d), row(d), full(1, d), full(1, d)],
        out_specs=out_specs,
        out_shape=out_shape,
        scratch_shapes=[pltpu.VMEM((k, d), BF16)],
        compiler_params=_cp(("arbitrary",), VMEM_LIMIT),
        name="proj_ln",
    )(a, w, h, g, b)


def _ffn_ln_kernel(x_ref, w13_ref, w2_ref, h_ref, g_ref, b_ref, o_ref, obf_ref, *, alpha, d_ff, chunk):
    x = x_ref[...]
    acc = alpha * h_ref[...]
    for c in range(d_ff // chunk):
        gate = _dot(x, w13_ref[:, c * chunk:(c + 1) * chunk])
        up = _dot(x, w13_ref[:, d_ff + c * chunk:d_ff + (c + 1) * chunk])
        acc = acc + _dot((_silu(gate) * up).astype(BF16), w2_ref[c * chunk:(c + 1) * chunk, :])
    y = _layer_norm(acc, g_ref[...], b_ref[...])
    o_ref[...] = y
    obf_ref[...] = y.astype(BF16)


def _ffn_ln(x, w13, w2, h, g, b, alpha):
    r, d = x.shape
    d_ff = w2.shape[0]
    assert d_ff % FFN_CHUNK == 0
    row = lambda n: pl.BlockSpec((ROW_TILE, n), lambda i: (i, 0))
    full = lambda *shape: pl.BlockSpec(shape, lambda i: (0,) * len(shape))
    return pl.pallas_call(
        functools.partial(_ffn_ln_kernel, alpha=alpha, d_ff=d_ff, chunk=FFN_CHUNK),
        grid=(r // ROW_TILE,),
        in_specs=[row(d), full(d, 2 * d_ff), full(d_ff, d), row(d), full(1, d), full(1, d)],
        out_specs=[row(d), row(d)],
        out_shape=[jax.ShapeDtypeStruct((r, d), F32), jax.ShapeDtypeStruct((r, d), BF16)],
        compiler_params=_cp(("arbitrary",), VMEM_LIMIT),
        name="ffn_ln",
    )(x, w13, w2, h, g, b)


def _kv_kernel(x_ref, wkv_ref, g_ref, ca_ref, sa_ref, cb_ref, sb_ref, wuk_ref, wuv_ref, ones_ref,
               ckv_ref, kpe_ref, k_ref, v_ref, *, lora, rope, n_heads):
    y = _dot(x_ref[...], wkv_ref[...])
    cn = _rms_norm(y[:, :lora], g_ref[...])
    ckv_ref[...] = cn
    kpe_a = y[:, lora:lora + LANE] * ca_ref[...] + y[:, lora + LANE:lora + 2 * LANE] * sa_ref[...]
    kpe_ref[...] = kpe_a[:, :rope]
    kpe_b = y[:, lora + 2 * LANE:lora + 3 * LANE] * cb_ref[...] + y[:, lora + 3 * LANE:lora + 4 * LANE] * sb_ref[...]
    cb = cn.astype(BF16)
    kn = _dot(cb, wuk_ref[...])
    for h in range(n_heads):
        k_ref[:, h * LANE:(h + 1) * LANE] = (kn[:, h * LANE:(h + 1) * LANE] + kpe_b).astype(BF16)
    v_ref[...] = (_dot(cb, wuv_ref[...]) + ones_ref[...]).astype(BF16)


def _kv_proj(x, wkv, g, tabs, wuk, wuv, ones_row, lora, rope, n_heads):
    r, d = x.shape
    nv = wuv.shape[1]
    row = lambda n: pl.BlockSpec((ROW_TILE, n), lambda i: (i, 0))
    full = lambda *shape: pl.BlockSpec(shape, lambda i: (0,) * len(shape))
    return pl.pallas_call(
        functools.partial(_kv_kernel, lora=lora, rope=rope, n_heads=n_heads),
        grid=(r // ROW_TILE,),
        in_specs=[row(d), full(*wkv.shape), full(1, lora), row(LANE), row(LANE), row(LANE), row(LANE),
                  full(*wuk.shape), full(*wuv.shape), full(1, nv)],
        out_specs=[row(lora), row(rope), row(n_heads * LANE), row(nv)],
        out_shape=[jax.ShapeDtypeStruct((r, lora), F32), jax.ShapeDtypeStruct((r, rope), F32),
                   jax.ShapeDtypeStruct((r, n_heads * LANE), BF16), jax.ShapeDtypeStruct((r, nv), BF16)],
        compiler_params=_cp(("arbitrary",), VMEM_LIMIT),
        name="kv_proj",
    )(x, wkv, g, *tabs, wuk, wuv, ones_row)


def _q_kernel(x_ref, wdq_ref, g_ref, w1_ref, w2_ref, c_ref, s_ref, q_ref, *, n_heads):
    qc = _rms_norm(_dot(x_ref[...], wdq_ref[...]), g_ref[...]).astype(BF16)
    a = _dot(qc, w1_ref[...])
    b = _dot(qc, w2_ref[...])
    c = c_ref[...]
    s = s_ref[...]
    for h in range(n_heads):
        cols = slice(h * LANE, (h + 1) * LANE)
        q_ref[:, cols] = (a[:, cols] * c + b[:, cols] * s).astype(BF16)


def _q_proj(x, wdq, g, w1, w2, c_tab, s_tab, n_heads):
    r, d = x.shape
    ql = wdq.shape[1]
    row = lambda n: pl.BlockSpec((ROW_TILE, n), lambda i: (i, 0))
    full = lambda *shape: pl.BlockSpec(shape, lambda i: (0,) * len(shape))
    return pl.pallas_call(
        functools.partial(_q_kernel, n_heads=n_heads),
        grid=(r // ROW_TILE,),
        in_specs=[row(d), full(d, ql), full(1, ql), full(*w1.shape), full(*w2.shape), row(LANE), row(LANE)],
        out_specs=row(n_heads * LANE),
        out_shape=jax.ShapeDtypeStruct((r, n_heads * LANE), BF16),
        compiler_params=_cp(("arbitrary",), VMEM_LIMIT),
        name="q_proj",
    )(x, wdq, g, w1, w2, c_tab, s_tab)


def _q_sample_kernel(x_ref, wdq_ref, g_ref, w1_ref, wukt_ref, wpe_ref, wrot_ref, c_ref, s_ref,
                     qabs_ref, qpe_ref, *, n_heads, lora, scale):
    qc = _rms_norm(_dot(x_ref[...], wdq_ref[...]), g_ref[...]).astype(BF16)
    qn = (_dot(qc, w1_ref[...]) * scale).astype(BF16)
    for h in range(n_heads):
        qabs_ref[:, h * lora:(h + 1) * lora] = _dot(qn[:, h * LANE:(h + 1) * LANE], wukt_ref[h]).astype(BF16)
    qpe = _dot(qc, wpe_ref[...]) * c_ref[...] + _dot(qc, wrot_ref[...]) * s_ref[...]
    qpe_ref[...] = (qpe * scale).astype(BF16)


def _q_sample(x, wdq, g, w1, wukt, wpe, wrot, c_tab, s_tab, n_heads, lora, scale):
    ns = x.shape[0]
    args = (x, wdq, g, w1, wukt, wpe, wrot, c_tab, s_tab)
    full = lambda a: pl.BlockSpec(a.shape, lambda i, nd=a.ndim: (0,) * nd)
    return pl.pallas_call(
        functools.partial(_q_sample_kernel, n_heads=n_heads, lora=lora, scale=scale),
        grid=(1,),
        in_specs=[full(a) for a in args],
        out_specs=[pl.BlockSpec((ns, n_heads * lora), lambda i: (0, 0)),
                   pl.BlockSpec((ns, wpe.shape[1]), lambda i: (0, 0))],
        out_shape=[jax.ShapeDtypeStruct((ns, n_heads * lora), BF16),
                   jax.ShapeDtypeStruct((ns, wpe.shape[1]), BF16)],
        compiler_params=_cp(("arbitrary",), VMEM_LIMIT),
        name="q_sample",
    )(*args)


def _attn_step(m, acc, s, v):
    m_new = jnp.maximum(m, jnp.max(s, axis=-1, keepdims=True))
    p = jnp.exp2((s - m_new).astype(BF16))
    return m_new, jnp.exp2(m - m_new) * acc + _dot(p, v)


def _attn_prompt_kernel(q_ref, k_ref, v_ref, km_ref, vm_ref, o_ref, *, tq, tk, n_meta):
    i = pl.program_id(2)
    lane = lax.broadcasted_iota(jnp.int32, (tq, LANE), 1)
    heads = (slice(0, LANE), slice(LANE, 2 * LANE))
    qs = [q_ref[:, h] for h in heads]

    carry = []
    for q, h in zip(qs, heads):
        s = jnp.where(lane < n_meta, _dot_nt(q, km_ref[:, h]), -jnp.inf)
        m = jnp.max(s, axis=-1, keepdims=True)
        carry += [m, _dot(jnp.exp2((s - m).astype(BF16)), vm_ref[:, h])]

    def tile(j, carry, diag):
        off = pl.multiple_of(j * tk, tk)
        lo = 0 if diag is None else diag * tk
        out = []
        for e, (q, h) in enumerate(zip(qs, heads)):
            s = _dot_nt(q[lo:], k_ref[pl.ds(off, tk), h])
            if diag is not None:
                row = lax.broadcasted_iota(jnp.int32, (tk, tk), 0)
                col = lax.broadcasted_iota(jnp.int32, (tk, tk), 1)
                top = jnp.where(col <= row, s[:tk], -jnp.inf)
                s = jnp.concatenate([top, s[tk:]], axis=0) if tq - lo > tk else top
            m, acc = carry[2 * e], carry[2 * e + 1]
            m_new, acc_new = _attn_step(m[lo:], acc[lo:], s, v_ref[pl.ds(off, tk), h])
            if lo:
                m_new = jnp.concatenate([m[:lo], m_new], axis=0)
                acc_new = jnp.concatenate([acc[:lo], acc_new], axis=0)
            out += [m_new, acc_new]
        return out

    per_q = tq // tk
    carry = lax.fori_loop(0, i * per_q, lambda j, c: tuple(tile(j, c, None)), tuple(carry))
    for dj in range(per_q):
        carry = tile(i * per_q + dj, carry, dj)
    _, acc0, _, acc1 = carry
    l0 = jnp.sum(jnp.where(lane == LANE // 2, acc0, 0.0), axis=-1, keepdims=True)
    l1 = jnp.sum(jnp.where(lane == 0, acc1, 0.0), axis=-1, keepdims=True)
    o_ref[...] = jnp.where(lane < LANE // 2, acc0 * (1.0 / l0), acc1 * (1.0 / l1)).astype(BF16)


def _attn_prompt(q, k, v, bsz, seq, meta_row0, n_heads):
    tq, tk = min(ATTN_TQ, seq), ATTN_TK
    assert tq % tk == 0 and seq % tq == 0 and meta_row0 % LANE == 0 and n_heads % 2 == 0
    assert v.shape[1] == n_heads * LANE
    nq = seq // tq
    mb = meta_row0 // LANE
    pair = 2 * LANE
    return pl.pallas_call(
        functools.partial(_attn_prompt_kernel, tq=tq, tk=tk, n_meta=N_META),
        grid=(bsz, n_heads // 2, nq),
        in_specs=[
            pl.BlockSpec((tq, pair), lambda b, hp, i: (b * nq + i, hp)),
            pl.BlockSpec((seq, pair), lambda b, hp, i: (b, hp)),
            pl.BlockSpec((seq, pair), lambda b, hp, i: (b, hp)),
            pl.BlockSpec((LANE, pair), lambda b, hp, i: (mb, hp)),
            pl.BlockSpec((LANE, pair), lambda b, hp, i: (mb, hp)),
        ],
        out_specs=pl.BlockSpec((tq, LANE), lambda b, hp, i: (b * nq + i, hp)),
        out_shape=jax.ShapeDtypeStruct((bsz * seq, n_heads * LANE // 2), BF16),
        compiler_params=_cp(("arbitrary", "arbitrary", "arbitrary"), VMEM_LIMIT),
        name="attn_prompt",
    )(q, k, v, k, v)


def _softmax_part(s, v):
    m = jnp.max(s, axis=-1, keepdims=True)
    p = jnp.exp2(s - m)
    return m, jnp.sum(p, axis=-1, keepdims=True), _dot(p.astype(BF16), v)


def _softmax_merge(parts):
    m = functools.reduce(jnp.maximum, [p[0] for p in parts])
    w = [jnp.exp2(p[0] - m) for p in parts]
    l = functools.reduce(jnp.add, [wi * p[1] for wi, p in zip(w, parts)])
    acc = functools.reduce(jnp.add, [wi * p[2] for wi, p in zip(w, parts)])
    return m, l, acc


def _attn_paged_kernel(pt_ref, qabs_ref, qpe_ref, cnew_ref, pnew_ref, ckv_hbm, kpe_hbm, o_ref,
                       cbuf, pbuf, sem, m_ref, l_ref, acc_ref, *, ppc, n_groups, n_slots, n_heads, dec_seq, sub):
    c = pl.program_id(1)
    group = pl.program_id(0) * n_groups + c
    n_chunks = pl.num_programs(0) * n_groups * n_slots

    def copies(chunk, sl):
        base = chunk * ppc
        out = []
        for p in range(ppc):
            page = pt_ref[base + p]
            keys = pl.ds(p * PAGE_SIZE, PAGE_SIZE)
            out.append(pltpu.make_async_copy(ckv_hbm.at[page], cbuf.at[sl, keys, :], sem.at[0, sl]))
            out.append(pltpu.make_async_copy(kpe_hbm.at[page], pbuf.at[sl, :, keys], sem.at[1, sl]))
        return out

    def wait_slot(sl):
        pltpu.make_async_copy(cbuf.at[sl], cbuf.at[sl], sem.at[0, sl]).wait()
        pltpu.make_async_copy(pbuf.at[sl], pbuf.at[sl], sem.at[1, sl]).wait()

    @pl.when(group == 0)
    def _():
        for k in range(n_slots - 1):
            for cp in copies(k, k):
                cp.start()

    @pl.when(c == 0)
    def _():
        m_ref[...] = jnp.full(m_ref.shape, -jnp.inf, F32)
        l_ref[...] = jnp.zeros(l_ref.shape, F32)
        acc_ref[...] = jnp.zeros(acc_ref.shape, F32)

    qa = qabs_ref[0]
    qp = qpe_ref[0]
    n_sub = ppc * PAGE_SIZE // sub

    def slot_parts(sl):
        def scores(i):
            keys = slice(i * sub, (i + 1) * sub)
            kc = cbuf[sl, keys, :].astype(BF16)
            return _dot_nt(qa, kc) + _dot(qp, pbuf[sl, :, keys].astype(BF16)), kc

        ahead = [scores(i) for i in range(min(PAGED_SCORES_AHEAD, n_sub))]
        out = []
        for i in range(n_sub):
            s, kc = ahead.pop(0)
            if i + PAGED_SCORES_AHEAD < n_sub:
                ahead.append(scores(i + PAGED_SCORES_AHEAD))
            out.append(_softmax_part(s, kc))
        return out

    parts = [(m_ref[...], l_ref[...], acc_ref[...])]
    for k in range(n_slots):
        wait_slot(k)
        nxt = group * n_slots + (k + n_slots - 1)
        for cp in copies(jnp.where(nxt < n_chunks, nxt, 0), (k + n_slots - 1) % n_slots):
            cp.start()
        parts += slot_parts(k)
    m, l, acc = _softmax_merge(parts)
    m_ref[...] = m
    l_ref[...] = l
    acc_ref[...] = acc

    @pl.when(c == n_groups - 1)
    def _():
        kn = cnew_ref[0]
        s = _dot_nt(qa, kn) + _dot_nt(qp, pnew_ref[0])
        rows = s.shape[0]
        tok = lax.broadcasted_iota(jnp.int32, (rows, LANE), 0) // n_heads
        col = lax.broadcasted_iota(jnp.int32, (rows, LANE), 1)
        s = jnp.where((col <= tok) & (col < dec_seq), s, -jnp.inf)
        _, l2, acc2 = _softmax_merge([(m, l, acc), _softmax_part(s, kn)])
        o_ref[0] = acc2 * (1.0 / l2)

    @pl.when(group == pl.num_programs(0) * n_groups - 1)
    def _():
        for k in range(n_slots - 1):
            wait_slot(k)


def _attn_paged(page_table, qabs, qpe, cnew, pnew, cache_ckv, cache_kpe_t, n_heads, dec_seq):
    dec_b, n_pages = page_table.shape
    n_slots = PAGED_SLOTS
    assert n_pages % n_slots == 0
    ppc = min(PAGES_PER_CHUNK, n_pages // n_slots)
    sub = min(PAGED_KEY_BLOCK, ppc * PAGE_SIZE)
    assert n_pages % (n_slots * ppc) == 0 and cache_ckv.shape[1] == PAGE_SIZE and (ppc * PAGE_SIZE) % sub == 0
    n_groups = n_pages // (n_slots * ppc)
    rows, lora = qabs.shape[1], qabs.shape[2]
    rope = qpe.shape[2]
    kern = functools.partial(_attn_paged_kernel, ppc=ppc, n_groups=n_groups, n_slots=n_slots, n_heads=n_heads,
                             dec_seq=dec_seq, sub=sub)
    grid_spec = pltpu.PrefetchScalarGridSpec(
        num_scalar_prefetch=1,
        grid=(dec_b, n_groups),
        in_specs=[
            pl.BlockSpec((1, rows, lora), lambda b, c, pt: (b, 0, 0)),
            pl.BlockSpec((1, rows, rope), lambda b, c, pt: (b, 0, 0)),
            pl.BlockSpec((1, LANE, lora), lambda b, c, pt: (b, 0, 0)),
            pl.BlockSpec((1, LANE, rope), lambda b, c, pt: (b, 0, 0)),
            pl.BlockSpec(memory_space=pl.ANY),
            pl.BlockSpec(memory_space=pl.ANY),
        ],
        out_specs=pl.BlockSpec((1, rows, lora), lambda b, c, pt: (b, 0, 0)),
        scratch_shapes=[
            pltpu.VMEM((n_slots, ppc * PAGE_SIZE, lora), F32),
            pltpu.VMEM((n_slots, rope, ppc * PAGE_SIZE), F32),
            pltpu.SemaphoreType.DMA((2, n_slots)),
            pltpu.VMEM((rows, 1), F32),
            pltpu.VMEM((rows, 1), F32),
            pltpu.VMEM((rows, lora), F32),
        ],
    )
    return pl.pallas_call(
        kern,
        grid_spec=grid_spec,
        out_shape=jax.ShapeDtypeStruct((dec_b, rows, lora), F32),
        compiler_params=_cp(("arbitrary", "arbitrary"), VMEM_LIMIT),
        name="attn_paged",
    )(page_table.reshape(-1), qabs, qpe, cnew, pnew, cache_ckv, cache_kpe_t)


def _uv_sample_kernel(o_ref, w_ref, out_ref, *, n_heads, lora):
    for p in range(n_heads // 2):
        acc = _dot(o_ref[:, (2 * p) * lora:(2 * p + 1) * lora].astype(BF16), w_ref[2 * p])
        acc = acc + _dot(o_ref[:, (2 * p + 1) * lora:(2 * p + 2) * lora].astype(BF16), w_ref[2 * p + 1])
        out_ref[:, p * LANE:(p + 1) * LANE] = acc.astype(BF16)


def _uv_sample(o_lat, wuv_pad, n_heads, lora):
    ns = o_lat.shape[0]
    nv = n_heads * LANE // 2
    return pl.pallas_call(
        functools.partial(_uv_sample_kernel, n_heads=n_heads, lora=lora),
        grid=(1,),
        in_specs=[pl.BlockSpec(o_lat.shape, lambda i: (0, 0)), pl.BlockSpec(wuv_pad.shape, lambda i: (0, 0, 0))],
        out_specs=pl.BlockSpec((ns, nv), lambda i: (0, 0)),
        out_shape=jax.ShapeDtypeStruct((ns, nv), BF16),
        compiler_params=_cp(("arbitrary",), VMEM_LIMIT),
        name="uv_sample",
    )(o_lat, wuv_pad)


def _router_kernel(x_ref, w_ref, b_ref, o_ref, *, n_experts):
    logits = _dot(x_ref[...], w_ref[...]) + b_ref[...]
    lane = lax.broadcasted_iota(jnp.int32, logits.shape, 1).astype(F32)
    logits = jnp.where(lane < n_experts, logits, -jnp.inf)
    m1 = jnp.max(logits, axis=-1, keepdims=True)
    i1 = jnp.min(jnp.where(logits == m1, lane, float(LANE)), axis=-1, keepdims=True)
    rest = jnp.where(lane == i1, -jnp.inf, logits)
    m2 = jnp.max(rest, axis=-1, keepdims=True)
    i2 = jnp.min(jnp.where(rest == m2, lane, float(LANE)), axis=-1, keepdims=True)
    e2 = jnp.exp(m2 - m1)
    den = 1.0 + e2
    g1 = 1.0 / den
    g2 = e2 / den
    o_ref[...] = jnp.where(lane == 0, i1, jnp.where(lane == 1, i2, jnp.where(lane == 2, g1, jnp.where(lane == 3, g2, 0.0))))


def _router(x, w_pad, b_pad, n_experts):
    r, d = x.shape
    return pl.pallas_call(
        functools.partial(_router_kernel, n_experts=n_experts),
        grid=(r // ROW_TILE,),
        in_specs=[pl.BlockSpec((ROW_TILE, d), lambda i: (i, 0)), pl.BlockSpec((d, LANE), lambda i: (0, 0)),
                  pl.BlockSpec((1, LANE), lambda i: (0, 0))],
        out_specs=pl.BlockSpec((ROW_TILE, LANE), lambda i: (i, 0)),
        out_shape=jax.ShapeDtypeStruct((r, LANE), F32),
        compiler_params=_cp(("arbitrary",), VMEM_LIMIT),
        name="router",
    )(x, w_pad, b_pad)


def _moe_ffn_kernel(te_ref, nu_ref, src_ref, x_hbm, wg_ref, wu_ref, w2_ref, o_ref, xbuf, xb_ref, acc_ref, sem,
                    *, tm, rows_per_step):
    t = pl.program_id(0)
    f = pl.program_id(1)
    nf = pl.num_programs(1)
    used = t < nu_ref[0]
    slot = t % 2
    n_rows = rows_per_step * nf

    def row_copy(tile, r, sl):
        row = src_ref[tile * tm + r]
        return pltpu.make_async_copy(x_hbm.at[pl.ds(row * SUBLANES, SUBLANES), :],
                                     xbuf.at[sl, pl.ds(r * SUBLANES, SUBLANES), :], sem.at[sl])

    @pl.when((t == 0) & (f == 0))
    def _():
        def body(g, carry):
            for u in range(GATHER_UNROLL):
                row_copy(0, g * GATHER_UNROLL + u, 0).start()
            return carry
        lax.fori_loop(0, n_rows // GATHER_UNROLL, body, 0)

    @pl.when(f == 0)
    def _():
        pltpu.make_async_copy(xbuf.at[slot], xbuf.at[slot], sem.at[slot]).wait()
        for s, slab in enumerate(_load_row_tiles(xbuf, (slot,), tm)):
            xb_ref[:, s * LANE:(s + 1) * LANE] = slab.astype(BF16)

    def prefetch():
        for u in range(rows_per_step):
            row_copy(t + 1, f * rows_per_step + u, 1 - slot).start()

    @pl.when(used)
    def _():
        prefetch()
        x = xb_ref[...]
        y = None
        for c in range(wg_ref.shape[2] // FFN_CHUNK):
            cols = slice(c * FFN_CHUNK, (c + 1) * FFN_CHUNK)
            hcol = (_silu(_dot(x, wg_ref[0, :, cols])) * _dot(x, wu_ref[0, :, cols])).astype(BF16)
            part = _dot(hcol, w2_ref[0, cols, :])
            y = part if y is None else y + part

        @pl.when(f == 0)
        def _():
            acc_ref[...] = y

        @pl.when(f > 0)
        def _():
            acc_ref[...] += y

    @pl.when(jnp.logical_not(used))
    def _():
        prefetch()

    @pl.when(f == nf - 1)
    def _():
        _store_row_tiles(o_ref, jnp.where(used, acc_ref[...], 0.0))

    @pl.when((t == pl.num_programs(0) - 1) & (f == nf - 1))
    def _():
        pltpu.make_async_copy(xbuf.at[1 - slot], xbuf.at[1 - slot], sem.at[1 - slot]).wait()


def _moe_ffn(tile_expert, n_used, src, x, w13, w2):
    d = w13.shape[1]
    assert x.shape[1] == LANE and d == SUBLANES * LANE
    n_exp, eff = w2.shape[0], w2.shape[1]
    tm, tf = MOE_ROW_TILE, min(MOE_FF_CHUNK, eff)
    n = src.shape[0] - 2 * tm
    assert n % tm == 0 and eff % tf == 0 and tf % FFN_CHUNK == 0
    nf = eff // tf
    rows_per_step = -(-tm // (nf * GATHER_UNROLL)) * GATHER_UNROLL
    assert rows_per_step * nf <= 2 * tm

    def widx(base):
        def index_map(t, f, te, nu, src):
            live = t < nu[0]
            return (te[t], 0, jnp.where(live, f, nf - 1) + base)
        return index_map

    def w2idx(t, f, te, nu, src):
        return (te[t], jnp.where(t < nu[0], f, nf - 1), 0)

    grid_spec = pltpu.PrefetchScalarGridSpec(
        num_scalar_prefetch=3,
        grid=(n // tm, nf),
        in_specs=[
            pl.BlockSpec(memory_space=pl.ANY),
            pl.BlockSpec((1, d, tf), widx(0)),
            pl.BlockSpec((1, d, tf), widx(nf)),
            pl.BlockSpec((1, tf, d), w2idx),
        ],
        out_specs=pl.BlockSpec((tm * SUBLANES, LANE), lambda t, f, te, nu, src: (t, 0)),
        scratch_shapes=[
            pltpu.VMEM((2, rows_per_step * nf * SUBLANES, LANE), F32),
            pltpu.VMEM((tm, d), BF16),
            pltpu.VMEM((tm, d), F32),
            pltpu.SemaphoreType.DMA((2,)),
        ],
    )
    return pl.pallas_call(
        functools.partial(_moe_ffn_kernel, tm=tm, rows_per_step=rows_per_step),
        grid_spec=grid_spec,
        out_shape=jax.ShapeDtypeStruct((n * SUBLANES, LANE), F32),
        compiler_params=_cp(("arbitrary", "arbitrary"), VMEM_LIMIT),
        name="moe_ffn",
    )(tile_expert, n_used, src, x, w13, w13, w2)


def _combine_ln_kernel(p1_ref, p2_ref, y_hbm, h_ref, route_ref, g_ref, b_ref, o_ref, buf, sem, *, tile, alpha):
    i = pl.program_id(0)
    n = pl.num_programs(0)
    slot = i % 2

    def issue(t, sl):
        def body(g, carry):
            for u in range(GATHER_UNROLL):
                r = g * GATHER_UNROLL + u
                a = p1_ref[t * tile + r]
                b = p2_ref[t * tile + r]
                dst = pl.ds(r * SUBLANES, SUBLANES)
                pltpu.make_async_copy(y_hbm.at[pl.ds(a * SUBLANES, SUBLANES), :], buf.at[sl, 0, dst, :],
                                      sem.at[sl]).start()
                pltpu.make_async_copy(y_hbm.at[pl.ds(b * SUBLANES, SUBLANES), :], buf.at[sl, 1, dst, :],
                                      sem.at[sl]).start()
            return carry
        lax.fori_loop(0, tile // GATHER_UNROLL, body, 0)

    @pl.when(i == 0)
    def _():
        issue(i, slot)

    @pl.when(i + 1 < n)
    def _():
        issue(i + 1, 1 - slot)

    pltpu.make_async_copy(buf.at[slot], buf.at[slot], sem.at[slot]).wait()
    route = route_ref[...]
    g1, g2 = route[:, 2:3], route[:, 3:4]
    ya = _load_row_tiles(buf, (slot, 0), tile)
    yb = _load_row_tiles(buf, (slot, 1), tile)
    z = jnp.concatenate([alpha * h_ref[:, s * LANE:(s + 1) * LANE] + (g1 * ya[s] + g2 * yb[s])
                         for s in range(SUBLANES)], axis=1)
    o_ref[...] = _layer_norm(z, g_ref[...], b_ref[...])


def _combine_ln(pos1, pos2, y_sorted, h, route, g, b, alpha):
    r, d = h.shape
    tile = GATHER_TILE
    assert r % tile == 0 and tile % GATHER_UNROLL == 0 and d == SUBLANES * LANE
    grid_spec = pltpu.PrefetchScalarGridSpec(
        num_scalar_prefetch=2,
        grid=(r // tile,),
        in_specs=[
            pl.BlockSpec(memory_space=pl.ANY),
            pl.BlockSpec((tile, d), lambda i, p1, p2: (i, 0)),
            pl.BlockSpec((tile, LANE), lambda i, p1, p2: (i, 0)),
            pl.BlockSpec((1, d), lambda i, p1, p2: (0, 0)),
            pl.BlockSpec((1, d), lambda i, p1, p2: (0, 0)),
        ],
        out_specs=pl.BlockSpec((tile, d), lambda i, p1, p2: (i, 0)),
        scratch_shapes=[pltpu.VMEM((2, 2, tile * SUBLANES, LANE), F32), pltpu.SemaphoreType.DMA((2,))],
    )
    return pl.pallas_call(
        functools.partial(_combine_ln_kernel, tile=tile, alpha=alpha),
        grid_spec=grid_spec,
        out_shape=jax.ShapeDtypeStruct((r, d), F32),
        compiler_params=_cp(("arbitrary",), VMEM_LIMIT),
        name="combine_ln",
    )(pos1, pos2, y_sorted, h, route, g, b)


def _route_plan(route, n_experts, tm):
    r = route.shape[0]
    e = jnp.concatenate([route[:, 0], route[:, 1]]).astype(jnp.int32)
    onehot = (e[:, None] == jnp.arange(n_experts, dtype=jnp.int32)[None, :]).astype(jnp.int32)
    csum = jnp.cumsum(onehot, axis=0)
    rank = jnp.take_along_axis(csum, e[:, None], axis=1)[:, 0] - 1
    tiles = (csum[-1] + tm - 1) // tm
    tile_end = jnp.cumsum(tiles)
    pos = (tile_end - tiles)[e] * tm + rank
    n_tiles = (TOP_K * r) // tm + n_experts
    tok = jnp.arange(TOP_K * r, dtype=jnp.int32) % r
    src = jnp.zeros(((n_tiles + 2) * tm,), jnp.int32).at[pos].set(tok, unique_indices=True)
    n_used = tile_end[-1]
    tile_id = jnp.minimum(jnp.arange(n_tiles, dtype=jnp.int32), n_used - 1)
    tile_expert = jnp.sum((tile_end[None, :] <= tile_id[:, None]).astype(jnp.int32), axis=1)
    return src, tile_expert, n_used.reshape(1).astype(jnp.int32), pos[:r], pos[r:]


def _pad_cols(w, offset, width):
    return jnp.pad(w, ((0, 0), (offset, width - offset - w.shape[1])))


def kernel(x_prompt, x_sample, state_ret, cache_ckv, cache_kpe, page_table, meta_tokens, ln_g, ln_b, ret_w_in, ret_gn_g, ret_w_o, mla_w_dq, mla_q_norm, mla_w_uq, mla_w_o, kv_w_a, kv_norm, kv_w_b, ffn_w13, ffn_w2, moe_w_r, moe_b_r, moe_w13, moe_w2):
    bsz, seq, d = x_prompt.shape
    dec_b, dec_seq, _ = x_sample.shape
    n_ret, _, ret_h, dk, dv = state_ret.shape
    depth = ln_g.shape[0]
    assert depth == 2 and n_ret == 1 and mla_w_dq.shape[0] == 1, "layer pattern: one retention layer then one MLA layer"
    past_len = page_table.shape[1] * PAGE_SIZE
    lora, mla_h, nope_v = kv_w_b.shape
    rope = kv_w_a.shape[1] - lora
    nope = mla_w_uq.shape[2] // mla_h - rope
    v_dim = nope_v - nope
    n_experts = moe_w_r.shape[2]
    assert nope == LANE // 2 and v_dim == LANE // 2 and rope <= LANE // 4
    alpha = (2 * depth) ** 0.25
    mla_scale = (nope + rope) ** -0.5
    half = rope // 2

    n_p, n_s = bsz * seq, dec_b * dec_seq
    meta_row0 = n_p + n_s
    r_real = meta_row0 + N_META
    r_pad = -(-r_real // ROW_TILE) * ROW_TILE
    assert r_pad - meta_row0 >= LANE

    x0 = jnp.concatenate([x_prompt.reshape(n_p, d), x_sample.reshape(n_s, d), meta_tokens,
                          jnp.zeros((r_pad - r_real, d), F32)], axis=0)
    pos = jnp.concatenate([jnp.tile(N_META + jnp.arange(seq), bsz), jnp.tile(past_len + jnp.arange(dec_seq), dec_b),
                           jnp.arange(N_META), jnp.zeros((r_pad - r_real,), jnp.int32)]).astype(F32)

    inv_r = ROPE_BASE ** (-jnp.arange(dk // 2, dtype=F32) / (dk // 2))
    ang_r = pos[:, None] * inv_r[None, :]
    cos_r, sin_r = jnp.cos(ang_r), jnp.sin(ang_r)
    inv_m = ROPE_BASE ** (-jnp.arange(half, dtype=F32) / half)
    ang_m = pos[:, None] * inv_m[None, :]
    cos_m = jnp.concatenate([jnp.cos(ang_m)] * 2, axis=1)
    sin_m = jnp.concatenate([jnp.sin(ang_m)] * 2, axis=1)
    cos_a, sin_a = _pad_cols(cos_m, 0, LANE), _pad_cols(sin_m, 0, LANE)
    cos_b, sin_b = _pad_cols(cos_m, nope, LANE), _pad_cols(sin_m, nope, LANE)
    q_scale = mla_scale * LOG2E
    q_ctab = (_pad_cols(jnp.ones((r_pad, nope), F32), 0, LANE) + cos_b) * q_scale
    q_stab = sin_b * q_scale

    def rot(w):
        return jnp.concatenate([-w[..., half:], w[..., :half]], axis=-1)

    qkvg = _ret_proj(x0, ret_w_in[0], cos_r, sin_r, ret_h, dk)
    gn = ret_gn_g[0].reshape(1, -1)
    mix_p, mix_meta, st_p = _ret_prompt(qkvg, gn, _decay_tables(ret_h, dk, RET_CHUNK, RET_CHUNK),
                                        bsz, seq, meta_row0, ret_h, dk, dv)
    mix_s, st_s = _ret_sample(qkvg, state_ret[0], gn,
                              _decay_tables(ret_h, dk, SAMPLE_BATCH_TILE * dec_seq, dec_seq),
                              n_p, dec_b, dec_seq, ret_h, dk, dv)
    mix = jnp.concatenate([mix_p, mix_s, mix_meta, jnp.zeros((r_pad - r_real, mix_p.shape[1]), BF16)], axis=0)
    row = lambda v: v.reshape(1, -1)
    h1, h1b = _proj_ln(mix, ret_w_o[0], x0, row(ln_g[0, 0]), row(ln_b[0, 0]), alpha)
    h2, h2b = _ffn_ln(h1b, ffn_w13[0].astype(BF16), ffn_w2[0].astype(BF16), h1, row(ln_g[0, 1]), row(ln_b[0, 1]), alpha)

    wc, wpe = kv_w_a[:, :lora], kv_w_a[:, lora:]
    wkv = jnp.concatenate([wc, _pad_cols(wpe, 0, LANE), _pad_cols(rot(wpe), 0, LANE),
                           _pad_cols(wpe, nope, LANE), _pad_cols(rot(wpe), nope, LANE)], axis=1).astype(BF16)
    wuk = jnp.pad(kv_w_b[:, :, :nope], ((0, 0), (0, 0), (0, LANE - nope))).reshape(lora, mla_h * LANE).astype(BF16)
    even = (jnp.arange(mla_h) % 2 == 0)[None, :, None]
    wuv_h = kv_w_b[:, :, nope:]
    wuv = jnp.where(even, jnp.pad(wuv_h, ((0, 0), (0, 0), (0, LANE - v_dim))),
                    jnp.pad(wuv_h, ((0, 0), (0, 0), (LANE - v_dim, 0)))).reshape(lora, mla_h * LANE).astype(BF16)
    lane_id = jnp.arange(LANE)[None, None, :]
    ones_row = jnp.where(even, lane_id == v_dim, lane_id == 0).astype(F32).reshape(1, mla_h * LANE)
    ckv, kpe, k_heads, v_heads = _kv_proj(h2b, wkv, row(kv_norm), (cos_a, sin_a, cos_b, sin_b), wuk, wuv, ones_row,
                                          lora, rope, mla_h)

    wuq = mla_w_uq[0].reshape(-1, mla_h, nope + rope)
    wq_n, wq_r = wuq[:, :, :nope], wuq[:, :, nope:]
    pad_h = lambda w, off: jnp.pad(w, ((0, 0), (0, 0), (off, LANE - off - w.shape[2]))).reshape(w.shape[0], mla_h * LANE)
    w_q1 = (pad_h(wq_n, 0) + pad_h(wq_r, nope)).astype(BF16)
    w_q2 = pad_h(rot(wq_r), nope).astype(BF16)
    wdq = mla_w_dq[0].astype(BF16)
    qn_g = row(mla_q_norm[0])
    q_heads = _q_proj(h2b, wdq, qn_g, w_q1, w_q2, q_ctab, q_stab, mla_h)
    attn_p = _attn_prompt(q_heads, k_heads, v_heads, bsz, seq, meta_row0, mla_h)

    wukt = jnp.pad(jnp.transpose(kv_w_b[:, :, :nope], (1, 2, 0)), ((0, 0), (0, LANE - nope), (0, 0))).astype(BF16)
    w_pe = wq_r.reshape(-1, mla_h * rope).astype(BF16)
    w_pe_rot = rot(wq_r).reshape(-1, mla_h * rope).astype(BF16)
    cs = jnp.tile(cos_m[n_p:n_p + n_s], (1, mla_h))
    ss = jnp.tile(sin_m[n_p:n_p + n_s], (1, mla_h))
    qabs, qpe = _q_sample(h2b[n_p:n_p + n_s], wdq, qn_g, pad_h(wq_n, 0).astype(BF16), wukt, w_pe, w_pe_rot, cs, ss,
                          mla_h, lora, q_scale)
    ckv_s, kpe_s = ckv[n_p:n_p + n_s], kpe[n_p:n_p + n_s]
    new_pad = lambda a: jnp.pad(a.reshape(dec_b, dec_seq, -1), ((0, 0), (0, LANE - dec_seq), (0, 0))).astype(BF16)
    o_lat = _attn_paged(page_table, qabs.reshape(dec_b, dec_seq * mla_h, lora), qpe.reshape(dec_b, dec_seq * mla_h, rope),
                        new_pad(ckv_s), new_pad(kpe_s), cache_ckv, jnp.swapaxes(cache_kpe, 1, 2), mla_h, dec_seq)
    wuv_t = jnp.transpose(wuv_h, (1, 0, 2))
    wuv_pad = jnp.where(jnp.transpose(even, (1, 0, 2)), jnp.pad(wuv_t, ((0, 0), (0, 0), (0, LANE - v_dim))),
                        jnp.pad(wuv_t, ((0, 0), (0, 0), (LANE - v_dim, 0)))).astype(BF16)
    attn_s = _uv_sample(o_lat.reshape(n_s, mla_h * lora), wuv_pad, mla_h, lora)
    attn = jnp.concatenate([attn_p, attn_s, jnp.zeros((r_pad - meta_row0, attn_p.shape[1]), BF16)], axis=0)
    h3, h3b, h3_tiles = _proj_ln(attn, mla_w_o[0], h2, row(ln_g[1, 0]), row(ln_b[1, 0]), alpha, row_tiles=True)

    w_r = _pad_cols(moe_w_r[0], 0, LANE).astype(BF16)
    b_r = _pad_cols(moe_b_r[0].reshape(1, -1), 0, LANE)
    route = _router(h3b, w_r, b_r, n_experts)
    src, tile_expert, n_used, pos1, pos2 = _route_plan(route, n_experts, MOE_ROW_TILE)
    y_sorted = _moe_ffn(tile_expert, n_used, src, h3_tiles, moe_w13[0].astype(BF16), moe_w2[0].astype(BF16))
    h4 = _combine_ln(pos1, pos2, y_sorted, h3, route, row(ln_g[1, 1]), row(ln_b[1, 1]), alpha)

    y_prompt = h4[:n_p].reshape(bsz, seq, d)
    y_sample = h4[n_p:n_p + n_s].reshape(dec_b, dec_seq, d)

    def with_meta(a):
        meta = jnp.broadcast_to(a[meta_row0:r_real][None], (bsz, N_META, a.shape[1]))
        return jnp.concatenate([meta, a[:n_p].reshape(bsz, seq, -1)], axis=1)

    return (y_prompt, y_sample, st_p[None], with_meta(ckv), with_meta(kpe), st_s[None],
            ckv_s.reshape(dec_b, dec_seq, lora), kpe_s.reshape(dec_b, dec_seq, rope))
```

```python
import functools
import math

import jax
import jax.numpy as jnp
from jax import lax
from jax.experimental import pallas as pl
from jax.experimental.pallas import tpu as pltpu

F32 = jnp.float32
BF16 = jnp.bfloat16

N_META = 16
TOP_K = 2
ROPE_BASE = 10000.0
LN_EPS = 1e-5
RMS_EPS = 1e-6
PAGE_SIZE = 128
RET_CHUNK = 128

LANE = 128
SUBLANES = 8
ROW_TILE = 512
RET_PROJ_TILE = 1024
ATTN_TQ = 1024
ATTN_TK = 512
FFN_CHUNK = 256
MOE_ROW_TILE = 512
MOE_FF_CHUNK = 1792
GATHER_TILE = 256
GATHER_UNROLL = 8
PAGES_PER_CHUNK = 32
PAGED_SLOTS = 4
PAGED_KEY_BLOCK = 1024
PAGED_SCORES_AHEAD = 8
LOG2E = math.log2(math.e)
SAMPLE_BATCH_TILE = 4
VMEM_LIMIT = 56 * 1024 * 1024


def _cp(sem, vmem=None):
    return pltpu.CompilerParams(dimension_semantics=sem, vmem_limit_bytes=vmem)


def _layer_norm(z, g, b):
    mu = jnp.mean(z, axis=-1, keepdims=True)
    zc = z - mu
    var = jnp.mean(zc * zc, axis=-1, keepdims=True)
    return zc * lax.rsqrt(var + LN_EPS) * g + b


def _rms_norm(c, g):
    return c * lax.rsqrt(jnp.mean(c * c, axis=-1, keepdims=True) + RMS_EPS) * g


def _silu(x):
    return x * (1.0 / (1.0 + jnp.exp(-x)))


def _dot(a, b):
    return jnp.dot(a, b, preferred_element_type=F32)


def _dot_nt(a, b):
    return lax.dot_general(a, b, (((1,), (1,)), ((), ())), preferred_element_type=F32)


def _dot_tn(a, b):
    return lax.dot_general(a, b, (((0,), (0,)), ((), ())), preferred_element_type=F32)


def _ret_proj_kernel(x_ref, w_ref, cos_ref, sin_ref, o_ref, wbf_ref, *, n_heads, dk, k_scale):
    j = pl.program_id(0)
    i = pl.program_id(1)

    @pl.when(i == 0)
    def _():
        wbf_ref[...] = w_ref[...].astype(BF16)

    y = _dot(x_ref[...].astype(BF16), wbf_ref[...])
    half = dk // 2

    @pl.when(j < 2)
    def _():
        c = cos_ref[...]
        s = sin_ref[...]
        scale = jnp.where(j == 1, k_scale, 1.0).astype(F32)
        for h in range(n_heads):
            x1 = y[:, h * dk:h * dk + half]
            x2 = y[:, h * dk + half:(h + 1) * dk]
            o_ref[:, h * dk:h * dk + half] = ((x1 * c - x2 * s) * scale).astype(BF16)
            o_ref[:, h * dk + half:(h + 1) * dk] = ((x1 * s + x2 * c) * scale).astype(BF16)

    @pl.when(j >= 2)
    def _():
        o_ref[...] = y.astype(BF16)


def _ret_proj(x, w_in, cos_r, sin_r, n_heads, dk):
    r, d = x.shape
    n = w_in.shape[1]
    qk = n_heads * dk
    assert n % qk == 0 and dk // 2 == LANE
    kern = functools.partial(_ret_proj_kernel, n_heads=n_heads, dk=dk, k_scale=dk ** -0.5)
    tile = RET_PROJ_TILE if r % RET_PROJ_TILE == 0 else ROW_TILE
    return pl.pallas_call(
        kern,
        grid=(n // qk, r // tile),
        in_specs=[
            pl.BlockSpec((tile, d), lambda j, i: (i, 0)),
            pl.BlockSpec((d, qk), lambda j, i: (0, j)),
            pl.BlockSpec((tile, LANE), lambda j, i: (i, 0)),
            pl.BlockSpec((tile, LANE), lambda j, i: (i, 0)),
        ],
        out_specs=pl.BlockSpec((tile, qk), lambda j, i: (i, j)),
        out_shape=jax.ShapeDtypeStruct((r, n), BF16),
        scratch_shapes=[pltpu.VMEM((d, qk), BF16)],
        compiler_params=_cp(("arbitrary", "arbitrary"), VMEM_LIMIT),
        name="ret_proj",
    )(x, w_in, cos_r, sin_r)


def _group_norm_gate(o, g, gn):
    mu = jnp.mean(o, axis=-1, keepdims=True)
    oc = o - mu
    var = jnp.mean(oc * oc, axis=-1, keepdims=True)
    return (_silu(g.astype(F32)) * (oc * lax.rsqrt(var + LN_EPS) * gn)).astype(BF16)


def _ret_prompt_kernel(q_ref, k_ref, v_ref, g_ref, qm_ref, km_ref, vm_ref, gm_ref, gn_ref,
                       dec_ref, rdec_ref, cdec_ref, cpow_ref,
                       o_ref, om_ref, st_ref, s_ref, *, n_heads, dk, dv, chunk, n_meta, tile):
    t = pl.program_id(1)
    nt = pl.num_programs(1)

    @pl.when(t == 0)
    def _():
        for h in range(n_heads):
            q = qm_ref[:, h * dk:(h + 1) * dk]
            k = km_ref[:, h * dk:(h + 1) * dk]
            v = vm_ref[:, h * dv:(h + 1) * dv]
            sc = _dot_nt(q, k) * dec_ref[h, :n_meta, :n_meta]
            o = _dot(sc.astype(BF16), v)
            om_ref[:, h * dv:(h + 1) * dv] = _group_norm_gate(
                o, gm_ref[:, h * dv:(h + 1) * dv], gn_ref[:, h * dv:(h + 1) * dv])
            k_out = (k.astype(F32) * cdec_ref[h, chunk - n_meta:, :]).astype(BF16)
            s_ref[h] = _dot_tn(k_out, v)

    for h in range(n_heads):
        for c in range(tile // chunk):
            rows = slice(c * chunk, (c + 1) * chunk)
            q = q_ref[rows, h * dk:(h + 1) * dk]
            k = k_ref[rows, h * dk:(h + 1) * dk]
            v = v_ref[rows, h * dv:(h + 1) * dv]
            s_old = s_ref[h]
            sc = _dot_nt(q, k) * dec_ref[h]
            q_in = (q.astype(F32) * rdec_ref[h]).astype(BF16)
            o = _dot(sc.astype(BF16), v) + _dot(q_in, s_old.astype(BF16))
            k_out = (k.astype(F32) * cdec_ref[h]).astype(BF16)
            s_ref[h] = cpow_ref[h] * s_old + _dot_tn(k_out, v)
            o_ref[rows, h * dv:(h + 1) * dv] = _group_norm_gate(
                o, g_ref[rows, h * dv:(h + 1) * dv], gn_ref[:, h * dv:(h + 1) * dv])

    @pl.when(t == nt - 1)
    def _():
        st_ref[0] = s_ref[...]


def _ret_prompt(qkvg, gn_g, tabs, bsz, seq, meta_row0, n_heads, dk, dv):
    qk, vw = n_heads * dk, n_heads * dv
    tile = ROW_TILE
    assert seq % tile == 0 and tile % RET_CHUNK == 0 and meta_row0 % N_META == 0 and vw == 2 * qk
    nt = seq // tile
    mb = meta_row0 // N_META
    dec, rdec, cdec, cpow = tabs
    kern = functools.partial(_ret_prompt_kernel, n_heads=n_heads, dk=dk, dv=dv, chunk=RET_CHUNK,
                             n_meta=N_META, tile=tile)
    full = lambda *shape: pl.BlockSpec(shape, lambda b, t: (0,) * len(shape))
    return pl.pallas_call(
        kern,
        grid=(bsz, nt),
        in_specs=[
            pl.BlockSpec((tile, qk), lambda b, t: (b * nt + t, 0)),
            pl.BlockSpec((tile, qk), lambda b, t: (b * nt + t, 1)),
            pl.BlockSpec((tile, vw), lambda b, t: (b * nt + t, 1)),
            pl.BlockSpec((tile, vw), lambda b, t: (b * nt + t, 2)),
            pl.BlockSpec((N_META, qk), lambda b, t: (mb, 0)),
            pl.BlockSpec((N_META, qk), lambda b, t: (mb, 1)),
            pl.BlockSpec((N_META, vw), lambda b, t: (mb, 1)),
            pl.BlockSpec((N_META, vw), lambda b, t: (mb, 2)),
            full(1, vw),
            full(n_heads, RET_CHUNK, RET_CHUNK),
            full(n_heads, RET_CHUNK, dk),
            full(n_heads, RET_CHUNK, dk),
            full(n_heads, 1, 1),
        ],
        out_specs=[
            pl.BlockSpec((tile, vw), lambda b, t: (b * nt + t, 0)),
            pl.BlockSpec((N_META, vw), lambda b, t: (0, 0)),
            pl.BlockSpec((1, n_heads, dk, dv), lambda b, t: (b, 0, 0, 0)),
        ],
        out_shape=[
            jax.ShapeDtypeStruct((bsz * seq, vw), BF16),
            jax.ShapeDtypeStruct((N_META, vw), BF16),
            jax.ShapeDtypeStruct((bsz, n_heads, dk, dv), F32),
        ],
        scratch_shapes=[pltpu.VMEM((n_heads, dk, dv), F32)],
        compiler_params=_cp(("arbitrary", "arbitrary"), VMEM_LIMIT),
        name="ret_prompt",
    )(qkvg, qkvg, qkvg, qkvg, qkvg, qkvg, qkvg, qkvg, gn_g, dec, rdec, cdec, cpow)


def _ret_sample_kernel(q_ref, k_ref, v_ref, g_ref, s0_ref, gn_ref, dec_ref, rdec_ref, cdec_ref, cpow_ref,
                       o_ref, st_ref, *, n_heads, dk, dv, nb, dec_seq):
    rows = nb * dec_seq
    row_b = lax.broadcasted_iota(jnp.int32, (rows, 1), 0) // dec_seq
    for h in range(n_heads):
        q = q_ref[:, h * dk:(h + 1) * dk]
        k = k_ref[:, h * dk:(h + 1) * dk]
        v = v_ref[:, h * dv:(h + 1) * dv]
        sc = _dot_nt(q, k) * dec_ref[h]
        o = _dot(sc.astype(BF16), v)
        q_in = q.astype(F32) * rdec_ref[h]
        k_out = k.astype(F32) * cdec_ref[h]
        for b in range(nb):
            s_old = s0_ref[0, b, h]
            sel = row_b == b
            o = o + _dot(jnp.where(sel, q_in, 0.0).astype(BF16), s_old.astype(BF16))
            st_ref[0, b, h] = cpow_ref[h] * s_old + _dot_tn(jnp.where(sel, k_out, 0.0).astype(BF16), v)
        o_ref[:, h * dv:(h + 1) * dv] = _group_norm_gate(
            o, g_ref[:, h * dv:(h + 1) * dv], gn_ref[:, h * dv:(h + 1) * dv])


def _ret_sample(qkvg, state, gn_g, tabs, row0, dec_b, dec_seq, n_heads, dk, dv):
    qk, vw = n_heads * dk, n_heads * dv
    nb = SAMPLE_BATCH_TILE
    rows = nb * dec_seq
    assert dec_b % nb == 0 and row0 % rows == 0 and rows % 16 == 0
    rb = row0 // rows
    dec, rdec, cdec, cpow = tabs
    state5 = state.reshape(dec_b // nb, nb, n_heads, dk, dv)
    kern = functools.partial(_ret_sample_kernel, n_heads=n_heads, dk=dk, dv=dv, nb=nb, dec_seq=dec_seq)
    full = lambda *shape: pl.BlockSpec(shape, lambda i: (0,) * len(shape))
    out, st = pl.pallas_call(
        kern,
        grid=(dec_b // nb,),
        in_specs=[
            pl.BlockSpec((rows, qk), lambda i: (rb + i, 0)),
            pl.BlockSpec((rows, qk), lambda i: (rb + i, 1)),
            pl.BlockSpec((rows, vw), lambda i: (rb + i, 1)),
            pl.BlockSpec((rows, vw), lambda i: (rb + i, 2)),
            pl.BlockSpec((1, nb, n_heads, dk, dv), lambda i: (i, 0, 0, 0, 0)),
            full(1, vw),
            full(n_heads, rows, rows),
            full(n_heads, rows, dk),
            full(n_heads, rows, dk),
            full(n_heads, 1, 1),
        ],
        out_specs=[
            pl.BlockSpec((rows, vw), lambda i: (i, 0)),
            pl.BlockSpec((1, nb, n_heads, dk, dv), lambda i: (i, 0, 0, 0, 0)),
        ],
        out_shape=[
            jax.ShapeDtypeStruct((dec_b * dec_seq, vw), BF16),
            jax.ShapeDtypeStruct(state5.shape, F32),
        ],
        compiler_params=_cp(("arbitrary",), VMEM_LIMIT),
        name="ret_sample",
    )(qkvg, qkvg, qkvg, qkvg, state5, gn_g, dec, rdec, cdec, cpow)
    return out, st.reshape(dec_b, n_heads, dk, dv)


def _decay_tables(n_heads, dk, chunk, group):
    lg = jnp.log1p(-jnp.exp2(-5.0 - jnp.arange(n_heads, dtype=F32)))
    r = jnp.arange(chunk)
    t = (r % group).astype(F32)
    diff = t[:, None] - t[None, :]
    same = (r[:, None] // group) == (r[None, :] // group)
    dec = jnp.where(same & (diff >= 0), jnp.exp(jnp.maximum(diff, 0.0)[None] * lg[:, None, None]), 0.0)
    rdec = jnp.exp((t[None, :] + 1.0) * lg[:, None])
    cdec = jnp.exp((group - 1.0 - t)[None, :] * lg[:, None])
    cpow = jnp.exp(group * lg)
    bc = lambda a: jnp.broadcast_to(a[:, :, None], (n_heads, chunk, dk))
    return dec, bc(rdec), bc(cdec), cpow.reshape(n_heads, 1, 1)


def _store_row_tiles(ref, y):
    rows = y.shape[0]
    for s in range(SUBLANES):
        ref[pl.ds(s, rows, stride=SUBLANES), :] = y[:, s * LANE:(s + 1) * LANE]


def _load_row_tiles(ref, lead, rows):
    return [ref[lead + (pl.ds(s, rows, stride=SUBLANES), slice(None))] for s in range(SUBLANES)]


def _proj_ln_kernel(a_ref, w_ref, h_ref, g_ref, b_ref, o_ref, obf_ref, *rest, alpha, row_tiles):
    wbf_ref = rest[-1]

    @pl.when(pl.program_id(0) == 0)
    def _():
        wbf_ref[...] = w_ref[...].astype(BF16)

    z = alpha * h_ref[...] + _dot(a_ref[...], wbf_ref[...])
    y = _layer_norm(z, g_ref[...], b_ref[...])
    o_ref[...] = y
    obf_ref[...] = y.astype(BF16)
    if row_tiles:
        _store_row_tiles(rest[0], y)


def _proj_ln(a, w, h, g, b, alpha, row_tiles=False):
    r, k = a.shape
    d = w.shape[1]
    assert not row_tiles or d == SUBLANES * LANE
    row = lambda n: pl.BlockSpec((ROW_TILE, n), lambda i: (i, 0))
    full = lambda *shape: pl.BlockSpec(shape, lambda i: (0,) * len(shape))
    out_specs = [row(d), row(d)]
    out_shape = [jax.ShapeDtypeStruct((r, d), F32), jax.ShapeDtypeStruct((r, d), BF16)]
    if row_tiles:
        out_specs.append(pl.BlockSpec((ROW_TILE * SUBLANES, LANE), lambda i: (i, 0)))
        out_shape.append(jax.ShapeDtypeStruct((r * SUBLANES, LANE), F32))
    return pl.pallas_call(
        functools.partial(_proj_ln_kernel, alpha=alpha, row_tiles=row_tiles),
        grid=(r // ROW_TILE,),
        in_specs=[row(k), full(k, d), row(d), full(1, d), full(1, d)],
        out_specs=out_specs,
        out_shape=out_shape,
        scratch_shapes=[pltpu.VMEM((k, d), BF16)],
        compiler_params=_cp(("arbitrary",), VMEM_LIMIT),
        name="proj_ln",
    )(a, w, h, g, b)


def _ffn_ln_kernel(x_ref, w13_ref, w2_ref, h_ref, g_ref, b_ref, o_ref, obf_ref, *, alpha, d_ff, chunk):
    x = x_ref[...]
    acc = alpha * h_ref[...]
    for c in range(d_ff // chunk):
        gate = _dot(x, w13_ref[:, c * chunk:(c + 1) * chunk])
        up = _dot(x, w13_ref[:, d_ff + c * chunk:d_ff + (c + 1) * chunk])
        acc = acc + _dot((_silu(gate) * up).astype(BF16), w2_ref[c * chunk:(c + 1) * chunk, :])
    y = _layer_norm(acc, g_ref[...], b_ref[...])
    o_ref[...] = y
    obf_ref[...] = y.astype(BF16)


def _ffn_ln(x, w13, w2, h, g, b, alpha):
    r, d = x.shape
    d_ff = w2.shape[0]
    assert d_ff % FFN_CHUNK == 0
    row = lambda n: pl.BlockSpec((ROW_TILE, n), lambda i: (i, 0))
    full = lambda *shape: pl.BlockSpec(shape, lambda i: (0,) * len(shape))
    return pl.pallas_call(
        functools.partial(_ffn_ln_kernel, alpha=alpha, d_ff=d_ff, chunk=FFN_CHUNK),
        grid=(r // ROW_TILE,),
        in_specs=[row(d), full(d, 2 * d_ff), full(d_ff, d), row(d), full(1, d), full(1, d)],
        out_specs=[row(d), row(d)],
        out_shape=[jax.ShapeDtypeStruct((r, d), F32), jax.ShapeDtypeStruct((r, d), BF16)],
        compiler_params=_cp(("arbitrary",), VMEM_LIMIT),
        name="ffn_ln",
    )(x, w13, w2, h, g, b)


def _kv_kernel(x_ref, wkv_ref, g_ref, ca_ref, sa_ref, cb_ref, sb_ref, wuk_ref, wuv_ref, ones_ref,
               ckv_ref, kpe_ref, k_ref, v_ref, *, lora, rope, n_heads):
    y = _dot(x_ref[...], wkv_ref[...])
    cn = _rms_norm(y[:, :lora], g_ref[...])
    ckv_ref[...] = cn
    kpe_a = y[:, lora:lora + LANE] * ca_ref[...] + y[:, lora + LANE:lora + 2 * LANE] * sa_ref[...]
    kpe_ref[...] = kpe_a[:, :rope]
    kpe_b = y[:, lora + 2 * LANE:lora + 3 * LANE] * cb_ref[...] + y[:, lora + 3 * LANE:lora + 4 * LANE] * sb_ref[...]
    cb = cn.astype(BF16)
    kn = _dot(cb, wuk_ref[...])
    for h in range(n_heads):
        k_ref[:, h * LANE:(h + 1) * LANE] = (kn[:, h * LANE:(h + 1) * LANE] + kpe_b).astype(BF16)
    v_ref[...] = (_dot(cb, wuv_ref[...]) + ones_ref[...]).astype(BF16)


def _kv_proj(x, wkv, g, tabs, wuk, wuv, ones_row, lora, rope, n_heads):
    r, d = x.shape
    nv = wuv.shape[1]
    row = lambda n: pl.BlockSpec((ROW_TILE, n), lambda i: (i, 0))
    full = lambda *shape: pl.BlockSpec(shape, lambda i: (0,) * len(shape))
    return pl.pallas_call(
        functools.partial(_kv_kernel, lora=lora, rope=rope, n_heads=n_heads),
        grid=(r // ROW_TILE,),
        in_specs=[row(d), full(*wkv.shape), full(1, lora), row(LANE), row(LANE), row(LANE), row(LANE),
                  full(*wuk.shape), full(*wuv.shape), full(1, nv)],
        out_specs=[row(lora), row(rope), row(n_heads * LANE), row(nv)],
        out_shape=[jax.ShapeDtypeStruct((r, lora), F32), jax.ShapeDtypeStruct((r, rope), F32),
                   jax.ShapeDtypeStruct((r, n_heads * LANE), BF16), jax.ShapeDtypeStruct((r, nv), BF16)],
        compiler_params=_cp(("arbitrary",), VMEM_LIMIT),
        name="kv_proj",
    )(x, wkv, g, *tabs, wuk, wuv, ones_row)


def _q_kernel(x_ref, wdq_ref, g_ref, w1_ref, w2_ref, c_ref, s_ref, q_ref, *, n_heads):
    qc = _rms_norm(_dot(x_ref[...], wdq_ref[...]), g_ref[...]).astype(BF16)
    a = _dot(qc, w1_ref[...])
    b = _dot(qc, w2_ref[...])
    c = c_ref[...]
    s = s_ref[...]
    for h in range(n_heads):
        cols = slice(h * LANE, (h + 1) * LANE)
        q_ref[:, cols] = (a[:, cols] * c + b[:, cols] * s).astype(BF16)


def _q_proj(x, wdq, g, w1, w2, c_tab, s_tab, n_heads):
    r, d = x.shape
    ql = wdq.shape[1]
    row = lambda n: pl.BlockSpec((ROW_TILE, n), lambda i: (i, 0))
    full = lambda *shape: pl.BlockSpec(shape, lambda i: (0,) * len(shape))
    return pl.pallas_call(
        functools.partial(_q_kernel, n_heads=n_heads),
        grid=(r // ROW_TILE,),
        in_specs=[row(d), full(d, ql), full(1, ql), full(*w1.shape), full(*w2.shape), row(LANE), row(LANE)],
        out_specs=row(n_heads * LANE),
        out_shape=jax.ShapeDtypeStruct((r, n_heads * LANE), BF16),
        compiler_params=_cp(("arbitrary",), VMEM_LIMIT),
        name="q_proj",
    )(x, wdq, g, w1, w2, c_tab, s_tab)


def _q_sample_kernel(x_ref, wdq_ref, g_ref, w1_ref, wukt_ref, wpe_ref, wrot_ref, c_ref, s_ref,
                     qabs_ref, qpe_ref, *, n_heads, lora, scale):
    qc = _rms_norm(_dot(x_ref[...], wdq_ref[...]), g_ref[...]).astype(BF16)
    qn = (_dot(qc, w1_ref[...]) * scale).astype(BF16)
    for h in range(n_heads):
        qabs_ref[:, h * lora:(h + 1) * lora] = _dot(qn[:, h * LANE:(h + 1) * LANE], wukt_ref[h]).astype(BF16)
    qpe = _dot(qc, wpe_ref[...]) * c_ref[...] + _dot(qc, wrot_ref[...]) * s_ref[...]
    qpe_ref[...] = (qpe * scale).astype(BF16)


def _q_sample(x, wdq, g, w1, wukt, wpe, wrot, c_tab, s_tab, n_heads, lora, scale):
    ns = x.shape[0]
    args = (x, wdq, g, w1, wukt, wpe, wrot, c_tab, s_tab)
    full = lambda a: pl.BlockSpec(a.shape, lambda i, nd=a.ndim: (0,) * nd)
    return pl.pallas_call(
        functools.partial(_q_sample_kernel, n_heads=n_heads, lora=lora, scale=scale),
        grid=(1,),
        in_specs=[full(a) for a in args],
        out_specs=[pl.BlockSpec((ns, n_heads * lora), lambda i: (0, 0)),
                   pl.BlockSpec((ns, wpe.shape[1]), lambda i: (0, 0))],
        out_shape=[jax.ShapeDtypeStruct((ns, n_heads * lora), BF16),
                   jax.ShapeDtypeStruct((ns, wpe.shape[1]), BF16)],
        compiler_params=_cp(("arbitrary",), VMEM_LIMIT),
        name="q_sample",
    )(*args)


def _attn_step(m, acc, s, v):
    m_new = jnp.maximum(m, jnp.max(s, axis=-1, keepdims=True))
    p = jnp.exp2((s - m_new).astype(BF16))
    return m_new, jnp.exp2(m - m_new) * acc + _dot(p, v)


def _attn_prompt_kernel(q_ref, k_ref, v_ref, km_ref, vm_ref, o_ref, *, tq, tk, n_meta):
    i = pl.program_id(2)
    lane = lax.broadcasted_iota(jnp.int32, (tq, LANE), 1)
    heads = (slice(0, LANE), slice(LANE, 2 * LANE))
    qs = [q_ref[:, h] for h in heads]

    carry = []
    for q, h in zip(qs, heads):
        s = jnp.where(lane < n_meta, _dot_nt(q, km_ref[:, h]), -jnp.inf)
        m = jnp.max(s, axis=-1, keepdims=True)
        carry += [m, _dot(jnp.exp2((s - m).astype(BF16)), vm_ref[:, h])]

    def tile(j, carry, diag):
        off = pl.multiple_of(j * tk, tk)
        lo = 0 if diag is None else diag * tk
        out = []
        for e, (q, h) in enumerate(zip(qs, heads)):
            s = _dot_nt(q[lo:], k_ref[pl.ds(off, tk), h])
            if diag is not None:
                row = lax.broadcasted_iota(jnp.int32, (tk, tk), 0)
                col = lax.broadcasted_iota(jnp.int32, (tk, tk), 1)
                top = jnp.where(col <= row, s[:tk], -jnp.inf)
                s = jnp.concatenate([top, s[tk:]], axis=0) if tq - lo > tk else top
            m, acc = carry[2 * e], carry[2 * e + 1]
            m_new, acc_new = _attn_step(m[lo:], acc[lo:], s, v_ref[pl.ds(off, tk), h])
            if lo:
                m_new = jnp.concatenate([m[:lo], m_new], axis=0)
                acc_new = jnp.concatenate([acc[:lo], acc_new], axis=0)
            out += [m_new, acc_new]
        return out

    per_q = tq // tk
    carry = lax.fori_loop(0, i * per_q, lambda j, c: tuple(tile(j, c, None)), tuple(carry))
    for dj in range(per_q):
        carry = tile(i * per_q + dj, carry, dj)
    _, acc0, _, acc1 = carry
    l0 = jnp.sum(jnp.where(lane == LANE // 2, acc0, 0.0), axis=-1, keepdims=True)
    l1 = jnp.sum(jnp.where(lane == 0, acc1, 0.0), axis=-1, keepdims=True)
    o_ref[...] = jnp.where(lane < LANE // 2, acc0 * (1.0 / l0), acc1 * (1.0 / l1)).astype(BF16)


def _attn_prompt(q, k, v, bsz, seq, meta_row0, n_heads):
    tq, tk = min(ATTN_TQ, seq), ATTN_TK
    assert tq % tk == 0 and seq % tq == 0 and meta_row0 % LANE == 0 and n_heads % 2 == 0
    assert v.shape[1] == n_heads * LANE
    nq = seq // tq
    mb = meta_row0 // LANE
    pair = 2 * LANE
    return pl.pallas_call(
        functools.partial(_attn_prompt_kernel, tq=tq, tk=tk, n_meta=N_META),
        grid=(bsz, n_heads // 2, nq),
        in_specs=[
            pl.BlockSpec((tq, pair), lambda b, hp, i: (b * nq + i, hp)),
            pl.BlockSpec((seq, pair), lambda b, hp, i: (b, hp)),
            pl.BlockSpec((seq, pair), lambda b, hp, i: (b, hp)),
            pl.BlockSpec((LANE, pair), lambda b, hp, i: (mb, hp)),
            pl.BlockSpec((LANE, pair), lambda b, hp, i: (mb, hp)),
        ],
        out_specs=pl.BlockSpec((tq, LANE), lambda b, hp, i: (b * nq + i, hp)),
        out_shape=jax.ShapeDtypeStruct((bsz * seq, n_heads * LANE // 2), BF16),
        compiler_params=_cp(("arbitrary", "arbitrary", "arbitrary"), VMEM_LIMIT),
        name="attn_prompt",
    )(q, k, v, k, v)


def _softmax_part(s, v):
    m = jnp.max(s, axis=-1, keepdims=True)
    p = jnp.exp2(s - m)
    return m, jnp.sum(p, axis=-1, keepdims=True), _dot(p.astype(BF16), v)


def _softmax_merge(parts):
    m = functools.reduce(jnp.maximum, [p[0] for p in parts])
    w = [jnp.exp2(p[0] - m) for p in parts]
    l = functools.reduce(jnp.add, [wi * p[1] for wi, p in zip(w, parts)])
    acc = functools.reduce(jnp.add, [wi * p[2] for wi, p in zip(w, parts)])
    return m, l, acc


def _attn_paged_kernel(pt_ref, qabs_ref, qpe_ref, cnew_ref, pnew_ref, ckv_hbm, kpe_hbm, o_ref,
                       cbuf, pbuf, sem, m_ref, l_ref, acc_ref, *, ppc, n_groups, n_slots, n_heads, dec_seq, sub):
    c = pl.program_id(1)
    group = pl.program_id(0) * n_groups + c
    n_chunks = pl.num_programs(0) * n_groups * n_slots

    def copies(chunk, sl):
        base = chunk * ppc
        out = []
        for p in range(ppc):
            page = pt_ref[base + p]
            keys = pl.ds(p * PAGE_SIZE, PAGE_SIZE)
            out.append(pltpu.make_async_copy(ckv_hbm.at[page], cbuf.at[sl, keys, :], sem.at[0, sl]))
            out.append(pltpu.make_async_copy(kpe_hbm.at[page], pbuf.at[sl, :, keys], sem.at[1, sl]))
        return out

    def wait_slot(sl):
        pltpu.make_async_copy(cbuf.at[sl], cbuf.at[sl], sem.at[0, sl]).wait()
        pltpu.make_async_copy(pbuf.at[sl], pbuf.at[sl], sem.at[1, sl]).wait()

    @pl.when(group == 0)
    def _():
        for k in range(n_slots - 1):
            for cp in copies(k, k):
                cp.start()

    @pl.when(c == 0)
    def _():
        m_ref[...] = jnp.full(m_ref.shape, -jnp.inf, F32)
        l_ref[...] = jnp.zeros(l_ref.shape, F32)
        acc_ref[...] = jnp.zeros(acc_ref.shape, F32)

    qa = qabs_ref[0]
    qp = qpe_ref[0]
    n_sub = ppc * PAGE_SIZE // sub

    def slot_parts(sl):
        def scores(i):
            keys = slice(i * sub, (i + 1) * sub)
            kc = cbuf[sl, keys, :].astype(BF16)
            return _dot_nt(qa, kc) + _dot(qp, pbuf[sl, :, keys].astype(BF16)), kc

        ahead = [scores(i) for i in range(min(PAGED_SCORES_AHEAD, n_sub))]
        out = []
        for i in range(n_sub):
            s, kc = ahead.pop(0)
            if i + PAGED_SCORES_AHEAD < n_sub:
                ahead.append(scores(i + PAGED_SCORES_AHEAD))
            out.append(_softmax_part(s, kc))
        return out

    parts = [(m_ref[...], l_ref[...], acc_ref[...])]
    for k in range(n_slots):
        wait_slot(k)
        nxt = group * n_slots + (k + n_slots - 1)
        for cp in copies(jnp.where(nxt < n_chunks, nxt, 0), (k + n_slots - 1) % n_slots):
            cp.start()
        parts += slot_parts(k)
    m, l, acc = _softmax_merge(parts)
    m_ref[...] = m
    l_ref[...] = l
    acc_ref[...] = acc

    @pl.when(c == n_groups - 1)
    def _():
        kn = cnew_ref[0]
        s = _dot_nt(qa, kn) + _dot_nt(qp, pnew_ref[0])
        rows = s.shape[0]
        tok = lax.broadcasted_iota(jnp.int32, (rows, LANE), 0) // n_heads
        col = lax.broadcasted_iota(jnp.int32, (rows, LANE), 1)
        s = jnp.where((col <= tok) & (col < dec_seq), s, -jnp.inf)
        _, l2, acc2 = _softmax_merge([(m, l, acc), _softmax_part(s, kn)])
        o_ref[0] = acc2 * (1.0 / l2)

    @pl.when(group == pl.num_programs(0) * n_groups - 1)
    def _():
        for k in range(n_slots - 1):
            wait_slot(k)


def _attn_paged(page_table, qabs, qpe, cnew, pnew, cache_ckv, cache_kpe_t, n_heads, dec_seq):
    dec_b, n_pages = page_table.shape
    n_slots = PAGED_SLOTS
    assert n_pages % n_slots == 0
    ppc = min(PAGES_PER_CHUNK, n_pages // n_slots)
    sub = min(PAGED_KEY_BLOCK, ppc * PAGE_SIZE)
    assert n_pages % (n_slots * ppc) == 0 and cache_ckv.shape[1] == PAGE_SIZE and (ppc * PAGE_SIZE) % sub == 0
    n_groups = n_pages // (n_slots * ppc)
    rows, lora = qabs.shape[1], qabs.shape[2]
    rope = qpe.shape[2]
    kern = functools.partial(_attn_paged_kernel, ppc=ppc, n_groups=n_groups, n_slots=n_slots, n_heads=n_heads,
                             dec_seq=dec_seq, sub=sub)
    grid_spec = pltpu.PrefetchScalarGridSpec(
        num_scalar_prefetch=1,
        grid=(dec_b, n_groups),
        in_specs=[
            pl.BlockSpec((1, rows, lora), lambda b, c, pt: (b, 0, 0)),
            pl.BlockSpec((1, rows, rope), lambda b, c, pt: (b, 0, 0)),
            pl.BlockSpec((1, LANE, lora), lambda b, c, pt: (b, 0, 0)),
            pl.BlockSpec((1, LANE, rope), lambda b, c, pt: (b, 0, 0)),
            pl.BlockSpec(memory_space=pl.ANY),
            pl.BlockSpec(memory_space=pl.ANY),
        ],
        out_specs=pl.BlockSpec((1, rows, lora), lambda b, c, pt: (b, 0, 0)),
        scratch_shapes=[
            pltpu.VMEM((n_slots, ppc * PAGE_SIZE, lora), F32),
            pltpu.VMEM((n_slots, rope, ppc * PAGE_SIZE), F32),
            pltpu.SemaphoreType.DMA((2, n_slots)),
            pltpu.VMEM((rows, 1), F32),
            pltpu.VMEM((rows, 1), F32),
            pltpu.VMEM((rows, lora), F32),
        ],
    )
    return pl.pallas_call(
        kern,
        grid_spec=grid_spec,
        out_shape=jax.ShapeDtypeStruct((dec_b, rows, lora), F32),
        compiler_params=_cp(("arbitrary", "arbitrary"), VMEM_LIMIT),
        name="attn_paged",
    )(page_table.reshape(-1), qabs, qpe, cnew, pnew, cache_ckv, cache_kpe_t)


def _uv_sample_kernel(o_ref, w_ref, out_ref, *, n_heads, lora):
    for p in range(n_heads // 2):
        acc = _dot(o_ref[:, (2 * p) * lora:(2 * p + 1) * lora].astype(BF16), w_ref[2 * p])
        acc = acc + _dot(o_ref[:, (2 * p + 1) * lora:(2 * p + 2) * lora].astype(BF16), w_ref[2 * p + 1])
        out_ref[:, p * LANE:(p + 1) * LANE] = acc.astype(BF16)


def _uv_sample(o_lat, wuv_pad, n_heads, lora):
    ns = o_lat.shape[0]
    nv = n_heads * LANE // 2
    return pl.pallas_call(
        functools.partial(_uv_sample_kernel, n_heads=n_heads, lora=lora),
        grid=(1,),
        in_specs=[pl.BlockSpec(o_lat.shape, lambda i: (0, 0)), pl.BlockSpec(wuv_pad.shape, lambda i: (0, 0, 0))],
        out_specs=pl.BlockSpec((ns, nv), lambda i: (0, 0)),
        out_shape=jax.ShapeDtypeStruct((ns, nv), BF16),
        compiler_params=_cp(("arbitrary",), VMEM_LIMIT),
        name="uv_sample",
    )(o_lat, wuv_pad)


def _router_kernel(x_ref, w_ref, b_ref, o_ref, *, n_experts):
    logits = _dot(x_ref[...], w_ref[...]) + b_ref[...]
    lane = lax.broadcasted_iota(jnp.int32, logits.shape, 1).astype(F32)
    logits = jnp.where(lane < n_experts, logits, -jnp.inf)
    m1 = jnp.max(logits, axis=-1, keepdims=True)
    i1 = jnp.min(jnp.where(logits == m1, lane, float(LANE)), axis=-1, keepdims=True)
    rest = jnp.where(lane == i1, -jnp.inf, logits)
    m2 = jnp.max(rest, axis=-1, keepdims=True)
    i2 = jnp.min(jnp.where(rest == m2, lane, float(LANE)), axis=-1, keepdims=True)
    e2 = jnp.exp(m2 - m1)
    den = 1.0 + e2
    g1 = 1.0 / den
    g2 = e2 / den
    o_ref[...] = jnp.where(lane == 0, i1, jnp.where(lane == 1, i2, jnp.where(lane == 2, g1, jnp.where(lane == 3, g2, 0.0))))


def _router(x, w_pad, b_pad, n_experts):
    r, d = x.shape
    return pl.pallas_call(
        functools.partial(_router_kernel, n_experts=n_experts),
        grid=(r // ROW_TILE,),
        in_specs=[pl.BlockSpec((ROW_TILE, d), lambda i: (i, 0)), pl.BlockSpec((d, LANE), lambda i: (0, 0)),
                  pl.BlockSpec((1, LANE), lambda i: (0, 0))],
        out_specs=pl.BlockSpec((ROW_TILE, LANE), lambda i: (i, 0)),
        out_shape=jax.ShapeDtypeStruct((r, LANE), F32),
        compiler_params=_cp(("arbitrary",), VMEM_LIMIT),
        name="router",
    )(x, w_pad, b_pad)


def _moe_ffn_kernel(te_ref, nu_ref, src_ref, x_hbm, wg_ref, wu_ref, w2_ref, o_ref, xbuf, xb_ref, acc_ref, sem,
                    *, tm, rows_per_step):
    t = pl.program_id(0)
    f = pl.program_id(1)
    nf = pl.num_programs(1)
    used = t < nu_ref[0]
    slot = t % 2
    n_rows = rows_per_step * nf

    def row_copy(tile, r, sl):
        row = src_ref[tile * tm + r]
        return pltpu.make_async_copy(x_hbm.at[pl.ds(row * SUBLANES, SUBLANES), :],
                                     xbuf.at[sl, pl.ds(r * SUBLANES, SUBLANES), :], sem.at[sl])

    @pl.when((t == 0) & (f == 0))
    def _():
        def body(g, carry):
            for u in range(GATHER_UNROLL):
                row_copy(0, g * GATHER_UNROLL + u, 0).start()
            return carry
        lax.fori_loop(0, n_rows // GATHER_UNROLL, body, 0)

    @pl.when(f == 0)
    def _():
        pltpu.make_async_copy(xbuf.at[slot], xbuf.at[slot], sem.at[slot]).wait()
        for s, slab in enumerate(_load_row_tiles(xbuf, (slot,), tm)):
            xb_ref[:, s * LANE:(s + 1) * LANE] = slab.astype(BF16)

    def prefetch():
        for u in range(rows_per_step):
            row_copy(t + 1, f * rows_per_step + u, 1 - slot).start()

    @pl.when(used)
    def _():
        prefetch()
        x = xb_ref[...]
        y = None
        for c in range(wg_ref.shape[2] // FFN_CHUNK):
            cols = slice(c * FFN_CHUNK, (c + 1) * FFN_CHUNK)
            hcol = (_silu(_dot(x, wg_ref[0, :, cols])) * _dot(x, wu_ref[0, :, cols])).astype(BF16)
            part = _dot(hcol, w2_ref[0, cols, :])
            y = part if y is None else y + part

        @pl.when(f == 0)
        def _():
            acc_ref[...] = y

        @pl.when(f > 0)
        def _():
            acc_ref[...] += y

    @pl.when(jnp.logical_not(used))
    def _():
        prefetch()

    @pl.when(f == nf - 1)
    def _():
        _store_row_tiles(o_ref, jnp.where(used, acc_ref[...], 0.0))

    @pl.when((t == pl.num_programs(0) - 1) & (f == nf - 1))
    def _():
        pltpu.make_async_copy(xbuf.at[1 - slot], xbuf.at[1 - slot], sem.at[1 - slot]).wait()


def _moe_ffn(tile_expert, n_used, src, x, w13, w2):
    d = w13.shape[1]
    assert x.shape[1] == LANE and d == SUBLANES * LANE
    n_exp, eff = w2.shape[0], w2.shape[1]
    tm, tf = MOE_ROW_TILE, min(MOE_FF_CHUNK, eff)
    n = src.shape[0] - 2 * tm
    assert n % tm == 0 and eff % tf == 0 and tf % FFN_CHUNK == 0
    nf = eff // tf
    rows_per_step = -(-tm // (nf * GATHER_UNROLL)) * GATHER_UNROLL
    assert rows_per_step * nf <= 2 * tm

    def widx(base):
        def index_map(t, f, te, nu, src):
            live = t < nu[0]
            return (te[t], 0, jnp.where(live, f, nf - 1) + base)
        return index_map

    def w2idx(t, f, te, nu, src):
        return (te[t], jnp.where(t < nu[0], f, nf - 1), 0)

    grid_spec = pltpu.PrefetchScalarGridSpec(
        num_scalar_prefetch=3,
        grid=(n // tm, nf),
        in_specs=[
            pl.BlockSpec(memory_space=pl.ANY),
            pl.BlockSpec((1, d, tf), widx(0)),
            pl.BlockSpec((1, d, tf), widx(nf)),
            pl.BlockSpec((1, tf, d), w2idx),
        ],
        out_specs=pl.BlockSpec((tm * SUBLANES, LANE), lambda t, f, te, nu, src: (t, 0)),
        scratch_shapes=[
            pltpu.VMEM((2, rows_per_step * nf * SUBLANES, LANE), F32),
            pltpu.VMEM((tm, d), BF16),
            pltpu.VMEM((tm, d), F32),
            pltpu.SemaphoreType.DMA((2,)),
        ],
    )
    return pl.pallas_call(
        functools.partial(_moe_ffn_kernel, tm=tm, rows_per_step=rows_per_step),
        grid_spec=grid_spec,
        out_shape=jax.ShapeDtypeStruct((n * SUBLANES, LANE), F32),
        compiler_params=_cp(("arbitrary", "arbitrary"), VMEM_LIMIT),
        name="moe_ffn",
    )(tile_expert, n_used, src, x, w13, w13, w2)


def _combine_ln_kernel(p1_ref, p2_ref, y_hbm, h_ref, route_ref, g_ref, b_ref, o_ref, buf, sem, *, tile, alpha):
    i = pl.program_id(0)
    n_tiles = 2 * pl.num_programs(0)

    def row_copies(t, r, sl):
        a = p1_ref[t * tile + r]
        b = p2_ref[t * tile + r]
        dst = pl.ds(r * SUBLANES, SUBLANES)
        return (pltpu.make_async_copy(y_hbm.at[pl.ds(a * SUBLANES, SUBLANES), :], buf.at[sl, 0, dst, :], sem.at[sl]),
                pltpu.make_async_copy(y_hbm.at[pl.ds(b * SUBLANES, SUBLANES), :], buf.at[sl, 1, dst, :], sem.at[sl]))

    @pl.when(i == 0)
    def _():
        def body(g, carry):
            for u in range(GATHER_UNROLL):
                for cp in row_copies(0, g * GATHER_UNROLL + u, 0):
                    cp.start()
            return carry
        lax.fori_loop(0, tile // GATHER_UNROLL, body, 0)

    def wait_slot(sl):
        pltpu.make_async_copy(buf.at[sl], buf.at[sl], sem.at[sl]).wait()

    def prefetch(t, sl):
        t = jnp.minimum(t, n_tiles - 1)
        for r in range(tile):
            for cp in row_copies(t, r, sl):
                cp.start()

    def finish(sl):
        rows = slice(sl * tile, (sl + 1) * tile)
        route = route_ref[rows, :]
        g1, g2 = route[:, 2:3], route[:, 3:4]
        ya = _load_row_tiles(buf, (sl, 0), tile)
        yb = _load_row_tiles(buf, (sl, 1), tile)
        z = jnp.concatenate([alpha * h_ref[rows, s * LANE:(s + 1) * LANE] + (g1 * ya[s] + g2 * yb[s])
                             for s in range(SUBLANES)], axis=1)
        o_ref[rows, :] = _layer_norm(z, g_ref[...], b_ref[...])

    wait_slot(0)
    prefetch(2 * i + 1, 1)
    finish(0)
    wait_slot(1)
    prefetch(2 * i + 2, 0)
    finish(1)

    @pl.when(i == pl.num_programs(0) - 1)
    def _():
        wait_slot(0)


def _combine_ln(pos1, pos2, y_sorted, h, route, g, b, alpha):
    r, d = h.shape
    tile = GATHER_TILE
    assert r % (2 * tile) == 0 and tile % GATHER_UNROLL == 0 and d == SUBLANES * LANE
    grid_spec = pltpu.PrefetchScalarGridSpec(
        num_scalar_prefetch=2,
        grid=(r // (2 * tile),),
        in_specs=[
            pl.BlockSpec(memory_space=pl.ANY),
            pl.BlockSpec((2 * tile, d), lambda i, p1, p2: (i, 0)),
            pl.BlockSpec((2 * tile, LANE), lambda i, p1, p2: (i, 0)),
            pl.BlockSpec((1, d), lambda i, p1, p2: (0, 0)),
            pl.BlockSpec((1, d), lambda i, p1, p2: (0, 0)),
        ],
        out_specs=pl.BlockSpec((2 * tile, d), lambda i, p1, p2: (i, 0)),
        scratch_shapes=[pltpu.VMEM((2, 2, tile * SUBLANES, LANE), F32), pltpu.SemaphoreType.DMA((2,))],
    )
    return pl.pallas_call(
        functools.partial(_combine_ln_kernel, tile=tile, alpha=alpha),
        grid_spec=grid_spec,
        out_shape=jax.ShapeDtypeStruct((r, d), F32),
        compiler_params=_cp(("arbitrary",), VMEM_LIMIT),
        name="combine_ln",
    )(pos1, pos2, y_sorted, h, route, g, b)


def _route_plan(route, n_experts, tm):
    r = route.shape[0]
    e = jnp.concatenate([route[:, 0], route[:, 1]]).astype(jnp.int32)
    onehot = (e[:, None] == jnp.arange(n_experts, dtype=jnp.int32)[None, :]).astype(jnp.int32)
    csum = jnp.cumsum(onehot, axis=0)
    rank = jnp.take_along_axis(csum, e[:, None], axis=1)[:, 0] - 1
    tiles = (csum[-1] + tm - 1) // tm
    tile_end = jnp.cumsum(tiles)
    pos = (tile_end - tiles)[e] * tm + rank
    n_tiles = (TOP_K * r) // tm + n_experts
    tok = jnp.arange(TOP_K * r, dtype=jnp.int32) % r
    src = jnp.zeros(((n_tiles + 2) * tm,), jnp.int32).at[pos].set(tok, unique_indices=True)
    n_used = tile_end[-1]
    tile_id = jnp.minimum(jnp.arange(n_tiles, dtype=jnp.int32), n_used - 1)
    tile_expert = jnp.sum((tile_end[None, :] <= tile_id[:, None]).astype(jnp.int32), axis=1)
    return src, tile_expert, n_used.reshape(1).astype(jnp.int32), pos[:r], pos[r:]


def _pad_cols(w, offset, width):
    return jnp.pad(w, ((0, 0), (offset, width - offset - w.shape[1])))


def kernel(x_prompt, x_sample, state_ret, cache_ckv, cache_kpe, page_table, meta_tokens, ln_g, ln_b, ret_w_in, ret_gn_g, ret_w_o, mla_w_dq, mla_q_norm, mla_w_uq, mla_w_o, kv_w_a, kv_norm, kv_w_b, ffn_w13, ffn_w2, moe_w_r, moe_b_r, moe_w13, moe_w2):
    bsz, seq, d = x_prompt.shape
    dec_b, dec_seq, _ = x_sample.shape
    n_ret, _, ret_h, dk, dv = state_ret.shape
    depth = ln_g.shape[0]
    assert depth == 2 and n_ret == 1 and mla_w_dq.shape[0] == 1, "layer pattern: one retention layer then one MLA layer"
    past_len = page_table.shape[1] * PAGE_SIZE
    lora, mla_h, nope_v = kv_w_b.shape
    rope = kv_w_a.shape[1] - lora
    nope = mla_w_uq.shape[2] // mla_h - rope
    v_dim = nope_v - nope
    n_experts = moe_w_r.shape[2]
    assert nope == LANE // 2 and v_dim == LANE // 2 and rope <= LANE // 4
    alpha = (2 * depth) ** 0.25
    mla_scale = (nope + rope) ** -0.5
    half = rope // 2

    n_p, n_s = bsz * seq, dec_b * dec_seq
    meta_row0 = n_p + n_s
    r_real = meta_row0 + N_META
    r_pad = -(-r_real // ROW_TILE) * ROW_TILE
    assert r_pad - meta_row0 >= LANE

    x0 = jnp.concatenate([x_prompt.reshape(n_p, d), x_sample.reshape(n_s, d), meta_tokens,
                          jnp.zeros((r_pad - r_real, d), F32)], axis=0)
    pos = jnp.concatenate([jnp.tile(N_META + jnp.arange(seq), bsz), jnp.tile(past_len + jnp.arange(dec_seq), dec_b),
                           jnp.arange(N_META), jnp.zeros((r_pad - r_real,), jnp.int32)]).astype(F32)

    inv_r = ROPE_BASE ** (-jnp.arange(dk // 2, dtype=F32) / (dk // 2))
    ang_r = pos[:, None] * inv_r[None, :]
    cos_r, sin_r = jnp.cos(ang_r), jnp.sin(ang_r)
    inv_m = ROPE_BASE ** (-jnp.arange(half, dtype=F32) / half)
    ang_m = pos[:, None] * inv_m[None, :]
    cos_m = jnp.concatenate([jnp.cos(ang_m)] * 2, axis=1)
    sin_m = jnp.concatenate([jnp.sin(ang_m)] * 2, axis=1)
    cos_a, sin_a = _pad_cols(cos_m, 0, LANE), _pad_cols(sin_m, 0, LANE)
    cos_b, sin_b = _pad_cols(cos_m, nope, LANE), _pad_cols(sin_m, nope, LANE)
    q_scale = mla_scale * LOG2E
    q_ctab = (_pad_cols(jnp.ones((r_pad, nope), F32), 0, LANE) + cos_b) * q_scale
    q_stab = sin_b * q_scale

    def rot(w):
        return jnp.concatenate([-w[..., half:], w[..., :half]], axis=-1)

    qkvg = _ret_proj(x0, ret_w_in[0], cos_r, sin_r, ret_h, dk)
    gn = ret_gn_g[0].reshape(1, -1)
    mix_p, mix_meta, st_p = _ret_prompt(qkvg, gn, _decay_tables(ret_h, dk, RET_CHUNK, RET_CHUNK),
                                        bsz, seq, meta_row0, ret_h, dk, dv)
    mix_s, st_s = _ret_sample(qkvg, state_ret[0], gn,
                              _decay_tables(ret_h, dk, SAMPLE_BATCH_TILE * dec_seq, dec_seq),
                              n_p, dec_b, dec_seq, ret_h, dk, dv)
    mix = jnp.concatenate([mix_p, mix_s, mix_meta, jnp.zeros((r_pad - r_real, mix_p.shape[1]), BF16)], axis=0)
    row = lambda v: v.reshape(1, -1)
    h1, h1b = _proj_ln(mix, ret_w_o[0], x0, row(ln_g[0, 0]), row(ln_b[0, 0]), alpha)
    h2, h2b = _ffn_ln(h1b, ffn_w13[0].astype(BF16), ffn_w2[0].astype(BF16), h1, row(ln_g[0, 1]), row(ln_b[0, 1]), alpha)

    wc, wpe = kv_w_a[:, :lora], kv_w_a[:, lora:]
    wkv = jnp.concatenate([wc, _pad_cols(wpe, 0, LANE), _pad_cols(rot(wpe), 0, LANE),
                           _pad_cols(wpe, nope, LANE), _pad_cols(rot(wpe), nope, LANE)], axis=1).astype(BF16)
    wuk = jnp.pad(kv_w_b[:, :, :nope], ((0, 0), (0, 0), (0, LANE - nope))).reshape(lora, mla_h * LANE).astype(BF16)
    even = (jnp.arange(mla_h) % 2 == 0)[None, :, None]
    wuv_h = kv_w_b[:, :, nope:]
    wuv = jnp.where(even, jnp.pad(wuv_h, ((0, 0), (0, 0), (0, LANE - v_dim))),
                    jnp.pad(wuv_h, ((0, 0), (0, 0), (LANE - v_dim, 0)))).reshape(lora, mla_h * LANE).astype(BF16)
    lane_id = jnp.arange(LANE)[None, None, :]
    ones_row = jnp.where(even, lane_id == v_dim, lane_id == 0).astype(F32).reshape(1, mla_h * LANE)
    ckv, kpe, k_heads, v_heads = _kv_proj(h2b, wkv, row(kv_norm), (cos_a, sin_a, cos_b, sin_b), wuk, wuv, ones_row,
                                          lora, rope, mla_h)

    wuq = mla_w_uq[0].reshape(-1, mla_h, nope + rope)
    wq_n, wq_r = wuq[:, :, :nope], wuq[:, :, nope:]
    pad_h = lambda w, off: jnp.pad(w, ((0, 0), (0, 0), (off, LANE - off - w.shape[2]))).reshape(w.shape[0], mla_h * LANE)
    w_q1 = (pad_h(wq_n, 0) + pad_h(wq_r, nope)).astype(BF16)
    w_q2 = pad_h(rot(wq_r), nope).astype(BF16)
    wdq = mla_w_dq[0].astype(BF16)
    qn_g = row(mla_q_norm[0])
    q_heads = _q_proj(h2b, wdq, qn_g, w_q1, w_q2, q_ctab, q_stab, mla_h)
    attn_p = _attn_prompt(q_heads, k_heads, v_heads, bsz, seq, meta_row0, mla_h)

    wukt = jnp.pad(jnp.transpose(kv_w_b[:, :, :nope], (1, 2, 0)), ((0, 0), (0, LANE - nope), (0, 0))).astype(BF16)
    w_pe = wq_r.reshape(-1, mla_h * rope).astype(BF16)
    w_pe_rot = rot(wq_r).reshape(-1, mla_h * rope).astype(BF16)
    cs = jnp.tile(cos_m[n_p:n_p + n_s], (1, mla_h))
    ss = jnp.tile(sin_m[n_p:n_p + n_s], (1, mla_h))
    qabs, qpe = _q_sample(h2b[n_p:n_p + n_s], wdq, qn_g, pad_h(wq_n, 0).astype(BF16), wukt, w_pe, w_pe_rot, cs, ss,
                          mla_h, lora, q_scale)
    ckv_s, kpe_s = ckv[n_p:n_p + n_s], kpe[n_p:n_p + n_s]
    new_pad = lambda a: jnp.pad(a.reshape(dec_b, dec_seq, -1), ((0, 0), (0, LANE - dec_seq), (0, 0))).astype(BF16)
    o_lat = _attn_paged(page_table, qabs.reshape(dec_b, dec_seq * mla_h, lora), qpe.reshape(dec_b, dec_seq * mla_h, rope),
                        new_pad(ckv_s), new_pad(kpe_s), cache_ckv, jnp.swapaxes(cache_kpe, 1, 2), mla_h, dec_seq)
    wuv_t = jnp.transpose(wuv_h, (1, 0, 2))
    wuv_pad = jnp.where(jnp.transpose(even, (1, 0, 2)), jnp.pad(wuv_t, ((0, 0), (0, 0), (0, LANE - v_dim))),
                        jnp.pad(wuv_t, ((0, 0), (0, 0), (LANE - v_dim, 0)))).astype(BF16)
    attn_s = _uv_sample(o_lat.reshape(n_s, mla_h * lora), wuv_pad, mla_h, lora)
    attn = jnp.concatenate([attn_p, attn_s, jnp.zeros((r_pad - meta_row0, attn_p.shape[1]), BF16)], axis=0)
    h3, h3b, h3_tiles = _proj_ln(attn, mla_w_o[0], h2, row(ln_g[1, 0]), row(ln_b[1, 0]), alpha, row_tiles=True)

    w_r = _pad_cols(moe_w_r[0], 0, LANE).astype(BF16)
    b_r = _pad_cols(moe_b_r[0].reshape(1, -1), 0, LANE)
    route = _router(h3b, w_r, b_r, n_experts)
    src, tile_expert, n_used, pos1, pos2 = _route_plan(route, n_experts, MOE_ROW_TILE)
    y_sorted = _moe_ffn(tile_expert, n_used, src, h3_tiles, moe_w13[0].astype(BF16), moe_w2[0].astype(BF16))
    h4 = _combine_ln(pos1, pos2, y_sorted, h3, route, row(ln_g[1, 1]), row(ln_b[1, 1]), alpha)

    y_prompt = h4[:n_p].reshape(bsz, seq, d)
    y_sample = h4[n_p:n_p + n_s].reshape(dec_b, dec_seq, d)

    def with_meta(a):
        meta = jnp.broadcast_to(a[meta_row0:r_real][None], (bsz, N_META, a.shape[1]))
        return jnp.concatenate([meta, a[:n_p].reshape(bsz, seq, -1)], axis=1)

    return (y_prompt, y_sample, st_p[None], with_meta(ckv), with_meta(kpe), st_s[None],
            ckv_s.reshape(dec_b, dec_seq, lora), kpe_s.reshape(dec_b, dec_seq, rope))
```

```python
import functools
import math

import jax
import jax.numpy as jnp
from jax import lax
from jax.experimental import pallas as pl
from jax.experimental.pallas import tpu as pltpu

F32 = jnp.float32
BF16 = jnp.bfloat16

N_META = 16
TOP_K = 2
ROPE_BASE = 10000.0
LN_EPS = 1e-5
RMS_EPS = 1e-6
PAGE_SIZE = 128
RET_CHUNK = 128

LANE = 128
SUBLANES = 8
ROW_TILE = 512
RET_PROJ_TILE = 1024
ATTN_TQ = 1024
ATTN_TK = 512
FFN_CHUNK = 256
MOE_ROW_TILE = 512
MOE_FF_CHUNK = 1792
GATHER_TILE = 256
GATHER_UNROLL = 8
PAGES_PER_CHUNK = 32
PAGED_SLOTS = 4
PAGED_KEY_BLOCK = 1024
PAGED_SCORES_AHEAD = 8
LOG2E = math.log2(math.e)
SAMPLE_BATCH_TILE = 4
VMEM_LIMIT = 56 * 1024 * 1024


def _cp(sem, vmem=None):
    return pltpu.CompilerParams(dimension_semantics=sem, vmem_limit_bytes=vmem)


def _layer_norm(z, g, b):
    mu = jnp.mean(z, axis=-1, keepdims=True)
    zc = z - mu
    var = jnp.mean(zc * zc, axis=-1, keepdims=True)
    return zc * lax.rsqrt(var + LN_EPS) * g + b


def _rms_norm(c, g):
    return c * lax.rsqrt(jnp.mean(c * c, axis=-1, keepdims=True) + RMS_EPS) * g


def _silu(x):
    return x * (1.0 / (1.0 + jnp.exp(-x)))


def _dot(a, b):
    return jnp.dot(a, b, preferred_element_type=F32)


def _dot_nt(a, b):
    return lax.dot_general(a, b, (((1,), (1,)), ((), ())), preferred_element_type=F32)


def _dot_tn(a, b):
    return lax.dot_general(a, b, (((0,), (0,)), ((), ())), preferred_element_type=F32)


def _ret_proj_kernel(x_ref, w_ref, cos_ref, sin_ref, o_ref, wbf_ref, *, n_heads, dk, k_scale):
    j = pl.program_id(0)
    i = pl.program_id(1)

    @pl.when(i == 0)
    def _():
        wbf_ref[...] = w_ref[...].astype(BF16)

    y = _dot(x_ref[...].astype(BF16), wbf_ref[...])
    half = dk // 2

    @pl.when(j < 2)
    def _():
        c = cos_ref[...]
        s = sin_ref[...]
        scale = jnp.where(j == 1, k_scale, 1.0).astype(F32)
        for h in range(n_heads):
            x1 = y[:, h * dk:h * dk + half]
            x2 = y[:, h * dk + half:(h + 1) * dk]
            o_ref[:, h * dk:h * dk + half] = ((x1 * c - x2 * s) * scale).astype(BF16)
            o_ref[:, h * dk + half:(h + 1) * dk] = ((x1 * s + x2 * c) * scale).astype(BF16)

    @pl.when(j >= 2)
    def _():
        o_ref[...] = y.astype(BF16)


def _ret_proj(x, w_in, cos_r, sin_r, n_heads, dk):
    r, d = x.shape
    n = w_in.shape[1]
    qk = n_heads * dk
    assert n % qk == 0 and dk // 2 == LANE
    kern = functools.partial(_ret_proj_kernel, n_heads=n_heads, dk=dk, k_scale=dk ** -0.5)
    tile = RET_PROJ_TILE if r % RET_PROJ_TILE == 0 else ROW_TILE
    return pl.pallas_call(
        kern,
        grid=(n // qk, r // tile),
        in_specs=[
            pl.BlockSpec((tile, d), lambda j, i: (i, 0)),
            pl.BlockSpec((d, qk), lambda j, i: (0, j)),
            pl.BlockSpec((tile, LANE), lambda j, i: (i, 0)),
            pl.BlockSpec((tile, LANE), lambda j, i: (i, 0)),
        ],
        out_specs=pl.BlockSpec((tile, qk), lambda j, i: (i, j)),
        out_shape=jax.ShapeDtypeStruct((r, n), BF16),
        scratch_shapes=[pltpu.VMEM((d, qk), BF16)],
        compiler_params=_cp(("arbitrary", "arbitrary"), VMEM_LIMIT),
        name="ret_proj",
    )(x, w_in, cos_r, sin_r)


def _group_norm_gate(o, g, gn):
    mu = jnp.mean(o, axis=-1, keepdims=True)
    oc = o - mu
    var = jnp.mean(oc * oc, axis=-1, keepdims=True)
    return (_silu(g.astype(F32)) * (oc * lax.rsqrt(var + LN_EPS) * gn)).astype(BF16)


def _ret_prompt_kernel(q_ref, k_ref, v_ref, g_ref, qm_ref, km_ref, vm_ref, gm_ref, gn_ref,
                       dec_ref, rdec_ref, cdec_ref, cpow_ref,
                       o_ref, om_ref, st_ref, s_ref, *, n_heads, dk, dv, chunk, n_meta, tile):
    t = pl.program_id(1)
    nt = pl.num_programs(1)

    @pl.when(t == 0)
    def _():
        for h in range(n_heads):
            q = qm_ref[:, h * dk:(h + 1) * dk]
            k = km_ref[:, h * dk:(h + 1) * dk]
            v = vm_ref[:, h * dv:(h + 1) * dv]
            sc = _dot_nt(q, k) * dec_ref[h, :n_meta, :n_meta]
            o = _dot(sc.astype(BF16), v)
            om_ref[:, h * dv:(h + 1) * dv] = _group_norm_gate(
                o, gm_ref[:, h * dv:(h + 1) * dv], gn_ref[:, h * dv:(h + 1) * dv])
            k_out = (k.astype(F32) * cdec_ref[h, chunk - n_meta:, :]).astype(BF16)
            s_ref[h] = _dot_tn(k_out, v)

    for h in range(n_heads):
        for c in range(tile // chunk):
            rows = slice(c * chunk, (c + 1) * chunk)
            q = q_ref[rows, h * dk:(h + 1) * dk]
            k = k_ref[rows, h * dk:(h + 1) * dk]
            v = v_ref[rows, h * dv:(h + 1) * dv]
            s_old = s_ref[h]
            sc = _dot_nt(q, k) * dec_ref[h]
            q_in = (q.astype(F32) * rdec_ref[h]).astype(BF16)
            o = _dot(sc.astype(BF16), v) + _dot(q_in, s_old.astype(BF16))
            k_out = (k.astype(F32) * cdec_ref[h]).astype(BF16)
            s_ref[h] = cpow_ref[h] * s_old + _dot_tn(k_out, v)
            o_ref[rows, h * dv:(h + 1) * dv] = _group_norm_gate(
                o, g_ref[rows, h * dv:(h + 1) * dv], gn_ref[:, h * dv:(h + 1) * dv])

    @pl.when(t == nt - 1)
    def _():
        st_ref[0] = s_ref[...]


def _ret_prompt(qkvg, gn_g, tabs, bsz, seq, meta_row0, n_heads, dk, dv):
    qk, vw = n_heads * dk, n_heads * dv
    tile = ROW_TILE
    assert seq % tile == 0 and tile % RET_CHUNK == 0 and meta_row0 % N_META == 0 and vw == 2 * qk
    nt = seq // tile
    mb = meta_row0 // N_META
    dec, rdec, cdec, cpow = tabs
    kern = functools.partial(_ret_prompt_kernel, n_heads=n_heads, dk=dk, dv=dv, chunk=RET_CHUNK,
                             n_meta=N_META, tile=tile)
    full = lambda *shape: pl.BlockSpec(shape, lambda b, t: (0,) * len(shape))
    return pl.pallas_call(
        kern,
        grid=(bsz, nt),
        in_specs=[
            pl.BlockSpec((tile, qk), lambda b, t: (b * nt + t, 0)),
            pl.BlockSpec((tile, qk), lambda b, t: (b * nt + t, 1)),
            pl.BlockSpec((tile, vw), lambda b, t: (b * nt + t, 1)),
            pl.BlockSpec((tile, vw), lambda b, t: (b * nt + t, 2)),
            pl.BlockSpec((N_META, qk), lambda b, t: (mb, 0)),
            pl.BlockSpec((N_META, qk), lambda b, t: (mb, 1)),
            pl.BlockSpec((N_META, vw), lambda b, t: (mb, 1)),
            pl.BlockSpec((N_META, vw), lambda b, t: (mb, 2)),
            full(1, vw),
            full(n_heads, RET_CHUNK, RET_CHUNK),
            full(n_heads, RET_CHUNK, dk),
            full(n_heads, RET_CHUNK, dk),
            full(n_heads, 1, 1),
        ],
        out_specs=[
            pl.BlockSpec((tile, vw), lambda b, t: (b * nt + t, 0)),
            pl.BlockSpec((N_META, vw), lambda b, t: (0, 0)),
            pl.BlockSpec((1, n_heads, dk, dv), lambda b, t: (b, 0, 0, 0)),
        ],
        out_shape=[
            jax.ShapeDtypeStruct((bsz * seq, vw), BF16),
            jax.ShapeDtypeStruct((N_META, vw), BF16),
            jax.ShapeDtypeStruct((bsz, n_heads, dk, dv), F32),
        ],
        scratch_shapes=[pltpu.VMEM((n_heads, dk, dv), F32)],
        compiler_params=_cp(("arbitrary", "arbitrary"), VMEM_LIMIT),
        name="ret_prompt",
    )(qkvg, qkvg, qkvg, qkvg, qkvg, qkvg, qkvg, qkvg, gn_g, dec, rdec, cdec, cpow)


def _ret_sample_kernel(q_ref, k_ref, v_ref, g_ref, s0_ref, gn_ref, dec_ref, rdec_ref, cdec_ref, cpow_ref,
                       o_ref, st_ref, *, n_heads, dk, dv, nb, dec_seq):
    rows = nb * dec_seq
    row_b = lax.broadcasted_iota(jnp.int32, (rows, 1), 0) // dec_seq
    for h in range(n_heads):
        q = q_ref[:, h * dk:(h + 1) * dk]
        k = k_ref[:, h * dk:(h + 1) * dk]
        v = v_ref[:, h * dv:(h + 1) * dv]
        sc = _dot_nt(q, k) * dec_ref[h]
        o = _dot(sc.astype(BF16), v)
        q_in = q.astype(F32) * rdec_ref[h]
        k_out = k.astype(F32) * cdec_ref[h]
        for b in range(nb):
            s_old = s0_ref[0, b, h]
            sel = row_b == b
            o = o + _dot(jnp.where(sel, q_in, 0.0).astype(BF16), s_old.astype(BF16))
            st_ref[0, b, h] = cpow_ref[h] * s_old + _dot_tn(jnp.where(sel, k_out, 0.0).astype(BF16), v)
        o_ref[:, h * dv:(h + 1) * dv] = _group_norm_gate(
            o, g_ref[:, h * dv:(h + 1) * dv], gn_ref[:, h * dv:(h + 1) * dv])


def _ret_sample(qkvg, state, gn_g, tabs, row0, dec_b, dec_seq, n_heads, dk, dv):
    qk, vw = n_heads * dk, n_heads * dv
    nb = SAMPLE_BATCH_TILE
    rows = nb * dec_seq
    assert dec_b % nb == 0 and row0 % rows == 0 and rows % 16 == 0
    rb = row0 // rows
    dec, rdec, cdec, cpow = tabs
    state5 = state.reshape(dec_b // nb, nb, n_heads, dk, dv)
    kern = functools.partial(_ret_sample_kernel, n_heads=n_heads, dk=dk, dv=dv, nb=nb, dec_seq=dec_seq)
    full = lambda *shape: pl.BlockSpec(shape, lambda i: (0,) * len(shape))
    out, st = pl.pallas_call(
        kern,
        grid=(dec_b // nb,),
        in_specs=[
            pl.BlockSpec((rows, qk), lambda i: (rb + i, 0)),
            pl.BlockSpec((rows, qk), lambda i: (rb + i, 1)),
            pl.BlockSpec((rows, vw), lambda i: (rb + i, 1)),
            pl.BlockSpec((rows, vw), lambda i: (rb + i, 2)),
            pl.BlockSpec((1, nb, n_heads, dk, dv), lambda i: (i, 0, 0, 0, 0)),
            full(1, vw),
            full(n_heads, rows, rows),
            full(n_heads, rows, dk),
            full(n_heads, rows, dk),
            full(n_heads, 1, 1),
        ],
        out_specs=[
            pl.BlockSpec((rows, vw), lambda i: (i, 0)),
            pl.BlockSpec((1, nb, n_heads, dk, dv), lambda i: (i, 0, 0, 0, 0)),
        ],
        out_shape=[
            jax.ShapeDtypeStruct((dec_b * dec_seq, vw), BF16),
            jax.ShapeDtypeStruct(state5.shape, F32),
        ],
        compiler_params=_cp(("arbitrary",), VMEM_LIMIT),
        name="ret_sample",
    )(qkvg, qkvg, qkvg, qkvg, state5, gn_g, dec, rdec, cdec, cpow)
    return out, st.reshape(dec_b, n_heads, dk, dv)


def _decay_tables(n_heads, dk, chunk, group):
    lg = jnp.log1p(-jnp.exp2(-5.0 - jnp.arange(n_heads, dtype=F32)))
    r = jnp.arange(chunk)
    t = (r % group).astype(F32)
    diff = t[:, None] - t[None, :]
    same = (r[:, None] // group) == (r[None, :] // group)
    dec = jnp.where(same & (diff >= 0), jnp.exp(jnp.maximum(diff, 0.0)[None] * lg[:, None, None]), 0.0)
    rdec = jnp.exp((t[None, :] + 1.0) * lg[:, None])
    cdec = jnp.exp((group - 1.0 - t)[None, :] * lg[:, None])
    cpow = jnp.exp(group * lg)
    bc = lambda a: jnp.broadcast_to(a[:, :, None], (n_heads, chunk, dk))
    return dec, bc(rdec), bc(cdec), cpow.reshape(n_heads, 1, 1)


def _store_row_tiles(ref, y):
    rows = y.shape[0]
    for s in range(SUBLANES):
        ref[pl.ds(s, rows, stride=SUBLANES), :] = y[:, s * LANE:(s + 1) * LANE]


def _load_row_tiles(ref, lead, rows):
    return [ref[lead + (pl.ds(s, rows, stride=SUBLANES), slice(None))] for s in range(SUBLANES)]


def _proj_ln_kernel(a_ref, w_ref, h_ref, g_ref, b_ref, o_ref, obf_ref, *rest, alpha, row_tiles):
    wbf_ref = rest[-1]

    @pl.when(pl.program_id(0) == 0)
    def _():
        wbf_ref[...] = w_ref[...].astype(BF16)

    z = alpha * h_ref[...] + _dot(a_ref[...], wbf_ref[...])
    y = _layer_norm(z, g_ref[...], b_ref[...])
    o_ref[...] = y
    obf_ref[...] = y.astype(BF16)
    if row_tiles:
        _store_row_tiles(rest[0], y)


def _proj_ln(a, w, h, g, b, alpha, row_tiles=False):
    r, k = a.shape
    d = w.shape[1]
    assert not row_tiles or d == SUBLANES * LANE
    tile = RET_PROJ_TILE if r % RET_PROJ_TILE == 0 else ROW_TILE
    row = lambda n: pl.BlockSpec((tile, n), lambda i: (i, 0))
    full = lambda *shape: pl.BlockSpec(shape, lambda i: (0,) * len(shape))
    out_specs = [row(d), row(d)]
    out_shape = [jax.ShapeDtypeStruct((r, d), F32), jax.ShapeDtypeStruct((r, d), BF16)]
    if row_tiles:
        out_specs.append(pl.BlockSpec((tile * SUBLANES, LANE), lambda i: (i, 0)))
        out_shape.append(jax.ShapeDtypeStruct((r * SUBLANES, LANE), F32))
    return pl.pallas_call(
        functools.partial(_proj_ln_kernel, alpha=alpha, row_tiles=row_tiles),
        grid=(r // tile,),
        in_specs=[row(k), full(k, d), row(d), full(1, d), full(1, d)],
        out_specs=out_specs,
        out_shape=out_shape,
        scratch_shapes=[pltpu.VMEM((k, d), BF16)],
        compiler_params=_cp(("arbitrary",), VMEM_LIMIT),
        name="proj_ln",
    )(a, w, h, g, b)


def _ffn_ln_kernel(x_ref, w13_ref, w2_ref, h_ref, g_ref, b_ref, o_ref, obf_ref, *, alpha, d_ff, chunk):
    x = x_ref[...]
    acc = alpha * h_ref[...]
    for c in range(d_ff // chunk):
        gate = _dot(x, w13_ref[:, c * chunk:(c + 1) * chunk])
        up = _dot(x, w13_ref[:, d_ff + c * chunk:d_ff + (c + 1) * chunk])
        acc = acc + _dot((_silu(gate) * up).astype(BF16), w2_ref[c * chunk:(c + 1) * chunk, :])
    y = _layer_norm(acc, g_ref[...], b_ref[...])
    o_ref[...] = y
    obf_ref[...] = y.astype(BF16)


def _ffn_ln(x, w13, w2, h, g, b, alpha):
    r, d = x.shape
    d_ff = w2.shape[0]
    assert d_ff % FFN_CHUNK == 0
    row = lambda n: pl.BlockSpec((ROW_TILE, n), lambda i: (i, 0))
    full = lambda *shape: pl.BlockSpec(shape, lambda i: (0,) * len(shape))
    return pl.pallas_call(
        functools.partial(_ffn_ln_kernel, alpha=alpha, d_ff=d_ff, chunk=FFN_CHUNK),
        grid=(r // ROW_TILE,),
        in_specs=[row(d), full(d, 2 * d_ff), full(d_ff, d), row(d), full(1, d), full(1, d)],
        out_specs=[row(d), row(d)],
        out_shape=[jax.ShapeDtypeStruct((r, d), F32), jax.ShapeDtypeStruct((r, d), BF16)],
        compiler_params=_cp(("arbitrary",), VMEM_LIMIT),
        name="ffn_ln",
    )(x, w13, w2, h, g, b)


def _kv_kernel(x_ref, wkv_ref, g_ref, ca_ref, sa_ref, cb_ref, sb_ref, wuk_ref, wuv_ref, ones_ref,
               ckv_ref, kpe_ref, k_ref, v_ref, *, lora, rope, n_heads):
    y = _dot(x_ref[...], wkv_ref[...])
    cn = _rms_norm(y[:, :lora], g_ref[...])
    ckv_ref[...] = cn
    kpe_a = y[:, lora:lora + LANE] * ca_ref[...] + y[:, lora + LANE:lora + 2 * LANE] * sa_ref[...]
    kpe_ref[...] = kpe_a[:, :rope]
    kpe_b = y[:, lora + 2 * LANE:lora + 3 * LANE] * cb_ref[...] + y[:, lora + 3 * LANE:lora + 4 * LANE] * sb_ref[...]
    cb = cn.astype(BF16)
    kn = _dot(cb, wuk_ref[...])
    for h in range(n_heads):
        k_ref[:, h * LANE:(h + 1) * LANE] = (kn[:, h * LANE:(h + 1) * LANE] + kpe_b).astype(BF16)
    v_ref[...] = (_dot(cb, wuv_ref[...]) + ones_ref[...]).astype(BF16)


def _kv_proj(x, wkv, g, tabs, wuk, wuv, ones_row, lora, rope, n_heads):
    r, d = x.shape
    nv = wuv.shape[1]
    tile = RET_PROJ_TILE if r % RET_PROJ_TILE == 0 else ROW_TILE
    row = lambda n: pl.BlockSpec((tile, n), lambda i: (i, 0))
    full = lambda *shape: pl.BlockSpec(shape, lambda i: (0,) * len(shape))
    return pl.pallas_call(
        functools.partial(_kv_kernel, lora=lora, rope=rope, n_heads=n_heads),
        grid=(r // tile,),
        in_specs=[row(d), full(*wkv.shape), full(1, lora), row(LANE), row(LANE), row(LANE), row(LANE),
                  full(*wuk.shape), full(*wuv.shape), full(1, nv)],
        out_specs=[row(lora), row(rope), row(n_heads * LANE), row(nv)],
        out_shape=[jax.ShapeDtypeStruct((r, lora), F32), jax.ShapeDtypeStruct((r, rope), F32),
                   jax.ShapeDtypeStruct((r, n_heads * LANE), BF16), jax.ShapeDtypeStruct((r, nv), BF16)],
        compiler_params=_cp(("arbitrary",), VMEM_LIMIT),
        name="kv_proj",
    )(x, wkv, g, *tabs, wuk, wuv, ones_row)


def _q_kernel(x_ref, wdq_ref, g_ref, w1_ref, w2_ref, c_ref, s_ref, q_ref, *, n_heads):
    qc = _rms_norm(_dot(x_ref[...], wdq_ref[...]), g_ref[...]).astype(BF16)
    a = _dot(qc, w1_ref[...])
    b = _dot(qc, w2_ref[...])
    c = c_ref[...]
    s = s_ref[...]
    for h in range(n_heads):
        cols = slice(h * LANE, (h + 1) * LANE)
        q_ref[:, cols] = (a[:, cols] * c + b[:, cols] * s).astype(BF16)


def _q_proj(x, wdq, g, w1, w2, c_tab, s_tab, n_heads):
    r, d = x.shape
    ql = wdq.shape[1]
    tile = RET_PROJ_TILE if r % RET_PROJ_TILE == 0 else ROW_TILE
    row = lambda n: pl.BlockSpec((tile, n), lambda i: (i, 0))
    full = lambda *shape: pl.BlockSpec(shape, lambda i: (0,) * len(shape))
    return pl.pallas_call(
        functools.partial(_q_kernel, n_heads=n_heads),
        grid=(r // tile,),
        in_specs=[row(d), full(d, ql), full(1, ql), full(*w1.shape), full(*w2.shape), row(LANE), row(LANE)],
        out_specs=row(n_heads * LANE),
        out_shape=jax.ShapeDtypeStruct((r, n_heads * LANE), BF16),
        compiler_params=_cp(("arbitrary",), VMEM_LIMIT),
        name="q_proj",
    )(x, wdq, g, w1, w2, c_tab, s_tab)


def _q_sample_kernel(x_ref, wdq_ref, g_ref, w1_ref, wukt_ref, wpe_ref, wrot_ref, c_ref, s_ref,
                     qabs_ref, qpe_ref, *, n_heads, lora, scale):
    qc = _rms_norm(_dot(x_ref[...], wdq_ref[...]), g_ref[...]).astype(BF16)
    qn = (_dot(qc, w1_ref[...]) * scale).astype(BF16)
    for h in range(n_heads):
        qabs_ref[:, h * lora:(h + 1) * lora] = _dot(qn[:, h * LANE:(h + 1) * LANE], wukt_ref[h]).astype(BF16)
    qpe = _dot(qc, wpe_ref[...]) * c_ref[...] + _dot(qc, wrot_ref[...]) * s_ref[...]
    qpe_ref[...] = (qpe * scale).astype(BF16)


def _q_sample(x, wdq, g, w1, wukt, wpe, wrot, c_tab, s_tab, n_heads, lora, scale):
    ns = x.shape[0]
    args = (x, wdq, g, w1, wukt, wpe, wrot, c_tab, s_tab)
    full = lambda a: pl.BlockSpec(a.shape, lambda i, nd=a.ndim: (0,) * nd)
    return pl.pallas_call(
        functools.partial(_q_sample_kernel, n_heads=n_heads, lora=lora, scale=scale),
        grid=(1,),
        in_specs=[full(a) for a in args],
        out_specs=[pl.BlockSpec((ns, n_heads * lora), lambda i: (0, 0)),
                   pl.BlockSpec((ns, wpe.shape[1]), lambda i: (0, 0))],
        out_shape=[jax.ShapeDtypeStruct((ns, n_heads * lora), BF16),
                   jax.ShapeDtypeStruct((ns, wpe.shape[1]), BF16)],
        compiler_params=_cp(("arbitrary",), VMEM_LIMIT),
        name="q_sample",
    )(*args)


def _attn_step(m, acc, s, v):
    m_new = jnp.maximum(m, jnp.max(s, axis=-1, keepdims=True))
    p = jnp.exp2((s - m_new).astype(BF16))
    return m_new, jnp.exp2(m - m_new) * acc + _dot(p, v)


def _attn_prompt_kernel(q_ref, k_ref, v_ref, km_ref, vm_ref, o_ref, *, tq, tk, n_meta):
    i = pl.program_id(2)
    lane = lax.broadcasted_iota(jnp.int32, (tq, LANE), 1)
    heads = (slice(0, LANE), slice(LANE, 2 * LANE))
    qs = [q_ref[:, h] for h in heads]

    carry = []
    for q, h in zip(qs, heads):
        s = jnp.where(lane < n_meta, _dot_nt(q, km_ref[:, h]), -jnp.inf)
        m = jnp.max(s, axis=-1, keepdims=True)
        carry += [m, _dot(jnp.exp2((s - m).astype(BF16)), vm_ref[:, h])]

    def tile(j, carry, diag):
        off = pl.multiple_of(j * tk, tk)
        lo = 0 if diag is None else diag * tk
        out = []
        for e, (q, h) in enumerate(zip(qs, heads)):
            s = _dot_nt(q[lo:], k_ref[pl.ds(off, tk), h])
            if diag is not None:
                row = lax.broadcasted_iota(jnp.int32, (tk, tk), 0)
                col = lax.broadcasted_iota(jnp.int32, (tk, tk), 1)
                top = jnp.where(col <= row, s[:tk], -jnp.inf)
                s = jnp.concatenate([top, s[tk:]], axis=0) if tq - lo > tk else top
            m, acc = carry[2 * e], carry[2 * e + 1]
            m_new, acc_new = _attn_step(m[lo:], acc[lo:], s, v_ref[pl.ds(off, tk), h])
            if lo:
                m_new = jnp.concatenate([m[:lo], m_new], axis=0)
                acc_new = jnp.concatenate([acc[:lo], acc_new], axis=0)
            out += [m_new, acc_new]
        return out

    per_q = tq // tk
    carry = lax.fori_loop(0, i * per_q, lambda j, c: tuple(tile(j, c, None)), tuple(carry))
    for dj in range(per_q):
        carry = tile(i * per_q + dj, carry, dj)
    _, acc0, _, acc1 = carry
    l0 = jnp.sum(jnp.where(lane == LANE // 2, acc0, 0.0), axis=-1, keepdims=True)
    l1 = jnp.sum(jnp.where(lane == 0, acc1, 0.0), axis=-1, keepdims=True)
    o_ref[...] = jnp.where(lane < LANE // 2, acc0 * (1.0 / l0), acc1 * (1.0 / l1)).astype(BF16)


def _attn_prompt(q, k, v, bsz, seq, meta_row0, n_heads):
    tq, tk = min(ATTN_TQ, seq), ATTN_TK
    assert tq % tk == 0 and seq % tq == 0 and meta_row0 % LANE == 0 and n_heads % 2 == 0
    assert v.shape[1] == n_heads * LANE
    nq = seq // tq
    mb = meta_row0 // LANE
    pair = 2 * LANE
    return pl.pallas_call(
        functools.partial(_attn_prompt_kernel, tq=tq, tk=tk, n_meta=N_META),
        grid=(bsz, n_heads // 2, nq),
        in_specs=[
            pl.BlockSpec((tq, pair), lambda b, hp, i: (b * nq + i, hp)),
            pl.BlockSpec((seq, pair), lambda b, hp, i: (b, hp)),
            pl.BlockSpec((seq, pair), lambda b, hp, i: (b, hp)),
            pl.BlockSpec((LANE, pair), lambda b, hp, i: (mb, hp)),
            pl.BlockSpec((LANE, pair), lambda b, hp, i: (mb, hp)),
        ],
        out_specs=pl.BlockSpec((tq, LANE), lambda b, hp, i: (b * nq + i, hp)),
        out_shape=jax.ShapeDtypeStruct((bsz * seq, n_heads * LANE // 2), BF16),
        compiler_params=_cp(("arbitrary", "arbitrary", "arbitrary"), VMEM_LIMIT),
        name="attn_prompt",
    )(q, k, v, k, v)


def _softmax_part(s, v):
    m = jnp.max(s, axis=-1, keepdims=True)
    p = jnp.exp2(s - m)
    return m, jnp.sum(p, axis=-1, keepdims=True), _dot(p.astype(BF16), v)


def _softmax_merge(parts):
    m = functools.reduce(jnp.maximum, [p[0] for p in parts])
    w = [jnp.exp2(p[0] - m) for p in parts]
    l = functools.reduce(jnp.add, [wi * p[1] for wi, p in zip(w, parts)])
    acc = functools.reduce(jnp.add, [wi * p[2] for wi, p in zip(w, parts)])
    return m, l, acc


def _attn_paged_kernel(pt_ref, qabs_ref, qpe_ref, cnew_ref, pnew_ref, ckv_hbm, kpe_hbm, o_ref,
                       cbuf, pbuf, sem, m_ref, l_ref, acc_ref, *, ppc, n_groups, n_slots, n_heads, dec_seq, sub):
    c = pl.program_id(1)
    group = pl.program_id(0) * n_groups + c
    n_chunks = pl.num_programs(0) * n_groups * n_slots

    def copies(chunk, sl):
        base = chunk * ppc
        out = []
        for p in range(ppc):
            page = pt_ref[base + p]
            keys = pl.ds(p * PAGE_SIZE, PAGE_SIZE)
            out.append(pltpu.make_async_copy(ckv_hbm.at[page], cbuf.at[sl, keys, :], sem.at[0, sl]))
            out.append(pltpu.make_async_copy(kpe_hbm.at[page], pbuf.at[sl, :, keys], sem.at[1, sl]))
        return out

    def wait_slot(sl):
        pltpu.make_async_copy(cbuf.at[sl], cbuf.at[sl], sem.at[0, sl]).wait()
        pltpu.make_async_copy(pbuf.at[sl], pbuf.at[sl], sem.at[1, sl]).wait()

    @pl.when(group == 0)
    def _():
        for k in range(n_slots - 1):
            for cp in copies(k, k):
                cp.start()

    @pl.when(c == 0)
    def _():
        m_ref[...] = jnp.full(m_ref.shape, -jnp.inf, F32)
        l_ref[...] = jnp.zeros(l_ref.shape, F32)
        acc_ref[...] = jnp.zeros(acc_ref.shape, F32)

    qa = qabs_ref[0]
    qp = qpe_ref[0]
    n_sub = ppc * PAGE_SIZE // sub

    def slot_parts(sl):
        def scores(i):
            keys = slice(i * sub, (i + 1) * sub)
            kc = cbuf[sl, keys, :].astype(BF16)
            return _dot_nt(qa, kc) + _dot(qp, pbuf[sl, :, keys].astype(BF16)), kc

        ahead = [scores(i) for i in range(min(PAGED_SCORES_AHEAD, n_sub))]
        out = []
        for i in range(n_sub):
            s, kc = ahead.pop(0)
            if i + PAGED_SCORES_AHEAD < n_sub:
                ahead.append(scores(i + PAGED_SCORES_AHEAD))
            out.append(_softmax_part(s, kc))
        return out

    parts = [(m_ref[...], l_ref[...], acc_ref[...])]
    for k in range(n_slots):
        wait_slot(k)
        nxt = group * n_slots + (k + n_slots - 1)
        for cp in copies(jnp.where(nxt < n_chunks, nxt, 0), (k + n_slots - 1) % n_slots):
            cp.start()
        parts += slot_parts(k)
    m, l, acc = _softmax_merge(parts)
    m_ref[...] = m
    l_ref[...] = l
    acc_ref[...] = acc

    @pl.when(c == n_groups - 1)
    def _():
        kn = cnew_ref[0]
        s = _dot_nt(qa, kn) + _dot_nt(qp, pnew_ref[0])
        rows = s.shape[0]
        tok = lax.broadcasted_iota(jnp.int32, (rows, LANE), 0) // n_heads
        col = lax.broadcasted_iota(jnp.int32, (rows, LANE), 1)
        s = jnp.where((col <= tok) & (col < dec_seq), s, -jnp.inf)
        _, l2, acc2 = _softmax_merge([(m, l, acc), _softmax_part(s, kn)])
        o_ref[0] = acc2 * (1.0 / l2)

    @pl.when(group == pl.num_programs(0) * n_groups - 1)
    def _():
        for k in range(n_slots - 1):
            wait_slot(k)


def _attn_paged(page_table, qabs, qpe, cnew, pnew, cache_ckv, cache_kpe_t, n_heads, dec_seq):
    dec_b, n_pages = page_table.shape
    n_slots = PAGED_SLOTS
    assert n_pages % n_slots == 0
    ppc = min(PAGES_PER_CHUNK, n_pages // n_slots)
    sub = min(PAGED_KEY_BLOCK, ppc * PAGE_SIZE)
    assert n_pages % (n_slots * ppc) == 0 and cache_ckv.shape[1] == PAGE_SIZE and (ppc * PAGE_SIZE) % sub == 0
    n_groups = n_pages // (n_slots * ppc)
    rows, lora = qabs.shape[1], qabs.shape[2]
    rope = qpe.shape[2]
    kern = functools.partial(_attn_paged_kernel, ppc=ppc, n_groups=n_groups, n_slots=n_slots, n_heads=n_heads,
                             dec_seq=dec_seq, sub=sub)
    grid_spec = pltpu.PrefetchScalarGridSpec(
        num_scalar_prefetch=1,
        grid=(dec_b, n_groups),
        in_specs=[
            pl.BlockSpec((1, rows, lora), lambda b, c, pt: (b, 0, 0)),
            pl.BlockSpec((1, rows, rope), lambda b, c, pt: (b, 0, 0)),
            pl.BlockSpec((1, LANE, lora), lambda b, c, pt: (b, 0, 0)),
            pl.BlockSpec((1, LANE, rope), lambda b, c, pt: (b, 0, 0)),
            pl.BlockSpec(memory_space=pl.ANY),
            pl.BlockSpec(memory_space=pl.ANY),
        ],
        out_specs=pl.BlockSpec((1, rows, lora), lambda b, c, pt: (b, 0, 0)),
        scratch_shapes=[
            pltpu.VMEM((n_slots, ppc * PAGE_SIZE, lora), F32),
            pltpu.VMEM((n_slots, rope, ppc * PAGE_SIZE), F32),
            pltpu.SemaphoreType.DMA((2, n_slots)),
            pltpu.VMEM((rows, 1), F32),
            pltpu.VMEM((rows, 1), F32),
            pltpu.VMEM((rows, lora), F32),
        ],
    )
    return pl.pallas_call(
        kern,
        grid_spec=grid_spec,
        out_shape=jax.ShapeDtypeStruct((dec_b, rows, lora), F32),
        compiler_params=_cp(("arbitrary", "arbitrary"), VMEM_LIMIT),
        name="attn_paged",
    )(page_table.reshape(-1), qabs, qpe, cnew, pnew, cache_ckv, cache_kpe_t)


def _uv_sample_kernel(o_ref, w_ref, out_ref, *, n_heads, lora):
    for p in range(n_heads // 2):
        acc = _dot(o_ref[:, (2 * p) * lora:(2 * p + 1) * lora].astype(BF16), w_ref[2 * p])
        acc = acc + _dot(o_ref[:, (2 * p + 1) * lora:(2 * p + 2) * lora].astype(BF16), w_ref[2 * p + 1])
        out_ref[:, p * LANE:(p + 1) * LANE] = acc.astype(BF16)


def _uv_sample(o_lat, wuv_pad, n_heads, lora):
    ns = o_lat.shape[0]
    nv = n_heads * LANE // 2
    return pl.pallas_call(
        functools.partial(_uv_sample_kernel, n_heads=n_heads, lora=lora),
        grid=(1,),
        in_specs=[pl.BlockSpec(o_lat.shape, lambda i: (0, 0)), pl.BlockSpec(wuv_pad.shape, lambda i: (0, 0, 0))],
        out_specs=pl.BlockSpec((ns, nv), lambda i: (0, 0)),
        out_shape=jax.ShapeDtypeStruct((ns, nv), BF16),
        compiler_params=_cp(("arbitrary",), VMEM_LIMIT),
        name="uv_sample",
    )(o_lat, wuv_pad)


def _router_kernel(x_ref, w_ref, b_ref, o_ref, *, n_experts):
    logits = _dot(x_ref[...], w_ref[...]) + b_ref[...]
    lane = lax.broadcasted_iota(jnp.int32, logits.shape, 1).astype(F32)
    logits = jnp.where(lane < n_experts, logits, -jnp.inf)
    m1 = jnp.max(logits, axis=-1, keepdims=True)
    i1 = jnp.min(jnp.where(logits == m1, lane, float(LANE)), axis=-1, keepdims=True)
    rest = jnp.where(lane == i1, -jnp.inf, logits)
    m2 = jnp.max(rest, axis=-1, keepdims=True)
    i2 = jnp.min(jnp.where(rest == m2, lane, float(LANE)), axis=-1, keepdims=True)
    e2 = jnp.exp(m2 - m1)
    den = 1.0 + e2
    g1 = 1.0 / den
    g2 = e2 / den
    o_ref[...] = jnp.where(lane == 0, i1, jnp.where(lane == 1, i2, jnp.where(lane == 2, g1, jnp.where(lane == 3, g2, 0.0))))


def _router(x, w_pad, b_pad, n_experts):
    r, d = x.shape
    return pl.pallas_call(
        functools.partial(_router_kernel, n_experts=n_experts),
        grid=(r // ROW_TILE,),
        in_specs=[pl.BlockSpec((ROW_TILE, d), lambda i: (i, 0)), pl.BlockSpec((d, LANE), lambda i: (0, 0)),
                  pl.BlockSpec((1, LANE), lambda i: (0, 0))],
        out_specs=pl.BlockSpec((ROW_TILE, LANE), lambda i: (i, 0)),
        out_shape=jax.ShapeDtypeStruct((r, LANE), F32),
        compiler_params=_cp(("arbitrary",), VMEM_LIMIT),
        name="router",
    )(x, w_pad, b_pad)


def _moe_ffn_kernel(te_ref, nu_ref, src_ref, x_hbm, wg_ref, wu_ref, w2_ref, o_ref, xbuf, xb_ref, acc_ref, sem,
                    *, tm, rows_per_step):
    t = pl.program_id(0)
    f = pl.program_id(1)
    nf = pl.num_programs(1)
    used = t < nu_ref[0]
    slot = t % 2
    n_rows = rows_per_step * nf

    def row_copy(tile, r, sl):
        row = src_ref[tile * tm + r]
        return pltpu.make_async_copy(x_hbm.at[pl.ds(row * SUBLANES, SUBLANES), :],
                                     xbuf.at[sl, pl.ds(r * SUBLANES, SUBLANES), :], sem.at[sl])

    @pl.when((t == 0) & (f == 0))
    def _():
        def body(g, carry):
            for u in range(GATHER_UNROLL):
                row_copy(0, g * GATHER_UNROLL + u, 0).start()
            return carry
        lax.fori_loop(0, n_rows // GATHER_UNROLL, body, 0)

    @pl.when(f == 0)
    def _():
        pltpu.make_async_copy(xbuf.at[slot], xbuf.at[slot], sem.at[slot]).wait()
        for s, slab in enumerate(_load_row_tiles(xbuf, (slot,), tm)):
            xb_ref[:, s * LANE:(s + 1) * LANE] = slab.astype(BF16)

    def prefetch():
        for u in range(rows_per_step):
            row_copy(t + 1, f * rows_per_step + u, 1 - slot).start()

    @pl.when(used)
    def _():
        prefetch()
        x = xb_ref[...]
        y = None
        for c in range(wg_ref.shape[2] // FFN_CHUNK):
            cols = slice(c * FFN_CHUNK, (c + 1) * FFN_CHUNK)
            hcol = (_silu(_dot(x, wg_ref[0, :, cols])) * _dot(x, wu_ref[0, :, cols])).astype(BF16)
            part = _dot(hcol, w2_ref[0, cols, :])
            y = part if y is None else y + part

        @pl.when(f == 0)
        def _():
            acc_ref[...] = y

        @pl.when(f > 0)
        def _():
            acc_ref[...] += y

    @pl.when(jnp.logical_not(used))
    def _():
        prefetch()

    @pl.when(f == nf - 1)
    def _():
        _store_row_tiles(o_ref, jnp.where(used, acc_ref[...], 0.0))

    @pl.when((t == pl.num_programs(0) - 1) & (f == nf - 1))
    def _():
        pltpu.make_async_copy(xbuf.at[1 - slot], xbuf.at[1 - slot], sem.at[1 - slot]).wait()


def _moe_ffn(tile_expert, n_used, src, x, w13, w2):
    d = w13.shape[1]
    assert x.shape[1] == LANE and d == SUBLANES * LANE
    n_exp, eff = w2.shape[0], w2.shape[1]
    tm, tf = MOE_ROW_TILE, min(MOE_FF_CHUNK, eff)
    n = src.shape[0] - 2 * tm
    assert n % tm == 0 and eff % tf == 0 and tf % FFN_CHUNK == 0
    nf = eff // tf
    rows_per_step = -(-tm // (nf * GATHER_UNROLL)) * GATHER_UNROLL
    assert rows_per_step * nf <= 2 * tm

    def widx(base):
        def index_map(t, f, te, nu, src):
            live = t < nu[0]
            return (te[t], 0, jnp.where(live, f, nf - 1) + base)
        return index_map

    def w2idx(t, f, te, nu, src):
        return (te[t], jnp.where(t < nu[0], f, nf - 1), 0)

    grid_spec = pltpu.PrefetchScalarGridSpec(
        num_scalar_prefetch=3,
        grid=(n // tm, nf),
        in_specs=[
            pl.BlockSpec(memory_space=pl.ANY),
            pl.BlockSpec((1, d, tf), widx(0)),
            pl.BlockSpec((1, d, tf), widx(nf)),
            pl.BlockSpec((1, tf, d), w2idx),
        ],
        out_specs=pl.BlockSpec((tm * SUBLANES, LANE), lambda t, f, te, nu, src: (t, 0)),
        scratch_shapes=[
            pltpu.VMEM((2, rows_per_step * nf * SUBLANES, LANE), F32),
            pltpu.VMEM((tm, d), BF16),
            pltpu.VMEM((tm, d), F32),
            pltpu.SemaphoreType.DMA((2,)),
        ],
    )
    return pl.pallas_call(
        functools.partial(_moe_ffn_kernel, tm=tm, rows_per_step=rows_per_step),
        grid_spec=grid_spec,
        out_shape=jax.ShapeDtypeStruct((n * SUBLANES, LANE), F32),
        compiler_params=_cp(("arbitrary", "arbitrary"), VMEM_LIMIT),
        name="moe_ffn",
    )(tile_expert, n_used, src, x, w13, w13, w2)


def _combine_ln_kernel(p1_ref, p2_ref, y_hbm, h_ref, route_ref, g_ref, b_ref, o_ref, buf, sem, *, tile, alpha):
    i = pl.program_id(0)
    n = pl.num_programs(0)
    slot = i % 2

    def issue(t, sl):
        def body(g, carry):
            for u in range(GATHER_UNROLL):
                r = g * GATHER_UNROLL + u
                a = p1_ref[t * tile + r]
                b = p2_ref[t * tile + r]
                dst = pl.ds(r * SUBLANES, SUBLANES)
                pltpu.make_async_copy(y_hbm.at[pl.ds(a * SUBLANES, SUBLANES), :], buf.at[sl, 0, dst, :],
                                      sem.at[sl]).start()
                pltpu.make_async_copy(y_hbm.at[pl.ds(b * SUBLANES, SUBLANES), :], buf.at[sl, 1, dst, :],
                                      sem.at[sl]).start()
            return carry
        lax.fori_loop(0, tile // GATHER_UNROLL, body, 0)

    @pl.when(i == 0)
    def _():
        issue(i, slot)

    @pl.when(i + 1 < n)
    def _():
        issue(i + 1, 1 - slot)

    pltpu.make_async_copy(buf.at[slot], buf.at[slot], sem.at[slot]).wait()
    route = route_ref[...]
    g1, g2 = route[:, 2:3], route[:, 3:4]
    ya = _load_row_tiles(buf, (slot, 0), tile)
    yb = _load_row_tiles(buf, (slot, 1), tile)
    z = jnp.concatenate([alpha * h_ref[:, s * LANE:(s + 1) * LANE] + (g1 * ya[s] + g2 * yb[s])
                         for s in range(SUBLANES)], axis=1)
    o_ref[...] = _layer_norm(z, g_ref[...], b_ref[...])


def _combine_ln(pos1, pos2, y_sorted, h, route, g, b, alpha):
    r, d = h.shape
    tile = GATHER_TILE
    assert r % tile == 0 and tile % GATHER_UNROLL == 0 and d == SUBLANES * LANE
    grid_spec = pltpu.PrefetchScalarGridSpec(
        num_scalar_prefetch=2,
        grid=(r // tile,),
        in_specs=[
            pl.BlockSpec(memory_space=pl.ANY),
            pl.BlockSpec((tile, d), lambda i, p1, p2: (i, 0)),
            pl.BlockSpec((tile, LANE), lambda i, p1, p2: (i, 0)),
            pl.BlockSpec((1, d), lambda i, p1, p2: (0, 0)),
            pl.BlockSpec((1, d), lambda i, p1, p2: (0, 0)),
        ],
        out_specs=pl.BlockSpec((tile, d), lambda i, p1, p2: (i, 0)),
        scratch_shapes=[pltpu.VMEM((2, 2, tile * SUBLANES, LANE), F32), pltpu.SemaphoreType.DMA((2,))],
    )
    return pl.pallas_call(
        functools.partial(_combine_ln_kernel, tile=tile, alpha=alpha),
        grid_spec=grid_spec,
        out_shape=jax.ShapeDtypeStruct((r, d), F32),
        compiler_params=_cp(("arbitrary",), VMEM_LIMIT),
        name="combine_ln",
    )(pos1, pos2, y_sorted, h, route, g, b)


def _route_plan(route, n_experts, tm):
    r = route.shape[0]
    e = jnp.concatenate([route[:, 0], route[:, 1]]).astype(jnp.int32)
    onehot = (e[:, None] == jnp.arange(n_experts, dtype=jnp.int32)[None, :]).astype(jnp.int32)
    csum = jnp.cumsum(onehot, axis=0)
    rank = jnp.take_along_axis(csum, e[:, None], axis=1)[:, 0] - 1
    tiles = (csum[-1] + tm - 1) // tm
    tile_end = jnp.cumsum(tiles)
    pos = (tile_end - tiles)[e] * tm + rank
    n_tiles = (TOP_K * r) // tm + n_experts
    tok = jnp.arange(TOP_K * r, dtype=jnp.int32) % r
    src = jnp.zeros(((n_tiles + 2) * tm,), jnp.int32).at[pos].set(tok, unique_indices=True)
    n_used = tile_end[-1]
    tile_id = jnp.minimum(jnp.arange(n_tiles, dtype=jnp.int32), n_used - 1)
    tile_expert = jnp.sum((tile_end[None, :] <= tile_id[:, None]).astype(jnp.int32), axis=1)
    return src, tile_expert, n_used.reshape(1).astype(jnp.int32), pos[:r], pos[r:]


def _pad_cols(w, offset, width):
    return jnp.pad(w, ((0, 0), (offset, width - offset - w.shape[1])))


def kernel(x_prompt, x_sample, state_ret, cache_ckv, cache_kpe, page_table, meta_tokens, ln_g, ln_b, ret_w_in, ret_gn_g, ret_w_o, mla_w_dq, mla_q_norm, mla_w_uq, mla_w_o, kv_w_a, kv_norm, kv_w_b, ffn_w13, ffn_w2, moe_w_r, moe_b_r, moe_w13, moe_w2):
    bsz, seq, d = x_prompt.shape
    dec_b, dec_seq, _ = x_sample.shape
    n_ret, _, ret_h, dk, dv = state_ret.shape
    depth = ln_g.shape[0]
    assert depth == 2 and n_ret == 1 and mla_w_dq.shape[0] == 1, "layer pattern: one retention layer then one MLA layer"
    past_len = page_table.shape[1] * PAGE_SIZE
    lora, mla_h, nope_v = kv_w_b.shape
    rope = kv_w_a.shape[1] - lora
    nope = mla_w_uq.shape[2] // mla_h - rope
    v_dim = nope_v - nope
    n_experts = moe_w_r.shape[2]
    assert nope == LANE // 2 and v_dim == LANE // 2 and rope <= LANE // 4
    alpha = (2 * depth) ** 0.25
    mla_scale = (nope + rope) ** -0.5
    half = rope // 2

    n_p, n_s = bsz * seq, dec_b * dec_seq
    meta_row0 = n_p + n_s
    r_real = meta_row0 + N_META
    r_pad = -(-r_real // ROW_TILE) * ROW_TILE
    assert r_pad - meta_row0 >= LANE

    x0 = jnp.concatenate([x_prompt.reshape(n_p, d), x_sample.reshape(n_s, d), meta_tokens,
                          jnp.zeros((r_pad - r_real, d), F32)], axis=0)
    pos = jnp.concatenate([jnp.tile(N_META + jnp.arange(seq), bsz), jnp.tile(past_len + jnp.arange(dec_seq), dec_b),
                           jnp.arange(N_META), jnp.zeros((r_pad - r_real,), jnp.int32)]).astype(F32)

    inv_r = ROPE_BASE ** (-jnp.arange(dk // 2, dtype=F32) / (dk // 2))
    ang_r = pos[:, None] * inv_r[None, :]
    cos_r, sin_r = jnp.cos(ang_r), jnp.sin(ang_r)
    inv_m = ROPE_BASE ** (-jnp.arange(half, dtype=F32) / half)
    ang_m = pos[:, None] * inv_m[None, :]
    cos_m = jnp.concatenate([jnp.cos(ang_m)] * 2, axis=1)
    sin_m = jnp.concatenate([jnp.sin(ang_m)] * 2, axis=1)
    cos_a, sin_a = _pad_cols(cos_m, 0, LANE), _pad_cols(sin_m, 0, LANE)
    cos_b, sin_b = _pad_cols(cos_m, nope, LANE), _pad_cols(sin_m, nope, LANE)
    q_scale = mla_scale * LOG2E
    q_ctab = (_pad_cols(jnp.ones((r_pad, nope), F32), 0, LANE) + cos_b) * q_scale
    q_stab = sin_b * q_scale

    def rot(w):
        return jnp.concatenate([-w[..., half:], w[..., :half]], axis=-1)

    qkvg = _ret_proj(x0, ret_w_in[0], cos_r, sin_r, ret_h, dk)
    gn = ret_gn_g[0].reshape(1, -1)
    mix_p, mix_meta, st_p = _ret_prompt(qkvg, gn, _decay_tables(ret_h, dk, RET_CHUNK, RET_CHUNK),
                                        bsz, seq, meta_row0, ret_h, dk, dv)
    mix_s, st_s = _ret_sample(qkvg, state_ret[0], gn,
                              _decay_tables(ret_h, dk, SAMPLE_BATCH_TILE * dec_seq, dec_seq),
                              n_p, dec_b, dec_seq, ret_h, dk, dv)
    mix = jnp.concatenate([mix_p, mix_s, mix_meta, jnp.zeros((r_pad - r_real, mix_p.shape[1]), BF16)], axis=0)
    row = lambda v: v.reshape(1, -1)
    h1, h1b = _proj_ln(mix, ret_w_o[0], x0, row(ln_g[0, 0]), row(ln_b[0, 0]), alpha)
    h2, h2b = _ffn_ln(h1b, ffn_w13[0].astype(BF16), ffn_w2[0].astype(BF16), h1, row(ln_g[0, 1]), row(ln_b[0, 1]), alpha)

    wc, wpe = kv_w_a[:, :lora], kv_w_a[:, lora:]
    wkv = jnp.concatenate([wc, _pad_cols(wpe, 0, LANE), _pad_cols(rot(wpe), 0, LANE),
                           _pad_cols(wpe, nope, LANE), _pad_cols(rot(wpe), nope, LANE)], axis=1).astype(BF16)
    wuk = jnp.pad(kv_w_b[:, :, :nope], ((0, 0), (0, 0), (0, LANE - nope))).reshape(lora, mla_h * LANE).astype(BF16)
    even = (jnp.arange(mla_h) % 2 == 0)[None, :, None]
    wuv_h = kv_w_b[:, :, nope:]
    wuv = jnp.where(even, jnp.pad(wuv_h, ((0, 0), (0, 0), (0, LANE - v_dim))),
                    jnp.pad(wuv_h, ((0, 0), (0, 0), (LANE - v_dim, 0)))).reshape(lora, mla_h * LANE).astype(BF16)
    lane_id = jnp.arange(LANE)[None, None, :]
    ones_row = jnp.where(even, lane_id == v_dim, lane_id == 0).astype(F32).reshape(1, mla_h * LANE)
    ckv, kpe, k_heads, v_heads = _kv_proj(h2b, wkv, row(kv_norm), (cos_a, sin_a, cos_b, sin_b), wuk, wuv, ones_row,
                                          lora, rope, mla_h)

    wuq = mla_w_uq[0].reshape(-1, mla_h, nope + rope)
    wq_n, wq_r = wuq[:, :, :nope], wuq[:, :, nope:]
    pad_h = lambda w, off: jnp.pad(w, ((0, 0), (0, 0), (off, LANE - off - w.shape[2]))).reshape(w.shape[0], mla_h * LANE)
    w_q1 = (pad_h(wq_n, 0) + pad_h(wq_r, nope)).astype(BF16)
    w_q2 = pad_h(rot(wq_r), nope).astype(BF16)
    wdq = mla_w_dq[0].astype(BF16)
    qn_g = row(mla_q_norm[0])
    q_heads = _q_proj(h2b, wdq, qn_g, w_q1, w_q2, q_ctab, q_stab, mla_h)
    attn_p = _attn_prompt(q_heads, k_heads, v_heads, bsz, seq, meta_row0, mla_h)

    wukt = jnp.pad(jnp.transpose(kv_w_b[:, :, :nope], (1, 2, 0)), ((0, 0), (0, LANE - nope), (0, 0))).astype(BF16)
    w_pe = wq_r.reshape(-1, mla_h * rope).astype(BF16)
    w_pe_rot = rot(wq_r).reshape(-1, mla_h * rope).astype(BF16)
    cs = jnp.tile(cos_m[n_p:n_p + n_s], (1, mla_h))
    ss = jnp.tile(sin_m[n_p:n_p + n_s], (1, mla_h))
    qabs, qpe = _q_sample(h2b[n_p:n_p + n_s], wdq, qn_g, pad_h(wq_n, 0).astype(BF16), wukt, w_pe, w_pe_rot, cs, ss,
                          mla_h, lora, q_scale)
    ckv_s, kpe_s = ckv[n_p:n_p + n_s], kpe[n_p:n_p + n_s]
    new_pad = lambda a: jnp.pad(a.reshape(dec_b, dec_seq, -1), ((0, 0), (0, LANE - dec_seq), (0, 0))).astype(BF16)
    o_lat = _attn_paged(page_table, qabs.reshape(dec_b, dec_seq * mla_h, lora), qpe.reshape(dec_b, dec_seq * mla_h, rope),
                        new_pad(ckv_s), new_pad(kpe_s), cache_ckv, jnp.swapaxes(cache_kpe, 1, 2), mla_h, dec_seq)
    wuv_t = jnp.transpose(wuv_h, (1, 0, 2))
    wuv_pad = jnp.where(jnp.transpose(even, (1, 0, 2)), jnp.pad(wuv_t, ((0, 0), (0, 0), (0, LANE - v_dim))),
                        jnp.pad(wuv_t, ((0, 0), (0, 0), (LANE - v_dim, 0)))).astype(BF16)
    attn_s = _uv_sample(o_lat.reshape(n_s, mla_h * lora), wuv_pad, mla_h, lora)
    attn = jnp.concatenate([attn_p, attn_s, jnp.zeros((r_pad - meta_row0, attn_p.shape[1]), BF16)], axis=0)
    h3, h3b, h3_tiles = _proj_ln(attn, mla_w_o[0], h2, row(ln_g[1, 0]), row(ln_b[1, 0]), alpha, row_tiles=True)

    w_r = _pad_cols(moe_w_r[0], 0, LANE).astype(BF16)
    b_r = _pad_cols(moe_b_r[0].reshape(1, -1), 0, LANE)
    route = _router(h3b, w_r, b_r, n_experts)
    src, tile_expert, n_used, pos1, pos2 = _route_plan(route, n_experts, MOE_ROW_TILE)
    y_sorted = _moe_ffn(tile_expert, n_used, src, h3_tiles, moe_w13[0].astype(BF16), moe_w2[0].astype(BF16))
    h4 = _combine_ln(pos1, pos2, y_sorted, h3, route, row(ln_g[1, 1]), row(ln_b[1, 1]), alpha)

    y_prompt = h4[:n_p].reshape(bsz, seq, d)
    y_sample = h4[n_p:n_p + n_s].reshape(dec_b, dec_seq, d)

    def with_meta(a):
        meta = jnp.broadcast_to(a[meta_row0:r_real][None], (bsz, N_META, a.shape[1]))
        return jnp.concatenate([meta, a[:n_p].reshape(bsz, seq, -1)], axis=1)

    return (y_prompt, y_sample, st_p[None], with_meta(ckv), with_meta(kpe), st_s[None],
            ckv_s.reshape(dec_b, dec_seq, lora), kpe_s.reshape(dec_b, dec_seq, rope))
```
